```python
import jax
import jax.numpy as jnp
from jax import lax
import numpy as np

D_MODEL = 1024
BATCH = 16
SEQ = 2048
DEPTH = 1

HEAD_DIM = 64
ATTN_Q_HEADS = 8
ATTN_KV_HEADS = 2
ATTN_GROUP = ATTN_Q_HEADS // ATTN_KV_HEADS
WINDOW = 128
BLOCK = 128
RWKV_HEADS = 8
RWKV_HEAD_SIZE = 64
LORA_DECAY = 64
LORA_AAA = 64
LORA_GATE = 128
D_FF = 4 * D_MODEL
ATTN_Q_WIDTH = ATTN_Q_HEADS * HEAD_DIM
ATTN_KV_WIDTH = ATTN_KV_HEADS * HEAD_DIM
RWKV_WIDTH = RWKV_HEADS * RWKV_HEAD_SIZE
SPLIT_SIZES = (ATTN_Q_WIDTH, ATTN_KV_WIDTH, ATTN_KV_WIDTH, RWKV_WIDTH, RWKV_WIDTH, RWKV_WIDTH, LORA_DECAY, LORA_AAA, LORA_GATE, D_MODEL, D_MODEL)
IN_WIDTH = ATTN_Q_WIDTH + 2 * ATTN_KV_WIDTH + 3 * RWKV_WIDTH + LORA_DECAY + LORA_AAA + LORA_GATE + 2 * D_MODEL
RMS_EPS = 1e-6
GN_EPS = 64e-5
L2_EPS = 1e-12

kernel_name = "hybrid_swa_sink_rwkv7_gated_block"


def rmsnorm(z, gain):
    zf = z.astype(jnp.float32)
    return zf * lax.rsqrt(jnp.mean(zf * zf, axis=-1, keepdims=True) + RMS_EPS) * gain.astype(jnp.float32)


def token_shift(z, mu):
    prev = jnp.pad(z, ((0, 0), (1, 0), (0, 0)))[:, :-1]
    return z + (prev - z) * mu


def sliding_window_attention(q, k, v, q_gain, k_gain, sinks):
    b, t, _ = q.shape
    nb = t // BLOCK
    q = rmsnorm(q.reshape(b, nb, BLOCK, ATTN_KV_HEADS, ATTN_GROUP, HEAD_DIM), q_gain)
    k = rmsnorm(k.reshape(b, nb, BLOCK, ATTN_KV_HEADS, HEAD_DIM), k_gain)
    v = v.astype(jnp.float32).reshape(b, nb, BLOCK, ATTN_KV_HEADS, HEAD_DIM)

    def with_prev(z):
        prev = jnp.concatenate([jnp.zeros_like(z[:, :1]), z[:, :-1]], axis=1)
        return jnp.concatenate([prev, z], axis=2)

    kw, vw = with_prev(k), with_prev(v)
    scores = jnp.einsum('bnqkgd,bnskd->bnkgqs', q, kw) * (HEAD_DIM ** -0.5)
    qpos = jnp.arange(BLOCK)[:, None] + BLOCK
    spos = jnp.arange(2 * BLOCK)[None, :]
    rel = qpos - spos
    band = (rel >= 0) & (rel < WINDOW)
    exists = (spos >= BLOCK) | (jnp.arange(nb)[:, None, None] > 0)
    valid = (band[None] & exists)[None, :, None, None]
    scores = jnp.where(valid, scores, -jnp.inf)
    sink = sinks.astype(jnp.float32).reshape(1, 1, ATTN_KV_HEADS, ATTN_GROUP, 1, 1)
    m = jnp.maximum(jnp.max(scores, axis=-1, keepdims=True), sink)
    p = jnp.exp(scores - m)
    probs = p / (jnp.sum(p, axis=-1, keepdims=True) + jnp.exp(sink - m))
    out = jnp.einsum('bnkgqs,bnskd->bnqkgd', probs, vw)
    return out.reshape(b, t, ATTN_Q_WIDTH)


def wkv7_scan(r, w, k, v, a, b):
    def step(S, inp):
        r_t, w_t, k_t, v_t, a_t, b_t = inp
        sa = jnp.einsum('bhij,bhj->bhi', S, a_t)
        S = S * w_t[:, :, None, :] + sa[..., None] * b_t[:, :, None, :] + v_t[..., None] * k_t[:, :, None, :]
        return S, jnp.einsum('bhij,bhj->bhi', S, r_t)

    bsz, _, h, n = r.shape
    xs = tuple(jnp.swapaxes(z, 0, 1) for z in (r, w, k, v, a, b))
    S0 = jnp.zeros((bsz, h, n, n), jnp.float32)
    _, ys = lax.scan(step, S0, xs)
    return jnp.swapaxes(ys, 0, 1)


def heads(z):
    return z.reshape(*z.shape[:-1], RWKV_HEADS, RWKV_HEAD_SIZE)


def rwkv7_time_mix(r_p, k_p, v_p, w_p, a_p, g_p, mu_r, mu_k, mu_v, mu_w, mu_a, mu_g,
                   decay_bias, decay_up, aaa_bias, aaa_up, gate_up, k_k, k_a, r_k,
                   ln_x_gain, ln_x_bias):
    b, t, _ = r_p.shape
    r = token_shift(r_p, mu_r)
    k = token_shift(k_p, mu_k)
    v = token_shift(v_p, mu_v)
    w_log = -jax.nn.softplus(-(decay_bias + jnp.tanh(token_shift(w_p, mu_w)) @ decay_up)) - 0.5
    decay = jnp.exp(-jnp.exp(w_log.astype(jnp.float32)))
    a = jax.nn.sigmoid(aaa_bias + token_shift(a_p, mu_a) @ aaa_up)
    g = jax.nn.sigmoid(token_shift(g_p, mu_g)) @ gate_up
    kk = heads(k * k_k).astype(jnp.float32)
    kk = kk / jnp.maximum(jnp.sqrt(jnp.sum(kk * kk, axis=-1, keepdims=True)), L2_EPS)
    k = k * (1.0 + (a - 1.0) * k_a)
    r_h = heads(r).astype(jnp.float32)
    k_h = heads(k).astype(jnp.float32)
    v_h = heads(v).astype(jnp.float32)
    a_h = heads(a).astype(jnp.float32)
    y = wkv7_scan(r_h, heads(decay), k_h, v_h, -kk, kk * a_h)
    mean = jnp.mean(y, axis=-1, keepdims=True)
    var = jnp.mean(jnp.square(y - mean), axis=-1, keepdims=True)
    y = ((y - mean) * lax.rsqrt(var + GN_EPS)).reshape(b, t, RWKV_WIDTH) * ln_x_gain + ln_x_bias
    bonus = jnp.sum(r_h * k_h * r_k.reshape(RWKV_HEADS, RWKV_HEAD_SIZE), axis=-1, keepdims=True) * v_h
    return (y + bonus.reshape(b, t, RWKV_WIDTH)) * g


def hybrid_layer(h, norm1_gain, w_in, q_norm_gain, k_norm_gain, attn_sinks,
                 mu_r, mu_k, mu_v, mu_w, mu_a, mu_g, decay_bias, decay_up, aaa_bias, aaa_up,
                 gate_up, k_k, k_a, r_k, ln_x_gain, ln_x_bias, w_branch_attn, w_branch_rwkv,
                 w_out, norm2_gain, w_ff_in, w_ff_out):
    u = rmsnorm(h, norm1_gain)
    proj = u @ w_in
    split_at = np.cumsum(SPLIT_SIZES)[:-1].tolist()
    q, k, v, r_p, k_p, v_p, w_p, a_p, g_p, gate_a, gate_b = jnp.split(proj, split_at, axis=-1)
    y_attn = sliding_window_attention(q, k, v, q_norm_gain, k_norm_gain, attn_sinks) @ w_branch_attn
    y_rwkv = rwkv7_time_mix(r_p, k_p, v_p, w_p, a_p, g_p, mu_r, mu_k, mu_v, mu_w, mu_a, mu_g,
                            decay_bias, decay_up, aaa_bias, aaa_up, gate_up, k_k, k_a, r_k,
                            ln_x_gain, ln_x_bias) @ w_branch_rwkv
    mixed = jax.nn.sigmoid(gate_a) * y_attn + jax.nn.sigmoid(gate_b) * y_rwkv
    h = h + mixed @ w_out
    hidden = jnp.square(jax.nn.relu(rmsnorm(h, norm2_gain) @ w_ff_in))
    return h + hidden @ w_ff_out


def setup_inputs(seed: int = 0) -> dict:
    key = jax.random.key(seed)
    ks = jax.random.split(key, 28)
    f32 = jnp.float32

    def dense(k, fan_in, fan_out, scale=1.0):
        return jax.random.normal(k, (DEPTH, fan_in, fan_out), f32) * (scale * fan_in ** -0.5)

    def vec(k, n, mean, std):
        return mean + std * jax.random.normal(k, (DEPTH, n), f32)

    def unif(k, n):
        return jax.random.uniform(k, (DEPTH, n), f32)

    return {
        "x": jax.random.normal(ks[0], (BATCH, SEQ, D_MODEL), f32),
        "norm1_gain": vec(ks[1], D_MODEL, 1.0, 0.02),
        "w_in": dense(ks[2], D_MODEL, IN_WIDTH),
        "q_norm_gain": vec(ks[3], HEAD_DIM, 1.0, 0.02),
        "k_norm_gain": vec(ks[4], HEAD_DIM, 1.0, 0.02),
        "attn_sinks": vec(ks[5], ATTN_Q_HEADS, 0.0, 0.5),
        "mu_r": unif(ks[6], RWKV_WIDTH),
        "mu_k": unif(ks[7], RWKV_WIDTH),
        "mu_v": unif(ks[8], RWKV_WIDTH),
        "mu_w": unif(ks[9], LORA_DECAY),
        "mu_a": unif(ks[10], LORA_AAA),
        "mu_g": unif(ks[11], LORA_GATE),
        "decay_bias": vec(ks[12], RWKV_WIDTH, -1.0, 0.5),
        "decay_up": dense(ks[13], LORA_DECAY, RWKV_WIDTH, 0.5),
        "aaa_bias": vec(ks[14], RWKV_WIDTH, 0.0, 0.1),
        "aaa_up": dense(ks[15], LORA_AAA, RWKV_WIDTH),
        "gate_up": dense(ks[16], LORA_GATE, RWKV_WIDTH),
        "k_k": vec(ks[17], RWKV_WIDTH, 0.85, 0.05),
        "k_a": vec(ks[18], RWKV_WIDTH, 1.0, 0.05),
        "r_k": vec(ks[19], RWKV_WIDTH, 0.0, 0.1),
        "ln_x_gain": vec(ks[20], RWKV_WIDTH, 1.0, 0.02),
        "ln_x_bias": vec(ks[21], RWKV_WIDTH, 0.0, 0.02),
        "w_branch_attn": dense(ks[22], ATTN_Q_WIDTH, D_MODEL),
        "w_branch_rwkv": dense(ks[23], RWKV_WIDTH, D_MODEL),
        "w_out": dense(ks[24], D_MODEL, D_MODEL),
        "norm2_gain": vec(ks[25], D_MODEL, 1.0, 0.02),
        "w_ff_in": dense(ks[26], D_MODEL, D_FF),
        "w_ff_out": dense(ks[27], D_FF, D_MODEL),
    }


def reference(x, norm1_gain, w_in, q_norm_gain, k_norm_gain, attn_sinks,
              mu_r, mu_k, mu_v, mu_w, mu_a, mu_g, decay_bias, decay_up, aaa_bias, aaa_up,
              gate_up, k_k, k_a, r_k, ln_x_gain, ln_x_bias, w_branch_attn, w_branch_rwkv,
              w_out, norm2_gain, w_ff_in, w_ff_out):
    h = x.astype(jnp.float32)
    for l in range(DEPTH):
        h = hybrid_layer(h, norm1_gain[l], w_in[l], q_norm_gain[l], k_norm_gain[l], attn_sinks[l],
                         mu_r[l], mu_k[l], mu_v[l], mu_w[l], mu_a[l], mu_g[l],
                         decay_bias[l], decay_up[l], aaa_bias[l], aaa_up[l], gate_up[l],
                         k_k[l], k_a[l], r_k[l], ln_x_gain[l], ln_x_bias[l],
                         w_branch_attn[l], w_branch_rwkv[l], w_out[l],
                         norm2_gain[l], w_ff_in[l], w_ff_out[l])
    return h.astype(x.dtype)
```

```python
import functools
import math

import jax
import jax.numpy as jnp
from jax import lax
from jax.experimental import pallas as pl
from jax.experimental.pallas import tpu as pltpu

F32 = jnp.float32
BF16 = jnp.bfloat16

HEAD_DIM = 64
ATTN_Q_HEADS = 8
ATTN_KV_HEADS = 2
ATTN_GROUP = ATTN_Q_HEADS // ATTN_KV_HEADS
WINDOW = 128
BLOCK = 128
RWKV_HEADS = 8
RWKV_HEAD_SIZE = 64
LORA_DECAY = 64
LORA_AAA = 64
LORA_GATE = 128
ATTN_Q_WIDTH = ATTN_Q_HEADS * HEAD_DIM
ATTN_KV_WIDTH = ATTN_KV_HEADS * HEAD_DIM
RWKV_WIDTH = RWKV_HEADS * RWKV_HEAD_SIZE
RWKV_IN_WIDTH = 3 * RWKV_WIDTH + LORA_DECAY + LORA_AAA + LORA_GATE
RMS_EPS = 1e-6
GN_EPS = 64e-5
L2_EPS = 1e-12

CHUNK = 64
ROW_TILE = 512
ATTN_TILE = 512
VMEM_LIMIT = 56 * 1024 * 1024

_NT = (((1,), (1,)), ((), ()))
_TN = (((0,), (0,)), ((), ()))


def _bdot(a, b):
    return jnp.dot(a.astype(BF16), b.astype(BF16), preferred_element_type=F32)


def _bdot_nt(a, b):
    return lax.dot_general(a.astype(BF16), b.astype(BF16), _NT, preferred_element_type=F32)


def _bdot_tn(a, b):
    return lax.dot_general(a.astype(BF16), b.astype(BF16), _TN, preferred_element_type=F32)


def _rms(x, gain):
    return x * lax.rsqrt(jnp.mean(x * x, axis=-1, keepdims=True) + RMS_EPS) * gain


def _sigmoid(x):
    return 1.0 / (1.0 + jnp.exp(-x))


def _inproj_kernel(x_ref, g_ref, w_ref, q_ref, kv_ref, rw_ref):
    u = _rms(x_ref[...], g_ref[...])
    p = _bdot(u, w_ref[...])
    q_ref[...] = p[:, :ATTN_Q_WIDTH]
    kv_ref[...] = p[:, ATTN_Q_WIDTH:ATTN_Q_WIDTH + 2 * ATTN_KV_WIDTH]
    rw_ref[...] = p[:, ATTN_Q_WIDTH + 2 * ATTN_KV_WIDTH:]


def _inproj(x2, gain, w):
    n, d = x2.shape
    wid = w.shape[1]
    tm = min(ROW_TILE, n)
    return pl.pallas_call(
        _inproj_kernel,
        grid=(n // tm,),
        in_specs=[
            pl.BlockSpec((tm, d), lambda i: (i, 0)),
            pl.BlockSpec((1, d), lambda i: (0, 0)),
            pl.BlockSpec((d, wid), lambda i: (0, 0)),
        ],
        out_specs=[
            pl.BlockSpec((tm, ATTN_Q_WIDTH), lambda i: (i, 0)),
            pl.BlockSpec((tm, 2 * ATTN_KV_WIDTH), lambda i: (i, 0)),
            pl.BlockSpec((tm, RWKV_IN_WIDTH), lambda i: (i, 0)),
        ],
        out_shape=[
            jax.ShapeDtypeStruct((n, ATTN_Q_WIDTH), F32),
            jax.ShapeDtypeStruct((n, 2 * ATTN_KV_WIDTH), F32),
            jax.ShapeDtypeStruct((n, RWKV_IN_WIDTH), F32),
        ],
        compiler_params=pltpu.CompilerParams(
            dimension_semantics=("arbitrary",), vmem_limit_bytes=VMEM_LIMIT),
        name="inproj",
    )(x2, gain, w)


def _attn_kernel(q_ref, kv_ref, kvp_ref, qg_ref, kg_ref, sink_ref, o_ref, *, nblk):
    i = pl.program_id(1)
    q = q_ref[0]
    kv_all = jnp.concatenate([kvp_ref[0], kv_ref[0]], axis=0)
    qg = qg_ref[...] * (HEAD_DIM ** -0.5)
    kg = kg_ref[...]

    kn = [_rms(kv_all[:, h * HEAD_DIM:(h + 1) * HEAD_DIM], kg).astype(BF16)
          for h in range(ATTN_KV_HEADS)]
    vb = [kv_all[:, ATTN_KV_WIDTH + h * HEAD_DIM:ATTN_KV_WIDTH + (h + 1) * HEAD_DIM].astype(BF16)
          for h in range(ATTN_KV_HEADS)]

    rows = ATTN_GROUP * BLOCK
    qpos = (lax.broadcasted_iota(jnp.int32, (rows, 2 * BLOCK), 0) & (BLOCK - 1)) + BLOCK
    spos = lax.broadcasted_iota(jnp.int32, (rows, 2 * BLOCK), 1)
    rel = qpos - spos
    band = (rel >= 0) & (rel < WINDOW)
    band_first = band & ((spos >= BLOCK) | (i > 0))

    for j in range(nblk):
        valid = band_first if j == 0 else band
        qj = q[j * BLOCK:(j + 1) * BLOCK]
        outs = []
        for kh in range(ATTN_KV_HEADS):
            heads = [kh * ATTN_GROUP + g for g in range(ATTN_GROUP)]
            q4 = jnp.concatenate(
                [_rms(qj[:, h * HEAD_DIM:(h + 1) * HEAD_DIM], qg) for h in heads], axis=0)
            sink = jnp.concatenate(
                [jnp.full((BLOCK, 1), sink_ref[h], F32) for h in heads], axis=0)
            kw = kn[kh][j * BLOCK:(j + 2) * BLOCK]
            vw = vb[kh][j * BLOCK:(j + 2) * BLOCK]
            s = _bdot_nt(q4, kw)
            s = jnp.where(valid, s, -jnp.inf)
            m = jnp.maximum(jnp.max(s, axis=-1, keepdims=True), sink)
            p = jnp.exp(s - m)
            den = jnp.sum(p, axis=-1, keepdims=True) + jnp.exp(sink - m)
            o4 = _bdot(p, vw) / den
            outs.extend(o4[g * BLOCK:(g + 1) * BLOCK] for g in range(ATTN_GROUP))
        o_ref[0, j * BLOCK:(j + 1) * BLOCK, :] = jnp.concatenate(outs, axis=1).astype(o_ref.dtype)


def _attention(q, kv, q_gain, k_gain, sinks):
    b, t, _ = q.shape
    tq = min(ATTN_TILE, t)
    nblk = tq // BLOCK
    return pl.pallas_call(
        functools.partial(_attn_kernel, nblk=nblk),
        grid=(b, t // tq),
        in_specs=[
            pl.BlockSpec((1, tq, ATTN_Q_WIDTH), lambda bi, i: (bi, i, 0)),
            pl.BlockSpec((1, tq, 2 * ATTN_KV_WIDTH), lambda bi, i: (bi, i, 0)),
            pl.BlockSpec((1, BLOCK, 2 * ATTN_KV_WIDTH),
                         lambda bi, i: (bi, jnp.maximum(i * nblk - 1, 0), 0)),
            pl.BlockSpec((1, HEAD_DIM), lambda bi, i: (0, 0)),
            pl.BlockSpec((1, HEAD_DIM), lambda bi, i: (0, 0)),
            pl.BlockSpec(memory_space=pltpu.SMEM),
        ],
        out_specs=pl.BlockSpec((1, tq, ATTN_Q_WIDTH), lambda bi, i: (bi, i, 0)),
        out_shape=jax.ShapeDtypeStruct((b, t, ATTN_Q_WIDTH), BF16),
        compiler_params=pltpu.CompilerParams(
            dimension_semantics=("arbitrary", "arbitrary"), vmem_limit_bytes=VMEM_LIMIT),
        name="attention",
    )(q, kv, kv, q_gain, k_gain, sinks)


def _rwkv_kernel(xin_ref, mu_ref, dbias_ref, dup_ref, abias_ref, aup_ref, gup_ref,
                 kk_ref, ka_ref, rk_ref, lng_ref, lnb_ref, o_ref,
                 prev_ref, state_ref, den_ref, y_ref, bonus_ref):
    c = CHUNK
    n = RWKV_HEAD_SIZE
    w = RWKV_WIDTH

    @pl.when(pl.program_id(1) == 0)
    def _():
        prev_ref[...] = jnp.zeros_like(prev_ref)
        state_ref[...] = jnp.zeros_like(state_ref)

    xin = xin_ref[0]
    row = lax.broadcasted_iota(jnp.int32, (c, 1), 0)
    prev = jnp.where(row == 0, prev_ref[...], pltpu.roll(xin, 1, 0))
    prev_ref[...] = xin[c - 1:c, :]
    xs = xin + (prev - xin) * mu_ref[...]

    r = xs[:, 0:w]
    k = xs[:, w:2 * w]
    v = xs[:, 2 * w:3 * w]
    w_lora = xs[:, 3 * w:3 * w + LORA_DECAY]
    a_lora = xs[:, 3 * w + LORA_DECAY:3 * w + LORA_DECAY + LORA_AAA]
    g_lora = xs[:, 3 * w + LORA_DECAY + LORA_AAA:]

    d = dbias_ref[...] + _bdot(jnp.tanh(w_lora), dup_ref[...])
    lw = (-math.exp(-0.5)) * _sigmoid(d)
    a_sig = _sigmoid(abias_ref[...] + _bdot(a_lora, aup_ref[...]))
    gate = _bdot(_sigmoid(g_lora), gup_ref[...])

    kkr = k * kk_ref[...]
    k2 = k * (1.0 + (a_sig - 1.0) * ka_ref[...])
    rk = r * k2 * rk_ref[...]
    for h in range(RWKV_HEADS):
        sl = slice(h * n, (h + 1) * n)
        kh = kkr[:, sl]
        nrm = jnp.sqrt(jnp.sum(kh * kh, axis=-1, keepdims=True))
        den_ref[:, sl] = jnp.broadcast_to(jnp.maximum(nrm, L2_EPS), (c, n))
        bonus_ref[:, sl] = jnp.sum(rk[:, sl], axis=-1, keepdims=True) * v[:, sl]
    kk = kkr / den_ref[...]
    a_s = -kk
    b_s = kk * a_sig

    ti = lax.broadcasted_iota(jnp.int32, (c, c), 0)
    si = lax.broadcasted_iota(jnp.int32, (c, c), 1)
    lower = ti >= si
    strict = ti > si
    eye = ti == si
    cl = jnp.dot(lower.astype(F32), lw, preferred_element_type=F32,
                 precision=lax.Precision.HIGHEST)
    e_c = jnp.exp(cl)
    e_neg = jnp.exp(-cl)
    at = (a_s * jnp.exp(cl - lw)).astype(BF16)
    rt_f = r * e_c
    rt = rt_f.astype(BF16)
    bt = b_s * e_neg
    kt = k2 * e_neg
    g_c = e_c[c - 1:c, :]
    bh = (bt * g_c).astype(BF16)
    khat = (kt * g_c).astype(BF16)
    bt = bt.astype(BF16)
    kt = kt.astype(BF16)
    vb = v.astype(BF16)

    for h in range(RWKV_HEADS):
        sl = slice(h * n, (h + 1) * n)
        at_h, rt_h, bt_h, kt_h, v_h = at[:, sl], rt[:, sl], bt[:, sl], kt[:, sl], vb[:, sl]
        a_ab = jnp.where(strict, _bdot_nt(at_h, bt_h), 0.0)
        a_ak = jnp.where(strict, _bdot_nt(at_h, kt_h), 0.0)
        a_rb = jnp.where(lower, _bdot_nt(rt_h, bt_h), 0.0)
        a_rk = jnp.where(lower, _bdot_nt(rt_h, kt_h), 0.0)
        tinv = jnp.where(eye, 1.0, a_ab)
        apow = a_ab
        for _ in range(int(math.log2(c)) - 1):
            apow = _bdot(apow, apow)
            tinv = tinv + _bdot(tinv, apow)
        wmat = _bdot(tinv, at_h)
        u0 = _bdot(tinv, _bdot(a_ak, v_h))
        qmat = rt_f[:, sl] + _bdot(a_rb, wmat)
        y0 = _bdot(a_rb, u0) + _bdot(a_rk, v_h)
        gdiag = jnp.where(eye, jnp.broadcast_to(g_c[:, sl], (n, n)), 0.0)
        mmat = gdiag + _bdot_tn(bh[:, sl], wmat)
        n0 = _bdot_tn(bh[:, sl], u0) + _bdot_tn(khat[:, sl], v_h)

        h0 = state_ref[:, sl]
        y = _bdot(qmat, h0) + y0
        state_ref[:, sl] = _bdot(mmat, h0) + n0

        mean = jnp.mean(y, axis=-1, keepdims=True)
        yc = y - mean
        var = jnp.mean(yc * yc, axis=-1, keepdims=True)
        y_ref[:, sl] = yc * lax.rsqrt(var + GN_EPS)

    out = (y_ref[...] * lng_ref[...] + lnb_ref[...] + bonus_ref[...]) * gate
    o_ref[0] = out.astype(o_ref.dtype)


def _rwkv(rw, mu, decay_bias, decay_up, aaa_bias, aaa_up, gate_up, k_k, k_a, r_k, ln_g, ln_b):
    b, t, _ = rw.shape
    c = CHUNK
    w = RWKV_WIDTH

    def full(shape):
        return pl.BlockSpec(shape, lambda bi, ti: (0, 0))

    return pl.pallas_call(
        _rwkv_kernel,
        grid=(b, t // c),
        in_specs=[
            pl.BlockSpec((1, c, RWKV_IN_WIDTH), lambda bi, ti: (bi, ti, 0)),
            full((1, RWKV_IN_WIDTH)),
            full((1, w)), full((LORA_DECAY, w)),
            full((1, w)), full((LORA_AAA, w)),
            full((LORA_GATE, w)),
            full((1, w)), full((1, w)), full((1, w)), full((1, w)), full((1, w)),
        ],
        out_specs=pl.BlockSpec((1, c, w), lambda bi, ti: (bi, ti, 0)),
        out_shape=jax.ShapeDtypeStruct((b, t, w), BF16),
        scratch_shapes=[
            pltpu.VMEM((1, RWKV_IN_WIDTH), F32),
            pltpu.VMEM((RWKV_HEAD_SIZE, w), F32),
            pltpu.VMEM((c, w), F32),
            pltpu.VMEM((c, w), F32),
            pltpu.VMEM((c, w), F32),
        ],
        compiler_params=pltpu.CompilerParams(
            dimension_semantics=("arbitrary", "arbitrary"), vmem_limit_bytes=VMEM_LIMIT),
        name="rwkv",
    )(rw, mu, decay_bias, decay_up, aaa_bias, aaa_up, gate_up, k_k, k_a, r_k, ln_g, ln_b)


def _merge_kernel(x_ref, attn_ref, rwkv_ref, g_ref, wg_ref, wba_ref, wbr_ref, wo_ref, h_ref):
    x = x_ref[...]
    d = x.shape[1]
    u = _rms(x, g_ref[...])
    gates = _bdot(u, wg_ref[...])
    y_attn = jnp.dot(attn_ref[...], wba_ref[...], preferred_element_type=F32)
    y_rwkv = jnp.dot(rwkv_ref[...], wbr_ref[...], preferred_element_type=F32)
    mixed = _sigmoid(gates[:, :d]) * y_attn + _sigmoid(gates[:, d:]) * y_rwkv
    h_ref[...] = x + _bdot(mixed, wo_ref[...])


def _merge(x2, attn, rwkv, gain, w_gate, w_ba, w_br, w_out):
    n, d = x2.shape
    tm = min(ROW_TILE, n)

    def full(a):
        return pl.BlockSpec(a.shape, lambda i: (0, 0))

    return pl.pallas_call(
        _merge_kernel,
        grid=(n // tm,),
        in_specs=[
            pl.BlockSpec((tm, d), lambda i: (i, 0)),
            pl.BlockSpec((tm, attn.shape[1]), lambda i: (i, 0)),
            pl.BlockSpec((tm, rwkv.shape[1]), lambda i: (i, 0)),
            full(gain), full(w_gate), full(w_ba), full(w_br), full(w_out),
        ],
        out_specs=pl.BlockSpec((tm, d), lambda i: (i, 0)),
        out_shape=jax.ShapeDtypeStruct((n, d), F32),
        compiler_params=pltpu.CompilerParams(
            dimension_semantics=("arbitrary",), vmem_limit_bytes=VMEM_LIMIT),
        name="merge",
    )(x2, attn, rwkv, gain, w_gate, w_ba, w_br, w_out)


def _mlp_kernel(h_ref, g_ref, w1_ref, w2_ref, o_ref):
    h = h_ref[...]
    hidden = jnp.square(jnp.maximum(_bdot(_rms(h, g_ref[...]), w1_ref[...]), 0.0))
    o_ref[...] = h + _bdot(hidden, w2_ref[...])


def _mlp(h2, gain, w1, w2):
    n, d = h2.shape
    tm = min(ROW_TILE // 2, n)

    def full(a):
        return pl.BlockSpec(a.shape, lambda i: (0, 0))

    return pl.pallas_call(
        _mlp_kernel,
        grid=(n // tm,),
        in_specs=[pl.BlockSpec((tm, d), lambda i: (i, 0)), full(gain), full(w1), full(w2)],
        out_specs=pl.BlockSpec((tm, d), lambda i: (i, 0)),
        out_shape=jax.ShapeDtypeStruct((n, d), F32),
        compiler_params=pltpu.CompilerParams(
            dimension_semantics=("arbitrary",), vmem_limit_bytes=VMEM_LIMIT),
        name="mlp",
    )(h2, gain, w1, w2)


def _layer(h, norm1_gain, w_in, q_norm_gain, k_norm_gain, attn_sinks,
           mu_r, mu_k, mu_v, mu_w, mu_a, mu_g, decay_bias, decay_up, aaa_bias, aaa_up,
           gate_up, k_k, k_a, r_k, ln_x_gain, ln_x_bias, w_branch_attn, w_branch_rwkv,
           w_out, norm2_gain, w_ff_in, w_ff_out):
    b, t, d = h.shape
    x2 = h.reshape(b * t, d)
    row = lambda a: a.reshape(1, -1).astype(F32)
    n_proj = ATTN_Q_WIDTH + 2 * ATTN_KV_WIDTH + RWKV_IN_WIDTH

    q, kv, rw = _inproj(x2, row(norm1_gain), w_in[:, :n_proj].astype(BF16))
    attn = _attention(q.reshape(b, t, -1), kv.reshape(b, t, -1),
                      row(q_norm_gain), row(k_norm_gain), attn_sinks.astype(F32))
    mu = jnp.concatenate([mu_r, mu_k, mu_v, mu_w, mu_a, mu_g]).reshape(1, -1).astype(F32)
    rwkv = _rwkv(rw.reshape(b, t, -1), mu, row(decay_bias), decay_up.astype(BF16),
                 row(aaa_bias), aaa_up.astype(BF16), gate_up.astype(BF16),
                 row(k_k), row(k_a), row(r_k), row(ln_x_gain), row(ln_x_bias))
    h2 = _merge(x2, attn.reshape(b * t, -1), rwkv.reshape(b * t, -1), row(norm1_gain),
                w_in[:, n_proj:].astype(BF16), w_branch_attn.astype(BF16),
                w_branch_rwkv.astype(BF16), w_out.astype(BF16))
    out = _mlp(h2, row(norm2_gain), w_ff_in.astype(BF16), w_ff_out.astype(BF16))
    return out.reshape(b, t, d)


def kernel(x, norm1_gain, w_in, q_norm_gain, k_norm_gain, attn_sinks, mu_r, mu_k, mu_v, mu_w, mu_a, mu_g, decay_bias, decay_up, aaa_bias, aaa_up, gate_up, k_k, k_a, r_k, ln_x_gain, ln_x_bias, w_branch_attn, w_branch_rwkv, w_out, norm2_gain, w_ff_in, w_ff_out):
    h = x.astype(F32)
    params = (norm1_gain, w_in, q_norm_gain, k_norm_gain, attn_sinks, mu_r, mu_k, mu_v, mu_w,
              mu_a, mu_g, decay_bias, decay_up, aaa_bias, aaa_up, gate_up, k_k, k_a, r_k,
              ln_x_gain, ln_x_bias, w_branch_attn, w_branch_rwkv, w_out, norm2_gain,
              w_ff_in, w_ff_out)
    for l in range(norm1_gain.shape[0]):
        h = _layer(h, *(p[l] for p in params))
    return h.astype(x.dtype)
```

```python
import functools
import math

import jax
import jax.numpy as jnp
from jax import lax
from jax.experimental import pallas as pl
from jax.experimental.pallas import tpu as pltpu

F32 = jnp.float32
BF16 = jnp.bfloat16

HEAD_DIM = 64
ATTN_Q_HEADS = 8
ATTN_KV_HEADS = 2
ATTN_GROUP = ATTN_Q_HEADS // ATTN_KV_HEADS
WINDOW = 128
BLOCK = 128
RWKV_HEADS = 8
RWKV_HEAD_SIZE = 64
LORA_DECAY = 64
LORA_AAA = 64
LORA_GATE = 128
ATTN_Q_WIDTH = ATTN_Q_HEADS * HEAD_DIM
ATTN_KV_WIDTH = ATTN_KV_HEADS * HEAD_DIM
RWKV_WIDTH = RWKV_HEADS * RWKV_HEAD_SIZE
RWKV_IN_WIDTH = 3 * RWKV_WIDTH + LORA_DECAY + LORA_AAA + LORA_GATE
RMS_EPS = 1e-6
GN_EPS = 64e-5
L2_EPS = 1e-12

CHUNK = 64
ROW_TILE = 512
ATTN_TILE = 512
RWKV_TILE = 256
VMEM_LIMIT = 56 * 1024 * 1024

_NT = (((1,), (1,)), ((), ()))
_TN = (((0,), (0,)), ((), ()))


def _bdot(a, b):
    return jnp.dot(a.astype(BF16), b.astype(BF16), preferred_element_type=F32)


def _bdot_nt(a, b):
    return lax.dot_general(a.astype(BF16), b.astype(BF16), _NT, preferred_element_type=F32)


def _rms(x, gain):
    return x * lax.rsqrt(jnp.mean(x * x, axis=-1, keepdims=True) + RMS_EPS) * gain


def _sigmoid(x):
    return 1.0 / (1.0 + jnp.exp(-x))


def _inproj_kernel(x_ref, g_ref, w_ref, q_ref, kv_ref, rw_ref):
    u = _rms(x_ref[...], g_ref[...])
    p = _bdot(u, w_ref[...])
    q_ref[...] = p[:, :ATTN_Q_WIDTH]
    kv_ref[...] = p[:, ATTN_Q_WIDTH:ATTN_Q_WIDTH + 2 * ATTN_KV_WIDTH]
    rw_ref[...] = p[:, ATTN_Q_WIDTH + 2 * ATTN_KV_WIDTH:]


def _inproj(x2, gain, w):
    n, d = x2.shape
    wid = w.shape[1]
    tm = min(ROW_TILE, n)
    return pl.pallas_call(
        _inproj_kernel,
        grid=(n // tm,),
        in_specs=[
            pl.BlockSpec((tm, d), lambda i: (i, 0)),
            pl.BlockSpec((1, d), lambda i: (0, 0)),
            pl.BlockSpec((d, wid), lambda i: (0, 0)),
        ],
        out_specs=[
            pl.BlockSpec((tm, ATTN_Q_WIDTH), lambda i: (i, 0)),
            pl.BlockSpec((tm, 2 * ATTN_KV_WIDTH), lambda i: (i, 0)),
            pl.BlockSpec((tm, RWKV_IN_WIDTH), lambda i: (i, 0)),
        ],
        out_shape=[
            jax.ShapeDtypeStruct((n, ATTN_Q_WIDTH), F32),
            jax.ShapeDtypeStruct((n, 2 * ATTN_KV_WIDTH), F32),
            jax.ShapeDtypeStruct((n, RWKV_IN_WIDTH), F32),
        ],
        compiler_params=pltpu.CompilerParams(
            dimension_semantics=("arbitrary",), vmem_limit_bytes=VMEM_LIMIT),
        name="inproj",
    )(x2, gain, w)


def _attn_kernel(q_ref, kv_ref, kvp_ref, qg_ref, kg_ref, sink_ref, o_ref, *, nblk):
    i = pl.program_id(1)
    q = q_ref[0]
    kv_all = jnp.concatenate([kvp_ref[0], kv_ref[0]], axis=0)
    qg = qg_ref[...] * (HEAD_DIM ** -0.5)
    kg = kg_ref[...]

    kn = [_rms(kv_all[:, h * HEAD_DIM:(h + 1) * HEAD_DIM], kg).astype(BF16)
          for h in range(ATTN_KV_HEADS)]
    vb = [kv_all[:, ATTN_KV_WIDTH + h * HEAD_DIM:ATTN_KV_WIDTH + (h + 1) * HEAD_DIM].astype(BF16)
          for h in range(ATTN_KV_HEADS)]

    rows = ATTN_GROUP * BLOCK
    qpos = (lax.broadcasted_iota(jnp.int32, (rows, 2 * BLOCK), 0) & (BLOCK - 1)) + BLOCK
    spos = lax.broadcasted_iota(jnp.int32, (rows, 2 * BLOCK), 1)
    rel = qpos - spos
    band = (rel >= 0) & (rel < WINDOW)
    band_first = band & ((spos >= BLOCK) | (i > 0))

    for j in range(nblk):
        valid = band_first if j == 0 else band
        qj = q[j * BLOCK:(j + 1) * BLOCK]
        outs = []
        for kh in range(ATTN_KV_HEADS):
            heads = [kh * ATTN_GROUP + g for g in range(ATTN_GROUP)]
            q4 = jnp.concatenate(
                [_rms(qj[:, h * HEAD_DIM:(h + 1) * HEAD_DIM], qg) for h in heads], axis=0)
            sink = jnp.concatenate(
                [jnp.full((BLOCK, 1), sink_ref[h], F32) for h in heads], axis=0)
            kw = kn[kh][j * BLOCK:(j + 2) * BLOCK]
            vw = vb[kh][j * BLOCK:(j + 2) * BLOCK]
            s = _bdot_nt(q4, kw)
            s = jnp.where(valid, s, -jnp.inf)
            m = jnp.maximum(jnp.max(s, axis=-1, keepdims=True), sink)
            p = jnp.exp(s - m)
            den = jnp.sum(p, axis=-1, keepdims=True) + jnp.exp(sink - m)
            o4 = _bdot(p, vw) / den
            outs.extend(o4[g * BLOCK:(g + 1) * BLOCK] for g in range(ATTN_GROUP))
        o_ref[0, j * BLOCK:(j + 1) * BLOCK, :] = jnp.concatenate(outs, axis=1).astype(o_ref.dtype)


def _attention(q, kv, q_gain, k_gain, sinks):
    b, t, _ = q.shape
    tq = min(ATTN_TILE, t)
    nblk = tq // BLOCK
    return pl.pallas_call(
        functools.partial(_attn_kernel, nblk=nblk),
        grid=(b, t // tq),
        in_specs=[
            pl.BlockSpec((1, tq, ATTN_Q_WIDTH), lambda bi, i: (bi, i, 0)),
            pl.BlockSpec((1, tq, 2 * ATTN_KV_WIDTH), lambda bi, i: (bi, i, 0)),
            pl.BlockSpec((1, BLOCK, 2 * ATTN_KV_WIDTH),
                         lambda bi, i: (bi, jnp.maximum(i * nblk - 1, 0), 0)),
            pl.BlockSpec((1, HEAD_DIM), lambda bi, i: (0, 0)),
            pl.BlockSpec((1, HEAD_DIM), lambda bi, i: (0, 0)),
            pl.BlockSpec(memory_space=pltpu.SMEM),
        ],
        out_specs=pl.BlockSpec((1, tq, ATTN_Q_WIDTH), lambda bi, i: (bi, i, 0)),
        out_shape=jax.ShapeDtypeStruct((b, t, ATTN_Q_WIDTH), BF16),
        compiler_params=pltpu.CompilerParams(
            dimension_semantics=("arbitrary", "arbitrary"), vmem_limit_bytes=VMEM_LIMIT),
        name="attention",
    )(q, kv, kv, q_gain, k_gain, sinks)


def _pair_blockdiag(x, even):
    zero = jnp.zeros_like(x)
    return jnp.concatenate([jnp.where(even, x, zero), jnp.where(even, zero, x)], axis=0)


def _pair_sum(x, even):
    s_e = jnp.sum(jnp.where(even, x, 0.0), axis=-1, keepdims=True)
    s_o = jnp.sum(jnp.where(even, 0.0, x), axis=-1, keepdims=True)
    return jnp.where(even, s_e, s_o)


def _rwkv_kernel(xin_ref, mu_ref, dbias_ref, dup_ref, abias_ref, aup_ref, gup_ref,
                 kk_ref, ka_ref, rk_ref, lng_ref, lnb_ref, o_ref, prev_ref, state_ref):
    c = CHUNK
    hp = 2 * RWKV_HEAD_SIZE
    w = RWKV_WIDTH
    tb = xin_ref.shape[1]
    nc = tb // c
    npair = w // hp

    @pl.when(pl.program_id(1) == 0)
    def _():
        prev_ref[...] = jnp.zeros_like(prev_ref)
        state_ref[...] = jnp.zeros_like(state_ref)

    xin = xin_ref[0]
    row = lax.broadcasted_iota(jnp.int32, (tb, 1), 0)
    prev = jnp.where(row == 0, prev_ref[...], pltpu.roll(xin, 1, 0))
    prev_ref[...] = xin[tb - 1:tb, :]
    xs = xin + (prev - xin) * mu_ref[...]

    r = xs[:, 0:w]
    k = xs[:, w:2 * w]
    v = xs[:, 2 * w:3 * w]
    w_lora = xs[:, 3 * w:3 * w + LORA_DECAY]
    a_lora = xs[:, 3 * w + LORA_DECAY:3 * w + LORA_DECAY + LORA_AAA]
    g_lora = xs[:, 3 * w + LORA_DECAY + LORA_AAA:]

    d = dbias_ref[...] + _bdot(jnp.tanh(w_lora), dup_ref[...])
    lw = (-math.exp(-0.5)) * _sigmoid(d)
    a_sig = _sigmoid(abias_ref[...] + _bdot(a_lora, aup_ref[...]))
    gate = _bdot(_sigmoid(g_lora), gup_ref[...])

    kkr = k * kk_ref[...]
    k2 = k * (1.0 + (a_sig - 1.0) * ka_ref[...])
    rk = r * k2 * rk_ref[...]
    even_t = lax.broadcasted_iota(jnp.int32, (tb, hp), 1) < RWKV_HEAD_SIZE
    cols = [slice(p * hp, (p + 1) * hp) for p in range(npair)]
    den = jnp.concatenate(
        [jnp.maximum(jnp.sqrt(_pair_sum(kkr[:, cs] * kkr[:, cs], even_t)), L2_EPS) for cs in cols],
        axis=1)
    bonus = jnp.concatenate([_pair_sum(rk[:, cs], even_t) for cs in cols], axis=1) * v
    kk = kkr / den
    a_s = -kk
    b_s = kk * a_sig

    ti = lax.broadcasted_iota(jnp.int32, (c, c), 0)
    si = lax.broadcasted_iota(jnp.int32, (c, c), 1)
    lower_f = (ti >= si).astype(F32)
    cl = jnp.concatenate(
        [jnp.dot(lower_f, lw[ci * c:(ci + 1) * c], preferred_element_type=F32,
                 precision=lax.Precision.HIGHEST) for ci in range(nc)], axis=0)
    e_c = jnp.exp(cl)
    e_neg = jnp.exp(-cl)
    g_rows = [e_c[ci * c + c - 1:ci * c + c, :] for ci in range(nc)]
    g_full = jnp.concatenate([jnp.broadcast_to(g, (c, w)) for g in g_rows], axis=0)
    at = a_s * jnp.exp(cl - lw)
    rt = r * e_c
    bt = b_s * e_neg
    kt = k2 * e_neg
    bh = (bt * g_full).astype(BF16)
    khat = (kt * g_full).astype(BF16)
    at_b, rt_b, bt_b, kt_b, v_b = (z.astype(BF16) for z in (at, rt, bt, kt, v))

    even = lax.broadcasted_iota(jnp.int32, (c, hp), 1) < RWKV_HEAD_SIZE
    gi = lax.broadcasted_iota(jnp.int32, (2 * hp, 2 * hp), 0)
    gj = lax.broadcasted_iota(jnp.int32, (2 * hp, 2 * hp), 1)
    g_t, g_s = gi & (c - 1), gj & (c - 1)
    g_mask = (g_t > g_s) | ((gi >= hp) & (g_t == g_s))
    pi = lax.broadcasted_iota(jnp.int32, (hp, hp), 0)
    pj = lax.broadcasted_iota(jnp.int32, (hp, hp), 1)
    eye = pi == pj

    units = [(ci, p) for ci in range(nc) for p in range(npair)]

    def blk(x, ci, p):
        return _pair_blockdiag(x[ci * c:(ci + 1) * c, cols[p]], even)

    gm = []
    for ci, p in units:
        lhs = jnp.concatenate([blk(at_b, ci, p), blk(rt_b, ci, p)], axis=0)
        rhs = jnp.concatenate([blk(bt_b, ci, p), blk(kt_b, ci, p)], axis=0)
        gm.append(jnp.where(g_mask, lax.dot_general(lhs, rhs, _NT, preferred_element_type=F32), 0.0))
    a_pow = [g[:hp, :hp].astype(BF16) for g in gm]
    t_inv = [jnp.where(eye, 1.0, g[:hp, :hp]) for g in gm]
    a_pow = [jnp.dot(a, a, preferred_element_type=F32).astype(BF16) for a in a_pow]
    for _ in range(int(math.log2(c)) - 2):
        res = [jnp.dot(a, jnp.concatenate([t.astype(BF16), a], axis=1), preferred_element_type=F32)
               for a, t in zip(a_pow, t_inv)]
        t_inv = [t + z[:, :hp] for t, z in zip(t_inv, res)]
        a_pow = [z[:, hp:].astype(BF16) for z in res]
    t_inv = [t + jnp.dot(a, t.astype(BF16), preferred_element_type=F32)
             for a, t in zip(a_pow, t_inv)]
    wx = [jnp.dot(t.astype(BF16),
                  jnp.concatenate([blk(at_b, ci, p), g[:hp, hp:].astype(BF16)], axis=1),
                  preferred_element_type=F32)
          for (ci, p), t, g in zip(units, t_inv, gm)]
    v_bd = [blk(v_b, ci, p) for ci, p in units]
    u0 = [jnp.dot(z[:, hp:].astype(BF16), vb, preferred_element_type=F32) for z, vb in zip(wx, v_bd)]
    rhs_ab = [jnp.concatenate(
        [jnp.concatenate([z[:, :hp].astype(BF16), u.astype(BF16)], axis=1),
         jnp.concatenate([jnp.zeros_like(vb), vb], axis=1)], axis=0)
        for z, u, vb in zip(wx, u0, v_bd)]
    ab_y = [jnp.dot(g[hp:, :].astype(BF16), rz, preferred_element_type=F32)
            for g, rz in zip(gm, rhs_ab)]
    ab_h = [lax.dot_general(jnp.concatenate([blk(bh, ci, p), blk(khat, ci, p)], axis=0), rz, _TN,
                            preferred_element_type=F32)
            for (ci, p), rz in zip(units, rhs_ab)]

    state = [state_ref[:, cs] for cs in cols]
    y_cols = [[None] * npair for _ in range(nc)]
    for idx, (ci, p) in enumerate(units):
        qmat = blk(rt, ci, p) + ab_y[idx][:, :hp]
        g_diag = jnp.where(eye, jnp.broadcast_to(g_rows[ci][:, cols[p]], (hp, hp)), 0.0)
        mmat = g_diag + ab_h[idx][:, :hp]
        ys = jnp.dot(jnp.concatenate([qmat, mmat], axis=0).astype(BF16), state[p].astype(BF16),
                     preferred_element_type=F32)
        y_bd = ys[:hp] + ab_y[idx][:, hp:]
        state[p] = ys[hp:] + ab_h[idx][:, hp:]
        y = y_bd[:c] + y_bd[c:]
        mean = _pair_sum(y, even) * (1.0 / RWKV_HEAD_SIZE)
        yc = y - mean
        var = _pair_sum(yc * yc, even) * (1.0 / RWKV_HEAD_SIZE)
        y_cols[ci][p] = yc * lax.rsqrt(var + GN_EPS)
    for p in range(npair):
        state_ref[:, cols[p]] = state[p]

    yn = jnp.concatenate([jnp.concatenate(yc_, axis=1) for yc_ in y_cols], axis=0)
    out = (yn * lng_ref[...] + lnb_ref[...] + bonus) * gate
    o_ref[0] = out.astype(o_ref.dtype)


def _rwkv(rw, mu, decay_bias, decay_up, aaa_bias, aaa_up, gate_up, k_k, k_a, r_k, ln_g, ln_b):
    b, t, _ = rw.shape
    tb = min(RWKV_TILE, t)
    w = RWKV_WIDTH

    def full(shape):
        return pl.BlockSpec(shape, lambda bi, ti: (0, 0))

    return pl.pallas_call(
        _rwkv_kernel,
        grid=(b, t // tb),
        in_specs=[
            pl.BlockSpec((1, tb, RWKV_IN_WIDTH), lambda bi, ti: (bi, ti, 0)),
            full((1, RWKV_IN_WIDTH)),
            full((1, w)), full((LORA_DECAY, w)),
            full((1, w)), full((LORA_AAA, w)),
            full((LORA_GATE, w)),
            full((1, w)), full((1, w)), full((1, w)), full((1, w)), full((1, w)),
        ],
        out_specs=pl.BlockSpec((1, tb, w), lambda bi, ti: (bi, ti, 0)),
        out_shape=jax.ShapeDtypeStruct((b, t, w), BF16),
        scratch_shapes=[
            pltpu.VMEM((1, RWKV_IN_WIDTH), F32),
            pltpu.VMEM((2 * RWKV_HEAD_SIZE, w), F32),
        ],
        compiler_params=pltpu.CompilerParams(
            dimension_semantics=("arbitrary", "arbitrary"), vmem_limit_bytes=VMEM_LIMIT),
        name="rwkv",
    )(rw, mu, decay_bias, decay_up, aaa_bias, aaa_up, gate_up, k_k, k_a, r_k, ln_g, ln_b)


def _merge_kernel(x_ref, attn_ref, rwkv_ref, g_ref, wg_ref, wba_ref, wbr_ref, wo_ref, h_ref):
    x = x_ref[...]
    d = x.shape[1]
    u = _rms(x, g_ref[...])
    gates = _bdot(u, wg_ref[...])
    y_attn = jnp.dot(attn_ref[...], wba_ref[...], preferred_element_type=F32)
    y_rwkv = jnp.dot(rwkv_ref[...], wbr_ref[...], preferred_element_type=F32)
    mixed = _sigmoid(gates[:, :d]) * y_attn + _sigmoid(gates[:, d:]) * y_rwkv
    h_ref[...] = x + _bdot(mixed, wo_ref[...])


def _merge(x2, attn, rwkv, gain, w_gate, w_ba, w_br, w_out):
    n, d = x2.shape
    tm = min(ROW_TILE, n)

    def full(a):
        return pl.BlockSpec(a.shape, lambda i: (0, 0))

    return pl.pallas_call(
        _merge_kernel,
        grid=(n // tm,),
        in_specs=[
            pl.BlockSpec((tm, d), lambda i: (i, 0)),
            pl.BlockSpec((tm, attn.shape[1]), lambda i: (i, 0)),
            pl.BlockSpec((tm, rwkv.shape[1]), lambda i: (i, 0)),
            full(gain), full(w_gate), full(w_ba), full(w_br), full(w_out),
        ],
        out_specs=pl.BlockSpec((tm, d), lambda i: (i, 0)),
        out_shape=jax.ShapeDtypeStruct((n, d), F32),
        compiler_params=pltpu.CompilerParams(
            dimension_semantics=("arbitrary",), vmem_limit_bytes=VMEM_LIMIT),
        name="merge",
    )(x2, attn, rwkv, gain, w_gate, w_ba, w_br, w_out)


def _mlp_kernel(h_ref, g_ref, w1_ref, w2_ref, o_ref):
    h = h_ref[...]
    hidden = jnp.square(jnp.maximum(_bdot(_rms(h, g_ref[...]), w1_ref[...]), 0.0))
    o_ref[...] = h + _bdot(hidden, w2_ref[...])


def _mlp(h2, gain, w1, w2):
    n, d = h2.shape
    tm = min(ROW_TILE // 2, n)

    def full(a):
        return pl.BlockSpec(a.shape, lambda i: (0, 0))

    return pl.pallas_call(
        _mlp_kernel,
        grid=(n // tm,),
        in_specs=[pl.BlockSpec((tm, d), lambda i: (i, 0)), full(gain), full(w1), full(w2)],
        out_specs=pl.BlockSpec((tm, d), lambda i: (i, 0)),
        out_shape=jax.ShapeDtypeStruct((n, d), F32),
        compiler_params=pltpu.CompilerParams(
            dimension_semantics=("arbitrary",), vmem_limit_bytes=VMEM_LIMIT),
        name="mlp",
    )(h2, gain, w1, w2)


def _layer(h, norm1_gain, w_in, q_norm_gain, k_norm_gain, attn_sinks,
           mu_r, mu_k, mu_v, mu_w, mu_a, mu_g, decay_bias, decay_up, aaa_bias, aaa_up,
           gate_up, k_k, k_a, r_k, ln_x_gain, ln_x_bias, w_branch_attn, w_branch_rwkv,
           w_out, norm2_gain, w_ff_in, w_ff_out):
    b, t, d = h.shape
    x2 = h.reshape(b * t, d)
    row = lambda a: a.reshape(1, -1).astype(F32)
    n_proj = ATTN_Q_WIDTH + 2 * ATTN_KV_WIDTH + RWKV_IN_WIDTH

    q, kv, rw = _inproj(x2, row(norm1_gain), w_in[:, :n_proj].astype(BF16))
    attn = _attention(q.reshape(b, t, -1), kv.reshape(b, t, -1),
                      row(q_norm_gain), row(k_norm_gain), attn_sinks.astype(F32))
    mu = jnp.concatenate([mu_r, mu_k, mu_v, mu_w, mu_a, mu_g]).reshape(1, -1).astype(F32)
    rwkv = _rwkv(rw.reshape(b, t, -1), mu, row(decay_bias), decay_up.astype(BF16),
                 row(aaa_bias), aaa_up.astype(BF16), gate_up.astype(BF16),
                 row(k_k), row(k_a), row(r_k), row(ln_x_gain), row(ln_x_bias))
    h2 = _merge(x2, attn.reshape(b * t, -1), rwkv.reshape(b * t, -1), row(norm1_gain),
                w_in[:, n_proj:].astype(BF16), w_branch_attn.astype(BF16),
                w_branch_rwkv.astype(BF16), w_out.astype(BF16))
    out = _mlp(h2, row(norm2_gain), w_ff_in.astype(BF16), w_ff_out.astype(BF16))
    return out.reshape(b, t, d)


def kernel(x, norm1_gain, w_in, q_norm_gain, k_norm_gain, attn_sinks, mu_r, mu_k, mu_v, mu_w, mu_a, mu_g, decay_bias, decay_up, aaa_bias, aaa_up, gate_up, k_k, k_a, r_k, ln_x_gain, ln_x_bias, w_branch_attn, w_branch_rwkv, w_out, norm2_gain, w_ff_in, w_ff_out):
    h = x.astype(F32)
    params = (norm1_gain, w_in, q_norm_gain, k_norm_gain, attn_sinks, mu_r, mu_k, mu_v, mu_w,
              mu_a, mu_g, decay_bias, decay_up, aaa_bias, aaa_up, gate_up, k_k, k_a, r_k,
              ln_x_gain, ln_x_bias, w_branch_attn, w_branch_rwkv, w_out, norm2_gain,
              w_ff_in, w_ff_out)
    for l in range(norm1_gain.shape[0]):
        h = _layer(h, *(p[l] for p in params))
    return h.astype(x.dtype)
```

```python
import functools
import math

import jax
import jax.numpy as jnp
from jax import lax
from jax.experimental import pallas as pl
from jax.experimental.pallas import tpu as pltpu

F32 = jnp.float32
BF16 = jnp.bfloat16

HEAD_DIM = 64
ATTN_Q_HEADS = 8
ATTN_KV_HEADS = 2
ATTN_GROUP = ATTN_Q_HEADS // ATTN_KV_HEADS
WINDOW = 128
BLOCK = 128
RWKV_HEADS = 8
RWKV_HEAD_SIZE = 64
LORA_DECAY = 64
LORA_AAA = 64
LORA_GATE = 128
ATTN_Q_WIDTH = ATTN_Q_HEADS * HEAD_DIM
ATTN_KV_WIDTH = ATTN_KV_HEADS * HEAD_DIM
RWKV_WIDTH = RWKV_HEADS * RWKV_HEAD_SIZE
RWKV_IN_WIDTH = 3 * RWKV_WIDTH + LORA_DECAY + LORA_AAA + LORA_GATE
RMS_EPS = 1e-6
GN_EPS = 64e-5
L2_EPS = 1e-12

CHUNK = 64
ROW_TILE = 512
ATTN_TILE = 512
RWKV_TILE = 256
VMEM_LIMIT = 56 * 1024 * 1024

_NT = (((1,), (1,)), ((), ()))
_TN = (((0,), (0,)), ((), ()))


def _bdot(a, b):
    return jnp.dot(a.astype(BF16), b.astype(BF16), preferred_element_type=F32)


def _bdot_nt(a, b):
    return lax.dot_general(a.astype(BF16), b.astype(BF16), _NT, preferred_element_type=F32)


def _rms(x, gain):
    return x * lax.rsqrt(jnp.mean(x * x, axis=-1, keepdims=True) + RMS_EPS) * gain


def _sigmoid(x):
    return 1.0 / (1.0 + jnp.exp(-x))


def _inproj_kernel(x_ref, g_ref, w_ref, q_ref, kv_ref, rw_ref):
    u = _rms(x_ref[...], g_ref[...])
    p = _bdot(u, w_ref[...])
    q_ref[...] = p[:, :ATTN_Q_WIDTH]
    kv_ref[...] = p[:, ATTN_Q_WIDTH:ATTN_Q_WIDTH + 2 * ATTN_KV_WIDTH]
    rw_ref[...] = p[:, ATTN_Q_WIDTH + 2 * ATTN_KV_WIDTH:]


def _inproj(x2, gain, w):
    n, d = x2.shape
    wid = w.shape[1]
    tm = min(ROW_TILE, n)
    return pl.pallas_call(
        _inproj_kernel,
        grid=(n // tm,),
        in_specs=[
            pl.BlockSpec((tm, d), lambda i: (i, 0)),
            pl.BlockSpec((1, d), lambda i: (0, 0)),
            pl.BlockSpec((d, wid), lambda i: (0, 0)),
        ],
        out_specs=[
            pl.BlockSpec((tm, ATTN_Q_WIDTH), lambda i: (i, 0)),
            pl.BlockSpec((tm, 2 * ATTN_KV_WIDTH), lambda i: (i, 0)),
            pl.BlockSpec((tm, RWKV_IN_WIDTH), lambda i: (i, 0)),
        ],
        out_shape=[
            jax.ShapeDtypeStruct((n, ATTN_Q_WIDTH), F32),
            jax.ShapeDtypeStruct((n, 2 * ATTN_KV_WIDTH), F32),
            jax.ShapeDtypeStruct((n, RWKV_IN_WIDTH), F32),
        ],
        compiler_params=pltpu.CompilerParams(
            dimension_semantics=("arbitrary",), vmem_limit_bytes=VMEM_LIMIT),
        name="inproj",
    )(x2, gain, w)


def _head_sumsq(x, ones_bd):
    x2 = x * x
    hi = x2.astype(BF16)
    lo = (x2 - hi.astype(F32)).astype(BF16)
    return (jnp.dot(hi, ones_bd, preferred_element_type=F32)
            + jnp.dot(lo, ones_bd, preferred_element_type=F32))


def _attn_kernel(q_ref, kv_ref, kvp_ref, qg_ref, kg_ref, sink_ref, o_ref, *, nblk):
    i = pl.program_id(1)
    hp = 2 * HEAD_DIM
    ncol = ATTN_Q_WIDTH // hp
    lane = lax.broadcasted_iota(jnp.int32, (1, hp), 1)
    even = lane < HEAD_DIM
    ri = lax.broadcasted_iota(jnp.int32, (hp, hp), 0)
    ci = lax.broadcasted_iota(jnp.int32, (hp, hp), 1)
    ones_bd = ((ri < HEAD_DIM) == (ci < HEAD_DIM)).astype(BF16)

    def head_rms(x, gain):
        ms = _head_sumsq(x, ones_bd) * (1.0 / HEAD_DIM)
        return x * lax.rsqrt(ms + RMS_EPS) * gain

    q = q_ref[0]
    qn = [head_rms(q[:, m * hp:(m + 1) * hp], qg_ref[:, m * hp:(m + 1) * hp]).astype(BF16)
          for m in range(ncol)]
    kv_all = jnp.concatenate([kvp_ref[0], kv_ref[0]], axis=0)
    kn = head_rms(kv_all[:, :hp], kg_ref[...])
    vv = kv_all[:, hp:]
    kn_sw = pltpu.roll(kn, HEAD_DIM, 1)
    vv_sw = pltpu.roll(vv, HEAD_DIM, 1)
    kdup = [jnp.where(even, kn, kn_sw).astype(BF16), jnp.where(even, kn_sw, kn).astype(BF16)]
    vdup = [jnp.where(even, vv, vv_sw).astype(BF16), jnp.where(even, vv_sw, vv).astype(BF16)]

    rows = ATTN_GROUP * BLOCK
    qpos = (lax.broadcasted_iota(jnp.int32, (rows, 2 * BLOCK), 0) & (BLOCK - 1)) + BLOCK
    spos = lax.broadcasted_iota(jnp.int32, (rows, 2 * BLOCK), 1)
    rel = qpos - spos
    band = (rel >= 0) & (rel < WINDOW)
    band_first = band & ((spos >= BLOCK) | (i > 0))

    units = [(j, kh) for j in range(nblk) for kh in range(ATTN_KV_HEADS)]
    zero_q = jnp.zeros((BLOCK, hp), BF16)
    scores = []
    for j, kh in units:
        parts = []
        for g in range(ATTN_GROUP):
            h = kh * ATTN_GROUP + g
            col = qn[h // 2][j * BLOCK:(j + 1) * BLOCK]
            parts.append(jnp.where(even, col, zero_q) if h % 2 == 0 else jnp.where(even, zero_q, col))
        s = lax.dot_general(jnp.concatenate(parts, axis=0), kdup[kh][j * BLOCK:(j + 2) * BLOCK], _NT,
                            preferred_element_type=F32)
        scores.append(jnp.where(band_first if j == 0 else band, s, -jnp.inf))
    probs, dens = [], []
    for (j, kh), s in zip(units, scores):
        sink = jnp.concatenate(
            [jnp.full((BLOCK, 1), sink_ref[kh * ATTN_GROUP + g], F32) for g in range(ATTN_GROUP)],
            axis=0)
        m = jnp.maximum(jnp.max(s, axis=-1, keepdims=True), sink)
        p = jnp.exp(s - m)
        dens.append(jnp.sum(p, axis=-1, keepdims=True) + jnp.exp(sink - m))
        probs.append(p.astype(BF16))
    for (j, kh), p, den in zip(units, probs, dens):
        o4 = jnp.dot(p, vdup[kh][j * BLOCK:(j + 2) * BLOCK], preferred_element_type=F32) * (1.0 / den)
        for m in range(ATTN_GROUP // 2):
            o_e = o4[(2 * m) * BLOCK:(2 * m + 1) * BLOCK]
            o_o = o4[(2 * m + 1) * BLOCK:(2 * m + 2) * BLOCK]
            c0 = (kh * (ATTN_GROUP // 2) + m) * hp
            o_ref[0, j * BLOCK:(j + 1) * BLOCK, c0:c0 + hp] = jnp.where(even, o_e, o_o).astype(o_ref.dtype)


def _attention(q, kv, q_gain, k_gain, sinks):
    b, t, _ = q.shape
    tq = min(ATTN_TILE, t)
    nblk = tq // BLOCK
    q_gain = jnp.tile(q_gain * (HEAD_DIM ** -0.5), (1, ATTN_Q_HEADS))
    k_gain = jnp.tile(k_gain, (1, ATTN_KV_HEADS))
    return pl.pallas_call(
        functools.partial(_attn_kernel, nblk=nblk),
        grid=(b, t // tq),
        in_specs=[
            pl.BlockSpec((1, tq, ATTN_Q_WIDTH), lambda bi, i: (bi, i, 0)),
            pl.BlockSpec((1, tq, 2 * ATTN_KV_WIDTH), lambda bi, i: (bi, i, 0)),
            pl.BlockSpec((1, BLOCK, 2 * ATTN_KV_WIDTH),
                         lambda bi, i: (bi, jnp.maximum(i * nblk - 1, 0), 0)),
            pl.BlockSpec((1, ATTN_Q_WIDTH), lambda bi, i: (0, 0)),
            pl.BlockSpec((1, ATTN_KV_WIDTH), lambda bi, i: (0, 0)),
            pl.BlockSpec(memory_space=pltpu.SMEM),
        ],
        out_specs=pl.BlockSpec((1, tq, ATTN_Q_WIDTH), lambda bi, i: (bi, i, 0)),
        out_shape=jax.ShapeDtypeStruct((b, t, ATTN_Q_WIDTH), BF16),
        compiler_params=pltpu.CompilerParams(
            dimension_semantics=("arbitrary", "arbitrary"), vmem_limit_bytes=VMEM_LIMIT),
        name="attention",
    )(q, kv, kv, q_gain, k_gain, sinks)


def _pair_blockdiag(x, even):
    zero = jnp.zeros_like(x)
    return jnp.concatenate([jnp.where(even, x, zero), jnp.where(even, zero, x)], axis=0)


def _pair_sum(x, even):
    s_e = jnp.sum(jnp.where(even, x, 0.0), axis=-1, keepdims=True)
    s_o = jnp.sum(jnp.where(even, 0.0, x), axis=-1, keepdims=True)
    return jnp.where(even, s_e, s_o)


def _rwkv_kernel(xin_ref, mu_ref, dbias_ref, dup_ref, abias_ref, aup_ref, gup_ref,
                 kk_ref, ka_ref, rk_ref, lng_ref, lnb_ref, o_ref, prev_ref, state_ref):
    c = CHUNK
    hp = 2 * RWKV_HEAD_SIZE
    w = RWKV_WIDTH
    tb = xin_ref.shape[1]
    nc = tb // c
    npair = w // hp

    @pl.when(pl.program_id(1) == 0)
    def _():
        prev_ref[...] = jnp.zeros_like(prev_ref)
        state_ref[...] = jnp.zeros_like(state_ref)

    xin = xin_ref[0]
    row = lax.broadcasted_iota(jnp.int32, (tb, 1), 0)
    prev = jnp.where(row == 0, prev_ref[...], pltpu.roll(xin, 1, 0))
    prev_ref[...] = xin[tb - 1:tb, :]
    xs = xin + (prev - xin) * mu_ref[...]

    r = xs[:, 0:w]
    k = xs[:, w:2 * w]
    v = xs[:, 2 * w:3 * w]
    w_lora = xs[:, 3 * w:3 * w + LORA_DECAY]
    a_lora = xs[:, 3 * w + LORA_DECAY:3 * w + LORA_DECAY + LORA_AAA]
    g_lora = xs[:, 3 * w + LORA_DECAY + LORA_AAA:]

    d = dbias_ref[...] + _bdot(jnp.tanh(w_lora), dup_ref[...])
    lw = (-math.exp(-0.5)) * _sigmoid(d)
    a_sig = _sigmoid(abias_ref[...] + _bdot(a_lora, aup_ref[...]))
    gate = _bdot(_sigmoid(g_lora), gup_ref[...])

    kkr = k * kk_ref[...]
    k2 = k * (1.0 + (a_sig - 1.0) * ka_ref[...])
    rk = r * k2 * rk_ref[...]
    even_t = lax.broadcasted_iota(jnp.int32, (tb, hp), 1) < RWKV_HEAD_SIZE
    cols = [slice(p * hp, (p + 1) * hp) for p in range(npair)]
    den = jnp.concatenate(
        [jnp.maximum(jnp.sqrt(_pair_sum(kkr[:, cs] * kkr[:, cs], even_t)), L2_EPS) for cs in cols],
        axis=1)
    bonus = jnp.concatenate([_pair_sum(rk[:, cs], even_t) for cs in cols], axis=1) * v
    kk = kkr / den
    a_s = -kk
    b_s = kk * a_sig

    ti = lax.broadcasted_iota(jnp.int32, (c, c), 0)
    si = lax.broadcasted_iota(jnp.int32, (c, c), 1)
    lower_f = (ti >= si).astype(F32)
    cl = jnp.concatenate(
        [jnp.dot(lower_f, lw[ci * c:(ci + 1) * c], preferred_element_type=F32,
                 precision=lax.Precision.HIGHEST) for ci in range(nc)], axis=0)
    e_c = jnp.exp(cl)
    e_neg = jnp.exp(-cl)
    g_rows = [e_c[ci * c + c - 1:ci * c + c, :] for ci in range(nc)]
    g_full = jnp.concatenate([jnp.broadcast_to(g, (c, w)) for g in g_rows], axis=0)
    at = a_s * jnp.exp(cl - lw)
    rt = r * e_c
    bt = b_s * e_neg
    kt = k2 * e_neg
    bh = (bt * g_full).astype(BF16)
    khat = (kt * g_full).astype(BF16)
    at_b, rt_b, bt_b, kt_b, v_b = (z.astype(BF16) for z in (at, rt, bt, kt, v))

    even = lax.broadcasted_iota(jnp.int32, (c, hp), 1) < RWKV_HEAD_SIZE
    gi = lax.broadcasted_iota(jnp.int32, (2 * hp, 2 * hp), 0)
    gj = lax.broadcasted_iota(jnp.int32, (2 * hp, 2 * hp), 1)
    g_t, g_s = gi & (c - 1), gj & (c - 1)
    g_mask = (g_t > g_s) | ((gi >= hp) & (g_t == g_s))
    pi = lax.broadcasted_iota(jnp.int32, (hp, hp), 0)
    pj = lax.broadcasted_iota(jnp.int32, (hp, hp), 1)
    eye = pi == pj

    units = [(ci, p) for ci in range(nc) for p in range(npair)]

    def blk(x, ci, p):
        return _pair_blockdiag(x[ci * c:(ci + 1) * c, cols[p]], even)

    gm = []
    for ci, p in units:
        lhs = jnp.concatenate([blk(at_b, ci, p), blk(rt_b, ci, p)], axis=0)
        rhs = jnp.concatenate([blk(bt_b, ci, p), blk(kt_b, ci, p)], axis=0)
        gm.append(jnp.where(g_mask, lax.dot_general(lhs, rhs, _NT, preferred_element_type=F32), 0.0))
    a_pow = [g[:hp, :hp].astype(BF16) for g in gm]
    t_inv = [jnp.where(eye, 1.0, g[:hp, :hp]) for g in gm]
    a_pow = [jnp.dot(a, a, preferred_element_type=F32).astype(BF16) for a in a_pow]
    for _ in range(int(math.log2(c)) - 2):
        res = [jnp.dot(a, jnp.concatenate([t.astype(BF16), a], axis=1), preferred_element_type=F32)
               for a, t in zip(a_pow, t_inv)]
        t_inv = [t + z[:, :hp] for t, z in zip(t_inv, res)]
        a_pow = [z[:, hp:].astype(BF16) for z in res]
    t_inv = [t + jnp.dot(a, t.astype(BF16), preferred_element_type=F32)
             for a, t in zip(a_pow, t_inv)]
    wx = [jnp.dot(t.astype(BF16),
                  jnp.concatenate([blk(at_b, ci, p), g[:hp, hp:].astype(BF16)], axis=1),
                  preferred_element_type=F32)
          for (ci, p), t, g in zip(units, t_inv, gm)]
    v_bd = [blk(v_b, ci, p) for ci, p in units]
    u0 = [jnp.dot(z[:, hp:].astype(BF16), vb, preferred_element_type=F32) for z, vb in zip(wx, v_bd)]
    rhs_ab = [jnp.concatenate(
        [jnp.concatenate([z[:, :hp].astype(BF16), u.astype(BF16)], axis=1),
         jnp.concatenate([jnp.zeros_like(vb), vb], axis=1)], axis=0)
        for z, u, vb in zip(wx, u0, v_bd)]
    ab_y = [jnp.dot(g[hp:, :].astype(BF16), rz, preferred_element_type=F32)
            for g, rz in zip(gm, rhs_ab)]
    ab_h = [lax.dot_general(jnp.concatenate([blk(bh, ci, p), blk(khat, ci, p)], axis=0), rz, _TN,
                            preferred_element_type=F32)
            for (ci, p), rz in zip(units, rhs_ab)]

    state = [state_ref[:, cs] for cs in cols]
    y_cols = [[None] * npair for _ in range(nc)]
    for idx, (ci, p) in enumerate(units):
        qmat = blk(rt, ci, p) + ab_y[idx][:, :hp]
        g_diag = jnp.where(eye, jnp.broadcast_to(g_rows[ci][:, cols[p]], (hp, hp)), 0.0)
        mmat = g_diag + ab_h[idx][:, :hp]
        ys = jnp.dot(jnp.concatenate([qmat, mmat], axis=0).astype(BF16), state[p].astype(BF16),
                     preferred_element_type=F32)
        y_bd = ys[:hp] + ab_y[idx][:, hp:]
        state[p] = ys[hp:] + ab_h[idx][:, hp:]
        y = y_bd[:c] + y_bd[c:]
        mean = _pair_sum(y, even) * (1.0 / RWKV_HEAD_SIZE)
        yc = y - mean
        var = _pair_sum(yc * yc, even) * (1.0 / RWKV_HEAD_SIZE)
        y_cols[ci][p] = yc * lax.rsqrt(var + GN_EPS)
    for p in range(npair):
        state_ref[:, cols[p]] = state[p]

    yn = jnp.concatenate([jnp.concatenate(yc_, axis=1) for yc_ in y_cols], axis=0)
    out = (yn * lng_ref[...] + lnb_ref[...] + bonus) * gate
    o_ref[0] = out.astype(o_ref.dtype)


def _rwkv(rw, mu, decay_bias, decay_up, aaa_bias, aaa_up, gate_up, k_k, k_a, r_k, ln_g, ln_b):
    b, t, _ = rw.shape
    tb = min(RWKV_TILE, t)
    w = RWKV_WIDTH

    def full(shape):
        return pl.BlockSpec(shape, lambda bi, ti: (0, 0))

    return pl.pallas_call(
        _rwkv_kernel,
        grid=(b, t // tb),
        in_specs=[
            pl.BlockSpec((1, tb, RWKV_IN_WIDTH), lambda bi, ti: (bi, ti, 0)),
            full((1, RWKV_IN_WIDTH)),
            full((1, w)), full((LORA_DECAY, w)),
            full((1, w)), full((LORA_AAA, w)),
            full((LORA_GATE, w)),
            full((1, w)), full((1, w)), full((1, w)), full((1, w)), full((1, w)),
        ],
        out_specs=pl.BlockSpec((1, tb, w), lambda bi, ti: (bi, ti, 0)),
        out_shape=jax.ShapeDtypeStruct((b, t, w), BF16),
        scratch_shapes=[
            pltpu.VMEM((1, RWKV_IN_WIDTH), F32),
            pltpu.VMEM((2 * RWKV_HEAD_SIZE, w), F32),
        ],
        compiler_params=pltpu.CompilerParams(
            dimension_semantics=("arbitrary", "arbitrary"), vmem_limit_bytes=VMEM_LIMIT),
        name="rwkv",
    )(rw, mu, decay_bias, decay_up, aaa_bias, aaa_up, gate_up, k_k, k_a, r_k, ln_g, ln_b)


def _merge_kernel(x_ref, attn_ref, rwkv_ref, g_ref, wg_ref, wba_ref, wbr_ref, wo_ref, h_ref):
    x = x_ref[...]
    d = x.shape[1]
    u = _rms(x, g_ref[...])
    gates = _bdot(u, wg_ref[...])
    y_attn = jnp.dot(attn_ref[...], wba_ref[...], preferred_element_type=F32)
    y_rwkv = jnp.dot(rwkv_ref[...], wbr_ref[...], preferred_element_type=F32)
    mixed = _sigmoid(gates[:, :d]) * y_attn + _sigmoid(gates[:, d:]) * y_rwkv
    h_ref[...] = x + _bdot(mixed, wo_ref[...])


def _merge(x2, attn, rwkv, gain, w_gate, w_ba, w_br, w_out):
    n, d = x2.shape
    tm = min(ROW_TILE, n)

    def full(a):
        return pl.BlockSpec(a.shape, lambda i: (0, 0))

    return pl.pallas_call(
        _merge_kernel,
        grid=(n // tm,),
        in_specs=[
            pl.BlockSpec((tm, d), lambda i: (i, 0)),
            pl.BlockSpec((tm, attn.shape[1]), lambda i: (i, 0)),
            pl.BlockSpec((tm, rwkv.shape[1]), lambda i: (i, 0)),
            full(gain), full(w_gate), full(w_ba), full(w_br), full(w_out),
        ],
        out_specs=pl.BlockSpec((tm, d), lambda i: (i, 0)),
        out_shape=jax.ShapeDtypeStruct((n, d), F32),
        compiler_params=pltpu.CompilerParams(
            dimension_semantics=("arbitrary",), vmem_limit_bytes=VMEM_LIMIT),
        name="merge",
    )(x2, attn, rwkv, gain, w_gate, w_ba, w_br, w_out)


def _mlp_kernel(h_ref, g_ref, w1_ref, w2_ref, o_ref):
    h = h_ref[...]
    hidden = jnp.square(jnp.maximum(_bdot(_rms(h, g_ref[...]), w1_ref[...]), 0.0))
    o_ref[...] = h + _bdot(hidden, w2_ref[...])


def _mlp(h2, gain, w1, w2):
    n, d = h2.shape
    tm = min(ROW_TILE // 2, n)

    def full(a):
        return pl.BlockSpec(a.shape, lambda i: (0, 0))

    return pl.pallas_call(
        _mlp_kernel,
        grid=(n // tm,),
        in_specs=[pl.BlockSpec((tm, d), lambda i: (i, 0)), full(gain), full(w1), full(w2)],
        out_specs=pl.BlockSpec((tm, d), lambda i: (i, 0)),
        out_shape=jax.ShapeDtypeStruct((n, d), F32),
        compiler_params=pltpu.CompilerParams(
            dimension_semantics=("arbitrary",), vmem_limit_bytes=VMEM_LIMIT),
        name="mlp",
    )(h2, gain, w1, w2)


def _layer(h, norm1_gain, w_in, q_norm_gain, k_norm_gain, attn_sinks,
           mu_r, mu_k, mu_v, mu_w, mu_a, mu_g, decay_bias, decay_up, aaa_bias, aaa_up,
           gate_up, k_k, k_a, r_k, ln_x_gain, ln_x_bias, w_branch_attn, w_branch_rwkv,
           w_out, norm2_gain, w_ff_in, w_ff_out):
    b, t, d = h.shape
    x2 = h.reshape(b * t, d)
    row = lambda a: a.reshape(1, -1).astype(F32)
    n_proj = ATTN_Q_WIDTH + 2 * ATTN_KV_WIDTH + RWKV_IN_WIDTH

    q, kv, rw = _inproj(x2, row(norm1_gain), w_in[:, :n_proj].astype(BF16))
    attn = _attention(q.reshape(b, t, -1), kv.reshape(b, t, -1),
                      row(q_norm_gain), row(k_norm_gain), attn_sinks.astype(F32))
    mu = jnp.concatenate([mu_r, mu_k, mu_v, mu_w, mu_a, mu_g]).reshape(1, -1).astype(F32)
    rwkv = _rwkv(rw.reshape(b, t, -1), mu, row(decay_bias), decay_up.astype(BF16),
                 row(aaa_bias), aaa_up.astype(BF16), gate_up.astype(BF16),
                 row(k_k), row(k_a), row(r_k), row(ln_x_gain), row(ln_x_bias))
    h2 = _merge(x2, attn.reshape(b * t, -1), rwkv.reshape(b * t, -1), row(norm1_gain),
                w_in[:, n_proj:].astype(BF16), w_branch_attn.astype(BF16),
                w_branch_rwkv.astype(BF16), w_out.astype(BF16))
    out = _mlp(h2, row(norm2_gain), w_ff_in.astype(BF16), w_ff_out.astype(BF16))
    return out.reshape(b, t, d)


def kernel(x, norm1_gain, w_in, q_norm_gain, k_norm_gain, attn_sinks, mu_r, mu_k, mu_v, mu_w, mu_a, mu_g, decay_bias, decay_up, aaa_bias, aaa_up, gate_up, k_k, k_a, r_k, ln_x_gain, ln_x_bias, w_branch_attn, w_branch_rwkv, w_out, norm2_gain, w_ff_in, w_ff_out):
    h = x.astype(F32)
    params = (norm1_gain, w_in, q_norm_gain, k_norm_gain, attn_sinks, mu_r, mu_k, mu_v, mu_w,
              mu_a, mu_g, decay_bias, decay_up, aaa_bias, aaa_up, gate_up, k_k, k_a, r_k,
              ln_x_gain, ln_x_bias, w_branch_attn, w_branch_rwkv, w_out, norm2_gain,
              w_ff_in, w_ff_out)
    for l in range(norm1_gain.shape[0]):
        h = _layer(h, *(p[l] for p in params))
    return h.astype(x.dtype)
```

```python
import math

import jax
import jax.numpy as jnp
from jax import lax
from jax.experimental import pallas as pl
from jax.experimental.pallas import tpu as pltpu

F32 = jnp.float32
BF16 = jnp.bfloat16

HEAD_DIM = 64
ATTN_Q_HEADS = 8
ATTN_KV_HEADS = 2
ATTN_GROUP = ATTN_Q_HEADS // ATTN_KV_HEADS
WINDOW = 128
BLOCK = 128
RWKV_HEADS = 8
RWKV_HEAD_SIZE = 64
LORA_DECAY = 64
LORA_AAA = 64
LORA_GATE = 128
ATTN_Q_WIDTH = ATTN_Q_HEADS * HEAD_DIM
ATTN_KV_WIDTH = ATTN_KV_HEADS * HEAD_DIM
RWKV_WIDTH = RWKV_HEADS * RWKV_HEAD_SIZE
RWKV_IN_WIDTH = 3 * RWKV_WIDTH + LORA_DECAY + LORA_AAA + LORA_GATE
RMS_EPS = 1e-6
GN_EPS = 64e-5
L2_EPS = 1e-12

CHUNK = 64
ROW_TILE = 512
ATTN_TILE = 512
RWKV_TILE = 256
VMEM_LIMIT = 56 * 1024 * 1024

_NT = (((1,), (1,)), ((), ()))
_TN = (((0,), (0,)), ((), ()))


def _bdot(a, b):
    return jnp.dot(a.astype(BF16), b.astype(BF16), preferred_element_type=F32)


def _rms(x, gain):
    return x * lax.rsqrt(jnp.mean(x * x, axis=-1, keepdims=True) + RMS_EPS) * gain


def _sigmoid(x):
    return 1.0 / (1.0 + jnp.exp(-x))


def _inproj_kernel(x_ref, g_ref, w_ref, q_ref, kv_ref, rw_ref):
    u = _rms(x_ref[...], g_ref[...])
    p = _bdot(u, w_ref[...])
    q_ref[...] = p[:, :ATTN_Q_WIDTH]
    kv_ref[...] = p[:, ATTN_Q_WIDTH:ATTN_Q_WIDTH + 2 * ATTN_KV_WIDTH]
    rw_ref[...] = p[:, ATTN_Q_WIDTH + 2 * ATTN_KV_WIDTH:]


def _inproj(x2, gain, w):
    n, d = x2.shape
    wid = w.shape[1]
    tm = min(ROW_TILE, n)
    return pl.pallas_call(
        _inproj_kernel,
        grid=(n // tm,),
        in_specs=[
            pl.BlockSpec((tm, d), lambda i: (i, 0)),
            pl.BlockSpec((1, d), lambda i: (0, 0)),
            pl.BlockSpec((d, wid), lambda i: (0, 0)),
        ],
        out_specs=[
            pl.BlockSpec((tm, ATTN_Q_WIDTH), lambda i: (i, 0)),
            pl.BlockSpec((tm, 2 * ATTN_KV_WIDTH), lambda i: (i, 0)),
            pl.BlockSpec((tm, RWKV_IN_WIDTH), lambda i: (i, 0)),
        ],
        out_shape=[
            jax.ShapeDtypeStruct((n, ATTN_Q_WIDTH), F32),
            jax.ShapeDtypeStruct((n, 2 * ATTN_KV_WIDTH), F32),
            jax.ShapeDtypeStruct((n, RWKV_IN_WIDTH), F32),
        ],
        compiler_params=pltpu.CompilerParams(
            dimension_semantics=("arbitrary",), vmem_limit_bytes=VMEM_LIMIT),
        name="inproj",
    )(x2, gain, w)


def _head_sumsq(x, ones_bd):
    x2 = x * x
    hi = x2.astype(BF16)
    lo = (x2 - hi.astype(F32)).astype(BF16)
    return (jnp.dot(hi, ones_bd, preferred_element_type=F32)
            + jnp.dot(lo, ones_bd, preferred_element_type=F32))


def _attn_scores(q, kv_all, q_gain, k_gain, first_tile):
    hp = 2 * HEAD_DIM
    nblk = q.shape[0] // BLOCK
    even = lax.broadcasted_iota(jnp.int32, (1, hp), 1) < HEAD_DIM
    ri = lax.broadcasted_iota(jnp.int32, (hp, hp), 0)
    ci = lax.broadcasted_iota(jnp.int32, (hp, hp), 1)
    ones_bd = ((ri < HEAD_DIM) == (ci < HEAD_DIM)).astype(BF16)

    def head_rms(x, gain):
        ms = _head_sumsq(x, ones_bd) * (1.0 / HEAD_DIM)
        return x * lax.rsqrt(ms + RMS_EPS) * gain

    qn = [head_rms(q[:, m * hp:(m + 1) * hp], q_gain[:, m * hp:(m + 1) * hp]).astype(BF16)
          for m in range(ATTN_Q_WIDTH // hp)]
    kn = head_rms(kv_all[:, :hp], k_gain)
    vv = kv_all[:, hp:]
    kn_sw = pltpu.roll(kn, HEAD_DIM, 1)
    vv_sw = pltpu.roll(vv, HEAD_DIM, 1)
    kdup = [jnp.where(even, kn, kn_sw).astype(BF16), jnp.where(even, kn_sw, kn).astype(BF16)]
    vdup = [jnp.where(even, vv, vv_sw).astype(BF16), jnp.where(even, vv_sw, vv).astype(BF16)]

    rows = ATTN_GROUP * BLOCK
    qpos = (lax.broadcasted_iota(jnp.int32, (rows, 2 * BLOCK), 0) & (BLOCK - 1)) + BLOCK
    spos = lax.broadcasted_iota(jnp.int32, (rows, 2 * BLOCK), 1)
    rel = qpos - spos
    band = (rel >= 0) & (rel < WINDOW)
    band_first = band & ((spos >= BLOCK) | jnp.logical_not(first_tile))

    units = [(j, kh) for j in range(nblk) for kh in range(ATTN_KV_HEADS)]
    zero_q = jnp.zeros((BLOCK, hp), BF16)
    scores = []
    for j, kh in units:
        parts = []
        for g in range(ATTN_GROUP):
            h = kh * ATTN_GROUP + g
            col = qn[h // 2][j * BLOCK:(j + 1) * BLOCK]
            parts.append(jnp.where(even, col, zero_q) if h % 2 == 0 else jnp.where(even, zero_q, col))
        s = lax.dot_general(jnp.concatenate(parts, axis=0), kdup[kh][j * BLOCK:(j + 2) * BLOCK], _NT,
                            preferred_element_type=F32)
        scores.append(jnp.where(band_first if j == 0 else band, s, -jnp.inf))
    return units, scores, vdup


def _attn_softmax(units, scores, sink_ref):
    probs, dens = [], []
    for (j, kh), s in zip(units, scores):
        sink = jnp.concatenate(
            [jnp.full((BLOCK, 1), sink_ref[kh * ATTN_GROUP + g], F32) for g in range(ATTN_GROUP)],
            axis=0)
        m = jnp.maximum(jnp.max(s, axis=-1, keepdims=True), sink)
        p = jnp.exp(s - m)
        dens.append(jnp.sum(p, axis=-1, keepdims=True) + jnp.exp(sink - m))
        probs.append(p.astype(BF16))
    return probs, dens


def _attn_values(units, probs, dens, vdup, o_ref):
    hp = 2 * HEAD_DIM
    even = lax.broadcasted_iota(jnp.int32, (1, hp), 1) < HEAD_DIM
    for (j, kh), p, den in zip(units, probs, dens):
        o4 = jnp.dot(p, vdup[kh][j * BLOCK:(j + 2) * BLOCK], preferred_element_type=F32) * (1.0 / den)
        for m in range(ATTN_GROUP // 2):
            o_e = o4[(2 * m) * BLOCK:(2 * m + 1) * BLOCK]
            o_o = o4[(2 * m + 1) * BLOCK:(2 * m + 2) * BLOCK]
            c0 = (kh * (ATTN_GROUP // 2) + m) * hp
            o_ref[j * BLOCK:(j + 1) * BLOCK, c0:c0 + hp] = jnp.where(even, o_e, o_o).astype(o_ref.dtype)


def _pair_blockdiag(x, even):
    zero = jnp.zeros_like(x)
    return jnp.concatenate([jnp.where(even, x, zero), jnp.where(even, zero, x)], axis=0)


def _pair_sum(x, even):
    s_e = jnp.sum(jnp.where(even, x, 0.0), axis=-1, keepdims=True)
    s_o = jnp.sum(jnp.where(even, 0.0, x), axis=-1, keepdims=True)
    return jnp.where(even, s_e, s_o)


def _rwkv_kernel(xin_ref, mu_ref, dbias_ref, dup_ref, abias_ref, aup_ref, gup_ref,
                 kk_ref, ka_ref, rk_ref, lng_ref, lnb_ref, o_ref, prev_ref, state_ref):
    c = CHUNK
    hp = 2 * RWKV_HEAD_SIZE
    w = RWKV_WIDTH
    tb = xin_ref.shape[1]
    nc = tb // c
    npair = w // hp

    @pl.when(pl.program_id(1) == 0)
    def _():
        prev_ref[...] = jnp.zeros_like(prev_ref)
        state_ref[...] = jnp.zeros_like(state_ref)

    xin = xin_ref[0]
    row = lax.broadcasted_iota(jnp.int32, (tb, 1), 0)
    prev = jnp.where(row == 0, prev_ref[...], pltpu.roll(xin, 1, 0))
    prev_ref[...] = xin[tb - 1:tb, :]
    xs = xin + (prev - xin) * mu_ref[...]

    r = xs[:, 0:w]
    k = xs[:, w:2 * w]
    v = xs[:, 2 * w:3 * w]
    w_lora = xs[:, 3 * w:3 * w + LORA_DECAY]
    a_lora = xs[:, 3 * w + LORA_DECAY:3 * w + LORA_DECAY + LORA_AAA]
    g_lora = xs[:, 3 * w + LORA_DECAY + LORA_AAA:]

    d = dbias_ref[...] + _bdot(jnp.tanh(w_lora), dup_ref[...])
    lw = (-math.exp(-0.5)) * _sigmoid(d)
    a_sig = _sigmoid(abias_ref[...] + _bdot(a_lora, aup_ref[...]))
    gate = _bdot(_sigmoid(g_lora), gup_ref[...])

    kkr = k * kk_ref[...]
    k2 = k * (1.0 + (a_sig - 1.0) * ka_ref[...])
    rk = r * k2 * rk_ref[...]
    even_t = lax.broadcasted_iota(jnp.int32, (tb, hp), 1) < RWKV_HEAD_SIZE
    cols = [slice(p * hp, (p + 1) * hp) for p in range(npair)]
    den = jnp.concatenate(
        [jnp.maximum(jnp.sqrt(_pair_sum(kkr[:, cs] * kkr[:, cs], even_t)), L2_EPS) for cs in cols],
        axis=1)
    bonus = jnp.concatenate([_pair_sum(rk[:, cs], even_t) for cs in cols], axis=1) * v
    kk = kkr / den
    a_s = -kk
    b_s = kk * a_sig

    ti = lax.broadcasted_iota(jnp.int32, (c, c), 0)
    si = lax.broadcasted_iota(jnp.int32, (c, c), 1)
    lower_f = (ti >= si).astype(F32)
    cl = jnp.concatenate(
        [jnp.dot(lower_f, lw[ci * c:(ci + 1) * c], preferred_element_type=F32,
                 precision=lax.Precision.HIGHEST) for ci in range(nc)], axis=0)
    e_c = jnp.exp(cl)
    e_neg = jnp.exp(-cl)
    g_rows = [e_c[ci * c + c - 1:ci * c + c, :] for ci in range(nc)]
    g_full = jnp.concatenate([jnp.broadcast_to(g, (c, w)) for g in g_rows], axis=0)
    at = a_s * jnp.exp(cl - lw)
    rt = r * e_c
    bt = b_s * e_neg
    kt = k2 * e_neg
    bh = (bt * g_full).astype(BF16)
    khat = (kt * g_full).astype(BF16)
    at_b, rt_b, bt_b, kt_b, v_b = (z.astype(BF16) for z in (at, rt, bt, kt, v))

    even = lax.broadcasted_iota(jnp.int32, (c, hp), 1) < RWKV_HEAD_SIZE
    gi = lax.broadcasted_iota(jnp.int32, (2 * hp, 2 * hp), 0)
    gj = lax.broadcasted_iota(jnp.int32, (2 * hp, 2 * hp), 1)
    g_t, g_s = gi & (c - 1), gj & (c - 1)
    g_mask = (g_t > g_s) | ((gi >= hp) & (g_t == g_s))
    pi = lax.broadcasted_iota(jnp.int32, (hp, hp), 0)
    pj = lax.broadcasted_iota(jnp.int32, (hp, hp), 1)
    eye = pi == pj

    units = [(ci, p) for ci in range(nc) for p in range(npair)]

    def blk(x, ci, p):
        return _pair_blockdiag(x[ci * c:(ci + 1) * c, cols[p]], even)

    gm = []
    for ci, p in units:
        lhs = jnp.concatenate([blk(at_b, ci, p), blk(rt_b, ci, p)], axis=0)
        rhs = jnp.concatenate([blk(bt_b, ci, p), blk(kt_b, ci, p)], axis=0)
        gm.append(jnp.where(g_mask, lax.dot_general(lhs, rhs, _NT, preferred_element_type=F32), 0.0))
    a_pow = [g[:hp, :hp].astype(BF16) for g in gm]
    t_inv = [jnp.where(eye, 1.0, g[:hp, :hp]) for g in gm]
    a_pow = [jnp.dot(a, a, preferred_element_type=F32).astype(BF16) for a in a_pow]
    for _ in range(int(math.log2(c)) - 2):
        res = [jnp.dot(a, jnp.concatenate([t.astype(BF16), a], axis=1), preferred_element_type=F32)
               for a, t in zip(a_pow, t_inv)]
        t_inv = [t + z[:, :hp] for t, z in zip(t_inv, res)]
        a_pow = [z[:, hp:].astype(BF16) for z in res]
    t_inv = [t + jnp.dot(a, t.astype(BF16), preferred_element_type=F32)
             for a, t in zip(a_pow, t_inv)]
    wx = [jnp.dot(t.astype(BF16),
                  jnp.concatenate([blk(at_b, ci, p), g[:hp, hp:].astype(BF16)], axis=1),
                  preferred_element_type=F32)
          for (ci, p), t, g in zip(units, t_inv, gm)]
    v_bd = [blk(v_b, ci, p) for ci, p in units]
    u0 = [jnp.dot(z[:, hp:].astype(BF16), vb, preferred_element_type=F32) for z, vb in zip(wx, v_bd)]
    rhs_ab = [jnp.concatenate(
        [jnp.concatenate([z[:, :hp].astype(BF16), u.astype(BF16)], axis=1),
         jnp.concatenate([jnp.zeros_like(vb), vb], axis=1)], axis=0)
        for z, u, vb in zip(wx, u0, v_bd)]
    ab_y = [jnp.dot(g[hp:, :].astype(BF16), rz, preferred_element_type=F32)
            for g, rz in zip(gm, rhs_ab)]
    ab_h = [lax.dot_general(jnp.concatenate([blk(bh, ci, p), blk(khat, ci, p)], axis=0), rz, _TN,
                            preferred_element_type=F32)
            for (ci, p), rz in zip(units, rhs_ab)]

    state = [state_ref[:, cs] for cs in cols]
    y_cols = [[None] * npair for _ in range(nc)]
    for idx, (ci, p) in enumerate(units):
        qmat = blk(rt, ci, p) + ab_y[idx][:, :hp]
        g_diag = jnp.where(eye, jnp.broadcast_to(g_rows[ci][:, cols[p]], (hp, hp)), 0.0)
        mmat = g_diag + ab_h[idx][:, :hp]
        ys = jnp.dot(jnp.concatenate([qmat, mmat], axis=0).astype(BF16), state[p].astype(BF16),
                     preferred_element_type=F32)
        y_bd = ys[:hp] + ab_y[idx][:, hp:]
        state[p] = ys[hp:] + ab_h[idx][:, hp:]
        y = y_bd[:c] + y_bd[c:]
        mean = _pair_sum(y, even) * (1.0 / RWKV_HEAD_SIZE)
        yc = y - mean
        var = _pair_sum(yc * yc, even) * (1.0 / RWKV_HEAD_SIZE)
        y_cols[ci][p] = yc * lax.rsqrt(var + GN_EPS)
    for p in range(npair):
        state_ref[:, cols[p]] = state[p]

    yn = jnp.concatenate([jnp.concatenate(yc_, axis=1) for yc_ in y_cols], axis=0)
    out = (yn * lng_ref[...] + lnb_ref[...] + bonus) * gate
    o_ref[0] = out.astype(o_ref.dtype)


def _rwkv(rw, mu, decay_bias, decay_up, aaa_bias, aaa_up, gate_up, k_k, k_a, r_k, ln_g, ln_b):
    b, t, _ = rw.shape
    tb = min(RWKV_TILE, t)
    w = RWKV_WIDTH

    def full(shape):
        return pl.BlockSpec(shape, lambda bi, ti: (0, 0))

    return pl.pallas_call(
        _rwkv_kernel,
        grid=(b, t // tb),
        in_specs=[
            pl.BlockSpec((1, tb, RWKV_IN_WIDTH), lambda bi, ti: (bi, ti, 0)),
            full((1, RWKV_IN_WIDTH)),
            full((1, w)), full((LORA_DECAY, w)),
            full((1, w)), full((LORA_AAA, w)),
            full((LORA_GATE, w)),
            full((1, w)), full((1, w)), full((1, w)), full((1, w)), full((1, w)),
        ],
        out_specs=pl.BlockSpec((1, tb, w), lambda bi, ti: (bi, ti, 0)),
        out_shape=jax.ShapeDtypeStruct((b, t, w), BF16),
        scratch_shapes=[
            pltpu.VMEM((1, RWKV_IN_WIDTH), F32),
            pltpu.VMEM((2 * RWKV_HEAD_SIZE, w), F32),
        ],
        compiler_params=pltpu.CompilerParams(
            dimension_semantics=("arbitrary", "arbitrary"), vmem_limit_bytes=VMEM_LIMIT),
        name="rwkv",
    )(rw, mu, decay_bias, decay_up, aaa_bias, aaa_up, gate_up, k_k, k_a, r_k, ln_g, ln_b)


def _attn_merge_kernel(x_ref, q_ref, kv_ref, kvp_ref, rwkv_ref, g_ref, qg_ref, kg_ref, sink_ref,
                       wg_ref, wba_ref, wbr_ref, wo_ref, h_ref, attn_ref):
    x = x_ref[...]
    d = x.shape[1]
    u = _rms(x, g_ref[...])
    kv_all = jnp.concatenate([kvp_ref[0], kv_ref[0]], axis=0)
    units, scores, vdup = _attn_scores(q_ref[0], kv_all, qg_ref[...], kg_ref[...],
                                       pl.program_id(1) == 0)
    gates = _bdot(u, wg_ref[...])
    y_rwkv = jnp.dot(rwkv_ref[...], wbr_ref[...], preferred_element_type=F32)
    probs, dens = _attn_softmax(units, scores, sink_ref)
    _attn_values(units, probs, dens, vdup, attn_ref)
    y_attn = jnp.dot(attn_ref[...], wba_ref[...], preferred_element_type=F32)
    mixed = _sigmoid(gates[:, :d]) * y_attn + _sigmoid(gates[:, d:]) * y_rwkv
    h_ref[...] = x + _bdot(mixed, wo_ref[...])


def _attn_merge(x2, q, kv, rwkv, gain, q_gain, k_gain, sinks, w_gate, w_ba, w_br, w_out):
    n, d = x2.shape
    b, t, _ = q.shape
    tq = min(ATTN_TILE, t)
    nblk = tq // BLOCK
    nt = t // tq
    q_gain = jnp.tile(q_gain * (HEAD_DIM ** -0.5), (1, ATTN_Q_HEADS))
    k_gain = jnp.tile(k_gain, (1, ATTN_KV_HEADS))

    def full(a):
        return pl.BlockSpec(a.shape, lambda bi, i: (0, 0))

    def rows(width):
        return pl.BlockSpec((tq, width), lambda bi, i: (bi * nt + i, 0))

    return pl.pallas_call(
        _attn_merge_kernel,
        grid=(b, nt),
        in_specs=[
            rows(d),
            pl.BlockSpec((1, tq, ATTN_Q_WIDTH), lambda bi, i: (bi, i, 0)),
            pl.BlockSpec((1, tq, 2 * ATTN_KV_WIDTH), lambda bi, i: (bi, i, 0)),
            pl.BlockSpec((1, BLOCK, 2 * ATTN_KV_WIDTH),
                         lambda bi, i: (bi, jnp.maximum(i * nblk - 1, 0), 0)),
            rows(rwkv.shape[1]),
            full(gain), full(q_gain), full(k_gain),
            pl.BlockSpec(memory_space=pltpu.SMEM),
            full(w_gate), full(w_ba), full(w_br), full(w_out),
        ],
        out_specs=rows(d),
        out_shape=jax.ShapeDtypeStruct((n, d), F32),
        scratch_shapes=[pltpu.VMEM((tq, ATTN_Q_WIDTH), BF16)],
        compiler_params=pltpu.CompilerParams(
            dimension_semantics=("arbitrary", "arbitrary"), vmem_limit_bytes=VMEM_LIMIT),
        name="attn_merge",
    )(x2, q, kv, kv, rwkv, gain, q_gain, k_gain, sinks, w_gate, w_ba, w_br, w_out)


def _mlp_kernel(h_ref, g_ref, w1_ref, w2_ref, o_ref):
    h = h_ref[...]
    hidden = jnp.square(jnp.maximum(_bdot(_rms(h, g_ref[...]), w1_ref[...]), 0.0))
    o_ref[...] = h + _bdot(hidden, w2_ref[...])


def _mlp(h2, gain, w1, w2):
    n, d = h2.shape
    tm = min(ROW_TILE // 2, n)

    def full(a):
        return pl.BlockSpec(a.shape, lambda i: (0, 0))

    return pl.pallas_call(
        _mlp_kernel,
        grid=(n // tm,),
        in_specs=[pl.BlockSpec((tm, d), lambda i: (i, 0)), full(gain), full(w1), full(w2)],
        out_specs=pl.BlockSpec((tm, d), lambda i: (i, 0)),
        out_shape=jax.ShapeDtypeStruct((n, d), F32),
        compiler_params=pltpu.CompilerParams(
            dimension_semantics=("arbitrary",), vmem_limit_bytes=VMEM_LIMIT),
        name="mlp",
    )(h2, gain, w1, w2)


def _layer(h, norm1_gain, w_in, q_norm_gain, k_norm_gain, attn_sinks,
           mu_r, mu_k, mu_v, mu_w, mu_a, mu_g, decay_bias, decay_up, aaa_bias, aaa_up,
           gate_up, k_k, k_a, r_k, ln_x_gain, ln_x_bias, w_branch_attn, w_branch_rwkv,
           w_out, norm2_gain, w_ff_in, w_ff_out):
    b, t, d = h.shape
    x2 = h.reshape(b * t, d)
    row = lambda a: a.reshape(1, -1).astype(F32)
    n_proj = ATTN_Q_WIDTH + 2 * ATTN_KV_WIDTH + RWKV_IN_WIDTH

    q, kv, rw = _inproj(x2, row(norm1_gain), w_in[:, :n_proj].astype(BF16))
    mu = jnp.concatenate([mu_r, mu_k, mu_v, mu_w, mu_a, mu_g]).reshape(1, -1).astype(F32)
    rwkv = _rwkv(rw.reshape(b, t, -1), mu, row(decay_bias), decay_up.astype(BF16),
                 row(aaa_bias), aaa_up.astype(BF16), gate_up.astype(BF16),
                 row(k_k), row(k_a), row(r_k), row(ln_x_gain), row(ln_x_bias))
    h2 = _attn_merge(x2, q.reshape(b, t, -1), kv.reshape(b, t, -1), rwkv.reshape(b * t, -1),
                     row(norm1_gain), row(q_norm_gain), row(k_norm_gain), attn_sinks.astype(F32),
                     w_in[:, n_proj:].astype(BF16), w_branch_attn.astype(BF16),
                     w_branch_rwkv.astype(BF16), w_out.astype(BF16))
    out = _mlp(h2, row(norm2_gain), w_ff_in.astype(BF16), w_ff_out.astype(BF16))
    return out.reshape(b, t, d)


def kernel(x, norm1_gain, w_in, q_norm_gain, k_norm_gain, attn_sinks, mu_r, mu_k, mu_v, mu_w, mu_a, mu_g, decay_bias, decay_up, aaa_bias, aaa_up, gate_up, k_k, k_a, r_k, ln_x_gain, ln_x_bias, w_branch_attn, w_branch_rwkv, w_out, norm2_gain, w_ff_in, w_ff_out):
    h = x.astype(F32)
    params = (norm1_gain, w_in, q_norm_gain, k_norm_gain, attn_sinks, mu_r, mu_k, mu_v, mu_w,
              mu_a, mu_g, decay_bias, decay_up, aaa_bias, aaa_up, gate_up, k_k, k_a, r_k,
              ln_x_gain, ln_x_bias, w_branch_attn, w_branch_rwkv, w_out, norm2_gain,
              w_ff_in, w_ff_out)
    for l in range(norm1_gain.shape[0]):
        h = _layer(h, *(p[l] for p in params))
    return h.astype(x.dtype)
```

```python
import math

import jax
import jax.numpy as jnp
from jax import lax
from jax.experimental import pallas as pl
from jax.experimental.pallas import tpu as pltpu

F32 = jnp.float32
BF16 = jnp.bfloat16

HEAD_DIM = 64
ATTN_Q_HEADS = 8
ATTN_KV_HEADS = 2
ATTN_GROUP = ATTN_Q_HEADS // ATTN_KV_HEADS
WINDOW = 128
BLOCK = 128
RWKV_HEADS = 8
RWKV_HEAD_SIZE = 64
LORA_DECAY = 64
LORA_AAA = 64
LORA_GATE = 128
ATTN_Q_WIDTH = ATTN_Q_HEADS * HEAD_DIM
ATTN_KV_WIDTH = ATTN_KV_HEADS * HEAD_DIM
RWKV_WIDTH = RWKV_HEADS * RWKV_HEAD_SIZE
RWKV_IN_WIDTH = 3 * RWKV_WIDTH + LORA_DECAY + LORA_AAA + LORA_GATE
RMS_EPS = 1e-6
GN_EPS = 64e-5
L2_EPS = 1e-12

CHUNK = 64
ROW_TILE = 512
ATTN_TILE = 512
RWKV_TILE = 256
VMEM_LIMIT = 56 * 1024 * 1024

_NT = (((1,), (1,)), ((), ()))
_TN = (((0,), (0,)), ((), ()))


def _bdot(a, b):
    return jnp.dot(a.astype(BF16), b.astype(BF16), preferred_element_type=F32)


def _rms(x, gain):
    return x * lax.rsqrt(jnp.mean(x * x, axis=-1, keepdims=True) + RMS_EPS) * gain


def _sigmoid(x):
    return 0.5 * jnp.tanh(0.5 * x) + 0.5


def _inproj_kernel(x_ref, g_ref, w_ref, q_ref, kv_ref, rw_ref):
    u = _rms(x_ref[...], g_ref[...])
    p = _bdot(u, w_ref[...])
    q_ref[...] = p[:, :ATTN_Q_WIDTH]
    kv_ref[...] = p[:, ATTN_Q_WIDTH:ATTN_Q_WIDTH + 2 * ATTN_KV_WIDTH]
    rw_ref[...] = p[:, ATTN_Q_WIDTH + 2 * ATTN_KV_WIDTH:]


def _inproj(x2, gain, w):
    n, d = x2.shape
    wid = w.shape[1]
    tm = min(ROW_TILE, n)
    return pl.pallas_call(
        _inproj_kernel,
        grid=(n // tm,),
        in_specs=[
            pl.BlockSpec((tm, d), lambda i: (i, 0)),
            pl.BlockSpec((1, d), lambda i: (0, 0)),
            pl.BlockSpec((d, wid), lambda i: (0, 0)),
        ],
        out_specs=[
            pl.BlockSpec((tm, ATTN_Q_WIDTH), lambda i: (i, 0)),
            pl.BlockSpec((tm, 2 * ATTN_KV_WIDTH), lambda i: (i, 0)),
            pl.BlockSpec((tm, RWKV_IN_WIDTH), lambda i: (i, 0)),
        ],
        out_shape=[
            jax.ShapeDtypeStruct((n, ATTN_Q_WIDTH), F32),
            jax.ShapeDtypeStruct((n, 2 * ATTN_KV_WIDTH), F32),
            jax.ShapeDtypeStruct((n, RWKV_IN_WIDTH), F32),
        ],
        compiler_params=pltpu.CompilerParams(
            dimension_semantics=("arbitrary",), vmem_limit_bytes=VMEM_LIMIT),
        name="inproj",
    )(x2, gain, w)


def _head_sumsq(x, ones_bd):
    x2 = x * x
    hi = x2.astype(BF16)
    lo = (x2 - hi.astype(F32)).astype(BF16)
    return (jnp.dot(hi, ones_bd, preferred_element_type=F32)
            + jnp.dot(lo, ones_bd, preferred_element_type=F32))


def _attn_scores(q, kv_all, q_gain, k_gain, first_tile):
    hp = 2 * HEAD_DIM
    nblk = q.shape[0] // BLOCK
    even = lax.broadcasted_iota(jnp.int32, (1, hp), 1) < HEAD_DIM
    ri = lax.broadcasted_iota(jnp.int32, (hp, hp), 0)
    ci = lax.broadcasted_iota(jnp.int32, (hp, hp), 1)
    ones_bd = ((ri < HEAD_DIM) == (ci < HEAD_DIM)).astype(BF16)

    def head_rms(x, gain):
        ms = _head_sumsq(x, ones_bd) * (1.0 / HEAD_DIM)
        return x * lax.rsqrt(ms + RMS_EPS) * gain

    qn = [head_rms(q[:, m * hp:(m + 1) * hp], q_gain[:, m * hp:(m + 1) * hp]).astype(BF16)
          for m in range(ATTN_Q_WIDTH // hp)]
    kn = head_rms(kv_all[:, :hp], k_gain)
    vv = kv_all[:, hp:]
    kn_sw = pltpu.roll(kn, HEAD_DIM, 1)
    vv_sw = pltpu.roll(vv, HEAD_DIM, 1)
    kdup = [jnp.where(even, kn, kn_sw).astype(BF16), jnp.where(even, kn_sw, kn).astype(BF16)]
    vdup = [jnp.where(even, vv, vv_sw).astype(BF16), jnp.where(even, vv_sw, vv).astype(BF16)]

    rows = ATTN_GROUP * BLOCK
    qpos = (lax.broadcasted_iota(jnp.int32, (rows, 2 * BLOCK), 0) & (BLOCK - 1)) + BLOCK
    spos = lax.broadcasted_iota(jnp.int32, (rows, 2 * BLOCK), 1)
    rel = qpos - spos
    band = (rel >= 0) & (rel < WINDOW)
    band_first = band & ((spos >= BLOCK) | jnp.logical_not(first_tile))

    units = [(j, kh) for j in range(nblk) for kh in range(ATTN_KV_HEADS)]
    zero_q = jnp.zeros((BLOCK, hp), BF16)
    scores = []
    for j, kh in units:
        parts = []
        for g in range(ATTN_GROUP):
            h = kh * ATTN_GROUP + g
            col = qn[h // 2][j * BLOCK:(j + 1) * BLOCK]
            parts.append(jnp.where(even, col, zero_q) if h % 2 == 0 else jnp.where(even, zero_q, col))
        s = lax.dot_general(jnp.concatenate(parts, axis=0), kdup[kh][j * BLOCK:(j + 2) * BLOCK], _NT,
                            preferred_element_type=F32)
        scores.append(jnp.where(band_first if j == 0 else band, s, -jnp.inf))
    return units, scores, vdup


def _attn_softmax(kh, s, sink_ref):
    sink = jnp.concatenate(
        [jnp.full((BLOCK, 1), sink_ref[kh * ATTN_GROUP + g], F32) for g in range(ATTN_GROUP)], axis=0)
    m = jnp.maximum(jnp.max(s, axis=-1, keepdims=True), sink)
    p = jnp.exp(s - m)
    den = jnp.sum(p, axis=-1, keepdims=True) + jnp.exp(sink - m)
    return p.astype(BF16), den


def _attn_values(j, kh, p, den, vdup, o_ref):
    hp = 2 * HEAD_DIM
    even = lax.broadcasted_iota(jnp.int32, (1, hp), 1) < HEAD_DIM
    o4 = jnp.dot(p, vdup[kh][j * BLOCK:(j + 2) * BLOCK], preferred_element_type=F32) * (1.0 / den)
    for m in range(ATTN_GROUP // 2):
        o_e = o4[(2 * m) * BLOCK:(2 * m + 1) * BLOCK]
        o_o = o4[(2 * m + 1) * BLOCK:(2 * m + 2) * BLOCK]
        c0 = (kh * (ATTN_GROUP // 2) + m) * hp
        o_ref[j * BLOCK:(j + 1) * BLOCK, c0:c0 + hp] = jnp.where(even, o_e, o_o).astype(o_ref.dtype)


def _pair_blockdiag(x, even):
    zero = jnp.zeros_like(x)
    return jnp.concatenate([jnp.where(even, x, zero), jnp.where(even, zero, x)], axis=0)


def _pair_sum(x, even):
    s_e = jnp.sum(jnp.where(even, x, 0.0), axis=-1, keepdims=True)
    s_o = jnp.sum(jnp.where(even, 0.0, x), axis=-1, keepdims=True)
    return jnp.where(even, s_e, s_o)


def _rwkv_kernel(xin_ref, mu_ref, dbias_ref, dup_ref, abias_ref, aup_ref, gup_ref,
                 kk_ref, ka_ref, rk_ref, lng_ref, lnb_ref, o_ref, prev_ref, state_ref):
    c = CHUNK
    hp = 2 * RWKV_HEAD_SIZE
    w = RWKV_WIDTH
    tb = xin_ref.shape[1]
    nc = tb // c
    npair = w // hp

    @pl.when(pl.program_id(1) == 0)
    def _():
        prev_ref[...] = jnp.zeros_like(prev_ref)
        state_ref[...] = jnp.zeros_like(state_ref)

    xin = xin_ref[0]
    row = lax.broadcasted_iota(jnp.int32, (tb, 1), 0)
    prev = jnp.where(row == 0, prev_ref[...], pltpu.roll(xin, 1, 0))
    prev_ref[...] = xin[tb - 1:tb, :]
    xs = xin + (prev - xin) * mu_ref[...]

    r = xs[:, 0:w]
    k = xs[:, w:2 * w]
    v = xs[:, 2 * w:3 * w]
    w_lora = xs[:, 3 * w:3 * w + LORA_DECAY]
    a_lora = xs[:, 3 * w + LORA_DECAY:3 * w + LORA_DECAY + LORA_AAA]
    g_lora = xs[:, 3 * w + LORA_DECAY + LORA_AAA:]

    d = dbias_ref[...] + _bdot(jnp.tanh(w_lora), dup_ref[...])
    lw = (-math.exp(-0.5)) * _sigmoid(d)
    a_sig = _sigmoid(abias_ref[...] + _bdot(a_lora, aup_ref[...]))
    gate = _bdot(_sigmoid(g_lora), gup_ref[...])

    kkr = k * kk_ref[...]
    k2 = k * (1.0 + (a_sig - 1.0) * ka_ref[...])
    rk = r * k2 * rk_ref[...]
    even_t = lax.broadcasted_iota(jnp.int32, (tb, hp), 1) < RWKV_HEAD_SIZE
    cols = [slice(p * hp, (p + 1) * hp) for p in range(npair)]
    den = jnp.concatenate(
        [jnp.maximum(jnp.sqrt(_pair_sum(kkr[:, cs] * kkr[:, cs], even_t)), L2_EPS) for cs in cols],
        axis=1)
    bonus = jnp.concatenate([_pair_sum(rk[:, cs], even_t) for cs in cols], axis=1) * v
    kk = kkr / den
    a_s = -kk
    b_s = kk * a_sig

    ti = lax.broadcasted_iota(jnp.int32, (c, c), 0)
    si = lax.broadcasted_iota(jnp.int32, (c, c), 1)
    lower_f = (ti >= si).astype(F32)
    cl = jnp.concatenate(
        [jnp.dot(lower_f, lw[ci * c:(ci + 1) * c], preferred_element_type=F32,
                 precision=lax.Precision.HIGHEST) for ci in range(nc)], axis=0)
    e_c = jnp.exp(cl)
    e_neg = jnp.exp(-cl)
    g_rows = [e_c[ci * c + c - 1:ci * c + c, :] for ci in range(nc)]
    g_full = jnp.concatenate([jnp.broadcast_to(g, (c, w)) for g in g_rows], axis=0)
    at = a_s * jnp.exp(cl - lw)
    rt = r * e_c
    bt = b_s * e_neg
    kt = k2 * e_neg
    bh = (bt * g_full).astype(BF16)
    khat = (kt * g_full).astype(BF16)
    at_b, rt_b, bt_b, kt_b, v_b = (z.astype(BF16) for z in (at, rt, bt, kt, v))

    even = lax.broadcasted_iota(jnp.int32, (c, hp), 1) < RWKV_HEAD_SIZE
    gi = lax.broadcasted_iota(jnp.int32, (2 * hp, 2 * hp), 0)
    gj = lax.broadcasted_iota(jnp.int32, (2 * hp, 2 * hp), 1)
    g_t, g_s = gi & (c - 1), gj & (c - 1)
    g_mask = (g_t > g_s) | ((gi >= hp) & (g_t == g_s))
    pi = lax.broadcasted_iota(jnp.int32, (hp, hp), 0)
    pj = lax.broadcasted_iota(jnp.int32, (hp, hp), 1)
    eye = pi == pj

    units = [(ci, p) for ci in range(nc) for p in range(npair)]

    def blk(x, ci, p):
        return _pair_blockdiag(x[ci * c:(ci + 1) * c, cols[p]], even)

    gm = []
    for ci, p in units:
        lhs = jnp.concatenate([blk(at_b, ci, p), blk(rt_b, ci, p)], axis=0)
        rhs = jnp.concatenate([blk(bt_b, ci, p), blk(kt_b, ci, p)], axis=0)
        gm.append(jnp.where(g_mask, lax.dot_general(lhs, rhs, _NT, preferred_element_type=F32), 0.0))
    a_pow = [g[:hp, :hp].astype(BF16) for g in gm]
    t_inv = [jnp.where(eye, 1.0, g[:hp, :hp]) for g in gm]
    a_pow = [jnp.dot(a, a, preferred_element_type=F32).astype(BF16) for a in a_pow]
    for _ in range(int(math.log2(c)) - 2):
        res = [jnp.dot(a, jnp.concatenate([t.astype(BF16), a], axis=1), preferred_element_type=F32)
               for a, t in zip(a_pow, t_inv)]
        t_inv = [t + z[:, :hp] for t, z in zip(t_inv, res)]
        a_pow = [z[:, hp:].astype(BF16) for z in res]
    t_inv = [t + jnp.dot(a, t.astype(BF16), preferred_element_type=F32)
             for a, t in zip(a_pow, t_inv)]
    wx = [jnp.dot(t.astype(BF16),
                  jnp.concatenate([blk(at_b, ci, p), g[:hp, hp:].astype(BF16)], axis=1),
                  preferred_element_type=F32)
          for (ci, p), t, g in zip(units, t_inv, gm)]
    v_bd = [blk(v_b, ci, p) for ci, p in units]
    u0 = [jnp.dot(z[:, hp:].astype(BF16), vb, preferred_element_type=F32) for z, vb in zip(wx, v_bd)]
    rhs_ab = [jnp.concatenate(
        [jnp.concatenate([z[:, :hp].astype(BF16), u.astype(BF16)], axis=1),
         jnp.concatenate([jnp.zeros_like(vb), vb], axis=1)], axis=0)
        for z, u, vb in zip(wx, u0, v_bd)]
    ab_y = [jnp.dot(g[hp:, :].astype(BF16), rz, preferred_element_type=F32)
            for g, rz in zip(gm, rhs_ab)]
    ab_h = [lax.dot_general(jnp.concatenate([blk(bh, ci, p), blk(khat, ci, p)], axis=0), rz, _TN,
                            preferred_element_type=F32)
            for (ci, p), rz in zip(units, rhs_ab)]

    state = [state_ref[:, cs] for cs in cols]
    y_cols = [[None] * npair for _ in range(nc)]
    for idx, (ci, p) in enumerate(units):
        qmat = blk(rt, ci, p) + ab_y[idx][:, :hp]
        g_diag = jnp.where(eye, jnp.broadcast_to(g_rows[ci][:, cols[p]], (hp, hp)), 0.0)
        mmat = g_diag + ab_h[idx][:, :hp]
        ys = jnp.dot(jnp.concatenate([qmat, mmat], axis=0).astype(BF16), state[p].astype(BF16),
                     preferred_element_type=F32)
        y_bd = ys[:hp] + ab_y[idx][:, hp:]
        state[p] = ys[hp:] + ab_h[idx][:, hp:]
        y = y_bd[:c] + y_bd[c:]
        mean = _pair_sum(y, even) * (1.0 / RWKV_HEAD_SIZE)
        yc = y - mean
        var = _pair_sum(yc * yc, even) * (1.0 / RWKV_HEAD_SIZE)
        y_cols[ci][p] = yc * lax.rsqrt(var + GN_EPS)
    for p in range(npair):
        state_ref[:, cols[p]] = state[p]

    yn = jnp.concatenate([jnp.concatenate(yc_, axis=1) for yc_ in y_cols], axis=0)
    out = (yn * lng_ref[...] + lnb_ref[...] + bonus) * gate
    o_ref[0] = out.astype(o_ref.dtype)


def _rwkv(rw, mu, decay_bias, decay_up, aaa_bias, aaa_up, gate_up, k_k, k_a, r_k, ln_g, ln_b):
    b, t, _ = rw.shape
    tb = min(RWKV_TILE, t)
    w = RWKV_WIDTH

    def full(shape):
        return pl.BlockSpec(shape, lambda bi, ti: (0, 0))

    return pl.pallas_call(
        _rwkv_kernel,
        grid=(b, t // tb),
        in_specs=[
            pl.BlockSpec((1, tb, RWKV_IN_WIDTH), lambda bi, ti: (bi, ti, 0)),
            full((1, RWKV_IN_WIDTH)),
            full((1, w)), full((LORA_DECAY, w)),
            full((1, w)), full((LORA_AAA, w)),
            full((LORA_GATE, w)),
            full((1, w)), full((1, w)), full((1, w)), full((1, w)), full((1, w)),
        ],
        out_specs=pl.BlockSpec((1, tb, w), lambda bi, ti: (bi, ti, 0)),
        out_shape=jax.ShapeDtypeStruct((b, t, w), BF16),
        scratch_shapes=[
            pltpu.VMEM((1, RWKV_IN_WIDTH), F32),
            pltpu.VMEM((2 * RWKV_HEAD_SIZE, w), F32),
        ],
        compiler_params=pltpu.CompilerParams(
            dimension_semantics=("arbitrary", "arbitrary"), vmem_limit_bytes=VMEM_LIMIT),
        name="rwkv",
    )(rw, mu, decay_bias, decay_up, aaa_bias, aaa_up, gate_up, k_k, k_a, r_k, ln_g, ln_b)


def _attn_merge_kernel(x_ref, q_ref, kv_ref, kvp_ref, rwkv_ref, g_ref, qg_ref, kg_ref, sink_ref,
                       wg_ref, wba_ref, wbr_ref, wo_ref, h_ref, attn_ref):
    x = x_ref[...]
    d = x.shape[1]
    u = _rms(x, g_ref[...])
    kv_all = jnp.concatenate([kvp_ref[0], kv_ref[0]], axis=0)
    units, scores, vdup = _attn_scores(q_ref[0], kv_all, qg_ref[...], kg_ref[...],
                                       pl.program_id(1) == 0)
    ub = u.astype(BF16)
    n_u = len(units)
    gw = wg_ref.shape[1] // n_u
    gate_parts, soft = [], []
    for idx, ((j, kh), s) in enumerate(zip(units, scores)):
        gate_parts.append(jnp.dot(ub, wg_ref[:, idx * gw:(idx + 1) * gw], preferred_element_type=F32))
        soft.append(_attn_softmax(kh, s, sink_ref))
    gates = jnp.concatenate(gate_parts, axis=1)
    rwkv_b = rwkv_ref[...]
    n_r = n_u // 2
    rw = wbr_ref.shape[1] // n_r
    rwkv_parts = []
    for c in range(n_r):
        rwkv_parts.append(jnp.dot(rwkv_b, wbr_ref[:, c * rw:(c + 1) * rw], preferred_element_type=F32))
        for idx in (2 * c, 2 * c + 1):
            _attn_values(*units[idx], *soft[idx], vdup, attn_ref)
    y_rwkv = jnp.concatenate(rwkv_parts, axis=1)
    y_attn = jnp.dot(attn_ref[...], wba_ref[...], preferred_element_type=F32)
    mixed = _sigmoid(gates[:, :d]) * y_attn + _sigmoid(gates[:, d:]) * y_rwkv
    h_ref[...] = x + _bdot(mixed, wo_ref[...])


def _attn_merge(x2, q, kv, rwkv, gain, q_gain, k_gain, sinks, w_gate, w_ba, w_br, w_out):
    n, d = x2.shape
    b, t, _ = q.shape
    tq = min(ATTN_TILE, t)
    nblk = tq // BLOCK
    nt = t // tq
    q_gain = jnp.tile(q_gain * (HEAD_DIM ** -0.5), (1, ATTN_Q_HEADS))
    k_gain = jnp.tile(k_gain, (1, ATTN_KV_HEADS))

    def full(a):
        return pl.BlockSpec(a.shape, lambda bi, i: (0, 0))

    def rows(width):
        return pl.BlockSpec((tq, width), lambda bi, i: (bi * nt + i, 0))

    return pl.pallas_call(
        _attn_merge_kernel,
        grid=(b, nt),
        in_specs=[
            rows(d),
            pl.BlockSpec((1, tq, ATTN_Q_WIDTH), lambda bi, i: (bi, i, 0)),
            pl.BlockSpec((1, tq, 2 * ATTN_KV_WIDTH), lambda bi, i: (bi, i, 0)),
            pl.BlockSpec((1, BLOCK, 2 * ATTN_KV_WIDTH),
                         lambda bi, i: (bi, jnp.maximum(i * nblk - 1, 0), 0)),
            rows(rwkv.shape[1]),
            full(gain), full(q_gain), full(k_gain),
            pl.BlockSpec(memory_space=pltpu.SMEM),
            full(w_gate), full(w_ba), full(w_br), full(w_out),
        ],
        out_specs=rows(d),
        out_shape=jax.ShapeDtypeStruct((n, d), F32),
        scratch_shapes=[pltpu.VMEM((tq, ATTN_Q_WIDTH), BF16)],
        compiler_params=pltpu.CompilerParams(
            dimension_semantics=("arbitrary", "arbitrary"), vmem_limit_bytes=VMEM_LIMIT),
        name="attn_merge",
    )(x2, q, kv, kv, rwkv, gain, q_gain, k_gain, sinks, w_gate, w_ba, w_br, w_out)


def _mlp_kernel(h_ref, g_ref, w1_ref, w2_ref, o_ref):
    h = h_ref[...]
    hidden = jnp.square(jnp.maximum(_bdot(_rms(h, g_ref[...]), w1_ref[...]), 0.0))
    o_ref[...] = h + _bdot(hidden, w2_ref[...])


def _mlp(h2, gain, w1, w2):
    n, d = h2.shape
    tm = min(ROW_TILE // 2, n)

    def full(a):
        return pl.BlockSpec(a.shape, lambda i: (0, 0))

    return pl.pallas_call(
        _mlp_kernel,
        grid=(n // tm,),
        in_specs=[pl.BlockSpec((tm, d), lambda i: (i, 0)), full(gain), full(w1), full(w2)],
        out_specs=pl.BlockSpec((tm, d), lambda i: (i, 0)),
        out_shape=jax.ShapeDtypeStruct((n, d), F32),
        compiler_params=pltpu.CompilerParams(
            dimension_semantics=("arbitrary",), vmem_limit_bytes=VMEM_LIMIT),
        name="mlp",
    )(h2, gain, w1, w2)


def _layer(h, norm1_gain, w_in, q_norm_gain, k_norm_gain, attn_sinks,
           mu_r, mu_k, mu_v, mu_w, mu_a, mu_g, decay_bias, decay_up, aaa_bias, aaa_up,
           gate_up, k_k, k_a, r_k, ln_x_gain, ln_x_bias, w_branch_attn, w_branch_rwkv,
           w_out, norm2_gain, w_ff_in, w_ff_out):
    b, t, d = h.shape
    x2 = h.reshape(b * t, d)
    row = lambda a: a.reshape(1, -1).astype(F32)
    n_proj = ATTN_Q_WIDTH + 2 * ATTN_KV_WIDTH + RWKV_IN_WIDTH

    q, kv, rw = _inproj(x2, row(norm1_gain), w_in[:, :n_proj].astype(BF16))
    mu = jnp.concatenate([mu_r, mu_k, mu_v, mu_w, mu_a, mu_g]).reshape(1, -1).astype(F32)
    rwkv = _rwkv(rw.reshape(b, t, -1), mu, row(decay_bias), decay_up.astype(BF16),
                 row(aaa_bias), aaa_up.astype(BF16), gate_up.astype(BF16),
                 row(k_k), row(k_a), row(r_k), row(ln_x_gain), row(ln_x_bias))
    h2 = _attn_merge(x2, q.reshape(b, t, -1), kv.reshape(b, t, -1), rwkv.reshape(b * t, -1),
                     row(norm1_gain), row(q_norm_gain), row(k_norm_gain), attn_sinks.astype(F32),
                     w_in[:, n_proj:].astype(BF16), w_branch_attn.astype(BF16),
                     w_branch_rwkv.astype(BF16), w_out.astype(BF16))
    out = _mlp(h2, row(norm2_gain), w_ff_in.astype(BF16), w_ff_out.astype(BF16))
    return out.reshape(b, t, d)


def kernel(x, norm1_gain, w_in, q_norm_gain, k_norm_gain, attn_sinks, mu_r, mu_k, mu_v, mu_w, mu_a, mu_g, decay_bias, decay_up, aaa_bias, aaa_up, gate_up, k_k, k_a, r_k, ln_x_gain, ln_x_bias, w_branch_attn, w_branch_rwkv, w_out, norm2_gain, w_ff_in, w_ff_out):
    h = x.astype(F32)
    params = (norm1_gain, w_in, q_norm_gain, k_norm_gain, attn_sinks, mu_r, mu_k, mu_v, mu_w,
              mu_a, mu_g, decay_bias, decay_up, aaa_bias, aaa_up, gate_up, k_k, k_a, r_k,
              ln_x_gain, ln_x_bias, w_branch_attn, w_branch_rwkv, w_out, norm2_gain,
              w_ff_in, w_ff_out)
    for l in range(norm1_gain.shape[0]):
        h = _layer(h, *(p[l] for p in params))
    return h.astype(x.dtype)
```

```python
import math

import jax
import jax.numpy as jnp
from jax import lax
from jax.experimental import pallas as pl
from jax.experimental.pallas import tpu as pltpu

F32 = jnp.float32
BF16 = jnp.bfloat16

HEAD_DIM = 64
ATTN_Q_HEADS = 8
ATTN_KV_HEADS = 2
ATTN_GROUP = ATTN_Q_HEADS // ATTN_KV_HEADS
WINDOW = 128
BLOCK = 128
RWKV_HEADS = 8
RWKV_HEAD_SIZE = 64
LORA_DECAY = 64
LORA_AAA = 64
LORA_GATE = 128
ATTN_Q_WIDTH = ATTN_Q_HEADS * HEAD_DIM
ATTN_KV_WIDTH = ATTN_KV_HEADS * HEAD_DIM
RWKV_WIDTH = RWKV_HEADS * RWKV_HEAD_SIZE
RWKV_IN_WIDTH = 3 * RWKV_WIDTH + LORA_DECAY + LORA_AAA + LORA_GATE
RMS_EPS = 1e-6
GN_EPS = 64e-5
L2_EPS = 1e-12

CHUNK = 64
ROW_TILE = 512
ATTN_TILE = 512
RWKV_TILE = 256
VMEM_LIMIT = 56 * 1024 * 1024

_NT = (((1,), (1,)), ((), ()))
_TN = (((0,), (0,)), ((), ()))


def _bdot(a, b):
    return jnp.dot(a.astype(BF16), b.astype(BF16), preferred_element_type=F32)


def _rms(x, gain):
    return x * lax.rsqrt(jnp.mean(x * x, axis=-1, keepdims=True) + RMS_EPS) * gain


def _sigmoid(x):
    return 0.5 * jnp.tanh(0.5 * x) + 0.5


def _inproj_kernel(x_ref, g_ref, w_ref, q_ref, kv_ref, rw_ref):
    u = _rms(x_ref[...], g_ref[...])
    p = _bdot(u, w_ref[...])
    q_ref[...] = p[:, :ATTN_Q_WIDTH]
    kv_ref[...] = p[:, ATTN_Q_WIDTH:ATTN_Q_WIDTH + 2 * ATTN_KV_WIDTH]
    rw_ref[...] = p[:, ATTN_Q_WIDTH + 2 * ATTN_KV_WIDTH:]


def _inproj(x2, gain, w):
    n, d = x2.shape
    wid = w.shape[1]
    tm = min(ROW_TILE, n)
    return pl.pallas_call(
        _inproj_kernel,
        grid=(n // tm,),
        in_specs=[
            pl.BlockSpec((tm, d), lambda i: (i, 0)),
            pl.BlockSpec((1, d), lambda i: (0, 0)),
            pl.BlockSpec((d, wid), lambda i: (0, 0)),
        ],
        out_specs=[
            pl.BlockSpec((tm, ATTN_Q_WIDTH), lambda i: (i, 0)),
            pl.BlockSpec((tm, 2 * ATTN_KV_WIDTH), lambda i: (i, 0)),
            pl.BlockSpec((tm, RWKV_IN_WIDTH), lambda i: (i, 0)),
        ],
        out_shape=[
            jax.ShapeDtypeStruct((n, ATTN_Q_WIDTH), F32),
            jax.ShapeDtypeStruct((n, 2 * ATTN_KV_WIDTH), F32),
            jax.ShapeDtypeStruct((n, RWKV_IN_WIDTH), F32),
        ],
        compiler_params=pltpu.CompilerParams(
            dimension_semantics=("arbitrary",), vmem_limit_bytes=VMEM_LIMIT),
        name="inproj",
    )(x2, gain, w)


def _head_sumsq(x, ones_bd):
    x2 = x * x
    hi = x2.astype(BF16)
    lo = (x2 - hi.astype(F32)).astype(BF16)
    return (jnp.dot(hi, ones_bd, preferred_element_type=F32)
            + jnp.dot(lo, ones_bd, preferred_element_type=F32))


def _attn_prep_kv(kv_all, q_gain, k_gain):
    hp = 2 * HEAD_DIM
    even = lax.broadcasted_iota(jnp.int32, (1, hp), 1) < HEAD_DIM
    ri = lax.broadcasted_iota(jnp.int32, (hp, hp), 0)
    ci = lax.broadcasted_iota(jnp.int32, (hp, hp), 1)
    ones_bd = ((ri < HEAD_DIM) == (ci < HEAD_DIM)).astype(BF16)
    k = kv_all[:, :hp]
    kn = k * lax.rsqrt(_head_sumsq(k, ones_bd) * (1.0 / HEAD_DIM) + RMS_EPS) * k_gain * q_gain
    kn_sw = pltpu.roll(kn, HEAD_DIM, 1)
    kdup = [jnp.where(even, kn, kn_sw).astype(BF16), jnp.where(even, kn_sw, kn).astype(BF16)]
    v_t = kv_all[:, hp:].T.astype(BF16)
    return kdup, v_t


def _attn_prep_q(q):
    hp = 2 * HEAD_DIM
    si = lax.broadcasted_iota(jnp.int32, (16, hp), 0)
    sj = lax.broadcasted_iota(jnp.int32, (16, hp), 1)
    sel = (((si == 0) & (sj < HEAD_DIM)) | ((si == 1) & (sj >= HEAD_DIM))).astype(BF16)
    q2 = q * q
    hi = q2.astype(BF16)
    lo = (q2 - hi.astype(F32)).astype(BF16)
    inv_rms = []
    for m in range(ATTN_Q_WIDTH // hp):
        cs = slice(m * hp, (m + 1) * hp)
        ss = (lax.dot_general(sel, hi[:, cs], _NT, preferred_element_type=F32)
              + lax.dot_general(sel, lo[:, cs], _NT, preferred_element_type=F32))
        inv_rms.append(lax.rsqrt(ss * (1.0 / HEAD_DIM) + RMS_EPS))
    return q.astype(BF16), inv_rms


def _attn_scores(j, kh, qb, kdup, inv_rms, no_prev):
    hp = 2 * HEAD_DIM
    even = lax.broadcasted_iota(jnp.int32, (1, hp), 1) < HEAD_DIM
    zero_q = jnp.zeros((BLOCK, hp), BF16)
    parts, scale = [], []
    for g in range(ATTN_GROUP):
        h = kh * ATTN_GROUP + g
        col = qb[j * BLOCK:(j + 1) * BLOCK, (h // 2) * hp:(h // 2 + 1) * hp]
        parts.append(jnp.where(even, col, zero_q) if h % 2 == 0 else jnp.where(even, zero_q, col))
        scale.append(inv_rms[h // 2][h % 2:h % 2 + 1, j * BLOCK:(j + 1) * BLOCK])
    s_t = lax.dot_general(kdup[kh][j * BLOCK:(j + 2) * BLOCK], jnp.concatenate(parts, axis=0), _NT,
                          preferred_element_type=F32)
    key = lax.broadcasted_iota(jnp.int32, (BLOCK, ATTN_GROUP * BLOCK), 0)
    qry = lax.broadcasted_iota(jnp.int32, (BLOCK, ATTN_GROUP * BLOCK), 1) & (BLOCK - 1)
    use_prev = key > qry
    prev = s_t[:BLOCK]
    if no_prev is not None:
        prev = jnp.where(no_prev, -jnp.inf, prev)
    folded = jnp.where(use_prev, prev, s_t[BLOCK:]) * jnp.concatenate(scale, axis=1)
    return folded, use_prev


def _attn_softmax(kh, folded, use_prev, sink_ref):
    sink = jnp.concatenate(
        [jnp.full((1, BLOCK), sink_ref[kh * ATTN_GROUP + g], F32) for g in range(ATTN_GROUP)], axis=1)
    m = jnp.maximum(jnp.max(folded, axis=0, keepdims=True), sink)
    p = jnp.exp(folded - m)
    den = jnp.sum(p, axis=0, keepdims=True) + jnp.exp(sink - m)
    zero = jnp.zeros_like(p)
    p_t = jnp.concatenate([jnp.where(use_prev, p, zero), jnp.where(use_prev, zero, p)], axis=0)
    return p_t.astype(BF16), 1.0 / den


def _attn_values(j, kh, p_t, inv_den, v_t, o_ref):
    o_t = jnp.dot(v_t[kh * HEAD_DIM:(kh + 1) * HEAD_DIM, j * BLOCK:(j + 2) * BLOCK], p_t,
                  preferred_element_type=F32) * inv_den
    for g in range(ATTN_GROUP):
        h = kh * ATTN_GROUP + g
        o_ref[h * HEAD_DIM:(h + 1) * HEAD_DIM, j * BLOCK:(j + 1) * BLOCK] = (
            o_t[:, g * BLOCK:(g + 1) * BLOCK].astype(o_ref.dtype))


def _pair_blockdiag(x, even):
    zero = jnp.zeros_like(x)
    return jnp.concatenate([jnp.where(even, x, zero), jnp.where(even, zero, x)], axis=0)


def _pair_sum(x, even):
    s_e = jnp.sum(jnp.where(even, x, 0.0), axis=-1, keepdims=True)
    s_o = jnp.sum(jnp.where(even, 0.0, x), axis=-1, keepdims=True)
    return jnp.where(even, s_e, s_o)


def _rwkv_kernel(xin_ref, mu_ref, dbias_ref, dup_ref, abias_ref, aup_ref, gup_ref,
                 kk_ref, ka_ref, rk_ref, lng_ref, lnb_ref, o_ref, prev_ref, state_ref):
    c = CHUNK
    hp = 2 * RWKV_HEAD_SIZE
    w = RWKV_WIDTH
    tb = xin_ref.shape[1]
    nc = tb // c
    npair = w // hp

    @pl.when(pl.program_id(1) == 0)
    def _():
        prev_ref[...] = jnp.zeros_like(prev_ref)
        state_ref[...] = jnp.zeros_like(state_ref)

    xin = xin_ref[0]
    row = lax.broadcasted_iota(jnp.int32, (tb, 1), 0)
    prev = jnp.where(row == 0, prev_ref[...], pltpu.roll(xin, 1, 0))
    prev_ref[...] = xin[tb - 1:tb, :]
    xs = xin + (prev - xin) * mu_ref[...]

    r = xs[:, 0:w]
    k = xs[:, w:2 * w]
    v = xs[:, 2 * w:3 * w]
    w_lora = xs[:, 3 * w:3 * w + LORA_DECAY]
    a_lora = xs[:, 3 * w + LORA_DECAY:3 * w + LORA_DECAY + LORA_AAA]
    g_lora = xs[:, 3 * w + LORA_DECAY + LORA_AAA:]

    d = dbias_ref[...] + _bdot(jnp.tanh(w_lora), dup_ref[...])
    lw = (-math.exp(-0.5)) * _sigmoid(d)
    a_sig = _sigmoid(abias_ref[...] + _bdot(a_lora, aup_ref[...]))
    gate = _bdot(_sigmoid(g_lora), gup_ref[...])

    kkr = k * kk_ref[...]
    k2 = k * (1.0 + (a_sig - 1.0) * ka_ref[...])
    rk = r * k2 * rk_ref[...]
    even_t = lax.broadcasted_iota(jnp.int32, (tb, hp), 1) < RWKV_HEAD_SIZE
    cols = [slice(p * hp, (p + 1) * hp) for p in range(npair)]
    den = jnp.concatenate(
        [jnp.maximum(jnp.sqrt(_pair_sum(kkr[:, cs] * kkr[:, cs], even_t)), L2_EPS) for cs in cols],
        axis=1)
    bonus = jnp.concatenate([_pair_sum(rk[:, cs], even_t) for cs in cols], axis=1) * v
    kk = kkr / den
    a_s = -kk
    b_s = kk * a_sig

    ti = lax.broadcasted_iota(jnp.int32, (c, c), 0)
    si = lax.broadcasted_iota(jnp.int32, (c, c), 1)
    lower_f = (ti >= si).astype(F32)
    cl = jnp.concatenate(
        [jnp.dot(lower_f, lw[ci * c:(ci + 1) * c], preferred_element_type=F32,
                 precision=lax.Precision.HIGHEST) for ci in range(nc)], axis=0)
    e_c = jnp.exp(cl)
    e_neg = jnp.exp(-cl)
    g_rows = [e_c[ci * c + c - 1:ci * c + c, :] for ci in range(nc)]
    g_full = jnp.concatenate([jnp.broadcast_to(g, (c, w)) for g in g_rows], axis=0)
    at = a_s * jnp.exp(cl - lw)
    rt = r * e_c
    bt = b_s * e_neg
    kt = k2 * e_neg
    bh = (bt * g_full).astype(BF16)
    khat = (kt * g_full).astype(BF16)
    at_b, rt_b, bt_b, kt_b, v_b = (z.astype(BF16) for z in (at, rt, bt, kt, v))

    even = lax.broadcasted_iota(jnp.int32, (c, hp), 1) < RWKV_HEAD_SIZE
    gi = lax.broadcasted_iota(jnp.int32, (2 * hp, 2 * hp), 0)
    gj = lax.broadcasted_iota(jnp.int32, (2 * hp, 2 * hp), 1)
    g_t, g_s = gi & (c - 1), gj & (c - 1)
    g_mask = (g_t > g_s) | ((gi >= hp) & (g_t == g_s))
    pi = lax.broadcasted_iota(jnp.int32, (hp, hp), 0)
    pj = lax.broadcasted_iota(jnp.int32, (hp, hp), 1)
    eye = pi == pj

    units = [(ci, p) for ci in range(nc) for p in range(npair)]

    def blk(x, ci, p):
        return _pair_blockdiag(x[ci * c:(ci + 1) * c, cols[p]], even)

    gm = []
    for ci, p in units:
        lhs = jnp.concatenate([blk(at_b, ci, p), blk(rt_b, ci, p)], axis=0)
        rhs = jnp.concatenate([blk(bt_b, ci, p), blk(kt_b, ci, p)], axis=0)
        gm.append(jnp.where(g_mask, lax.dot_general(lhs, rhs, _NT, preferred_element_type=F32), 0.0))
    a_pow = [g[:hp, :hp].astype(BF16) for g in gm]
    t_inv = [jnp.where(eye, 1.0, g[:hp, :hp]) for g in gm]
    a_pow = [jnp.dot(a, a, preferred_element_type=F32).astype(BF16) for a in a_pow]
    for _ in range(int(math.log2(c)) - 2):
        res = [jnp.dot(a, jnp.concatenate([t.astype(BF16), a], axis=1), preferred_element_type=F32)
               for a, t in zip(a_pow, t_inv)]
        t_inv = [t + z[:, :hp] for t, z in zip(t_inv, res)]
        a_pow = [z[:, hp:].astype(BF16) for z in res]
    t_inv = [t + jnp.dot(a, t.astype(BF16), preferred_element_type=F32)
             for a, t in zip(a_pow, t_inv)]
    wx = [jnp.dot(t.astype(BF16),
                  jnp.concatenate([blk(at_b, ci, p), g[:hp, hp:].astype(BF16)], axis=1),
                  preferred_element_type=F32)
          for (ci, p), t, g in zip(units, t_inv, gm)]
    v_bd = [blk(v_b, ci, p) for ci, p in units]
    u0 = [jnp.dot(z[:, hp:].astype(BF16), vb, preferred_element_type=F32) for z, vb in zip(wx, v_bd)]
    rhs_ab = [jnp.concatenate(
        [jnp.concatenate([z[:, :hp].astype(BF16), u.astype(BF16)], axis=1),
         jnp.concatenate([jnp.zeros_like(vb), vb], axis=1)], axis=0)
        for z, u, vb in zip(wx, u0, v_bd)]
    ab_y = [jnp.dot(g[hp:, :].astype(BF16), rz, preferred_element_type=F32)
            for g, rz in zip(gm, rhs_ab)]
    ab_h = [lax.dot_general(jnp.concatenate([blk(bh, ci, p), blk(khat, ci, p)], axis=0), rz, _TN,
                            preferred_element_type=F32)
            for (ci, p), rz in zip(units, rhs_ab)]

    state = [state_ref[:, cs] for cs in cols]
    y_cols = [[None] * npair for _ in range(nc)]
    for idx, (ci, p) in enumerate(units):
        qmat = blk(rt, ci, p) + ab_y[idx][:, :hp]
        g_diag = jnp.where(eye, jnp.broadcast_to(g_rows[ci][:, cols[p]], (hp, hp)), 0.0)
        mmat = g_diag + ab_h[idx][:, :hp]
        ys = jnp.dot(jnp.concatenate([qmat, mmat], axis=0).astype(BF16), state[p].astype(BF16),
                     preferred_element_type=F32)
        y_bd = ys[:hp] + ab_y[idx][:, hp:]
        state[p] = ys[hp:] + ab_h[idx][:, hp:]
        y = y_bd[:c] + y_bd[c:]
        mean = _pair_sum(y, even) * (1.0 / RWKV_HEAD_SIZE)
        yc = y - mean
        var = _pair_sum(yc * yc, even) * (1.0 / RWKV_HEAD_SIZE)
        y_cols[ci][p] = yc * lax.rsqrt(var + GN_EPS)
    for p in range(npair):
        state_ref[:, cols[p]] = state[p]

    yn = jnp.concatenate([jnp.concatenate(yc_, axis=1) for yc_ in y_cols], axis=0)
    out = (yn * lng_ref[...] + lnb_ref[...] + bonus) * gate
    o_ref[0] = out.astype(o_ref.dtype)


def _rwkv(rw, mu, decay_bias, decay_up, aaa_bias, aaa_up, gate_up, k_k, k_a, r_k, ln_g, ln_b):
    b, t, _ = rw.shape
    tb = min(RWKV_TILE, t)
    w = RWKV_WIDTH

    def full(shape):
        return pl.BlockSpec(shape, lambda bi, ti: (0, 0))

    return pl.pallas_call(
        _rwkv_kernel,
        grid=(b, t // tb),
        in_specs=[
            pl.BlockSpec((1, tb, RWKV_IN_WIDTH), lambda bi, ti: (bi, ti, 0)),
            full((1, RWKV_IN_WIDTH)),
            full((1, w)), full((LORA_DECAY, w)),
            full((1, w)), full((LORA_AAA, w)),
            full((LORA_GATE, w)),
            full((1, w)), full((1, w)), full((1, w)), full((1, w)), full((1, w)),
        ],
        out_specs=pl.BlockSpec((1, tb, w), lambda bi, ti: (bi, ti, 0)),
        out_shape=jax.ShapeDtypeStruct((b, t, w), BF16),
        scratch_shapes=[
            pltpu.VMEM((1, RWKV_IN_WIDTH), F32),
            pltpu.VMEM((2 * RWKV_HEAD_SIZE, w), F32),
        ],
        compiler_params=pltpu.CompilerParams(
            dimension_semantics=("arbitrary", "arbitrary"), vmem_limit_bytes=VMEM_LIMIT),
        name="rwkv",
    )(rw, mu, decay_bias, decay_up, aaa_bias, aaa_up, gate_up, k_k, k_a, r_k, ln_g, ln_b)


def _attn_merge_kernel(x_ref, q_ref, kv_ref, kvp_ref, rwkv_ref, g_ref, qg_ref, kg_ref, sink_ref,
                       wg_ref, wba_ref, wbr_ref, wo_ref, h_ref, attn_ref):
    x = x_ref[...]
    d = x.shape[1]
    nblk = q_ref.shape[1] // BLOCK
    units = [(j, kh) for j in range(nblk) for kh in range(ATTN_KV_HEADS)]
    n_u = len(units)
    gw = wg_ref.shape[1] // n_u
    gate_parts = []

    def gate_slice():
        c0 = len(gate_parts) * gw
        gate_parts.append(jnp.dot(ub, wg_ref[:, c0:c0 + gw], preferred_element_type=F32))

    y_rwkv = jnp.dot(rwkv_ref[...], wbr_ref[...], preferred_element_type=F32)
    ub = _rms(x, g_ref[...]).astype(BF16)
    gate_slice()
    kv_all = jnp.concatenate([kvp_ref[0], kv_ref[0]], axis=0)
    kdup, v_t = _attn_prep_kv(kv_all, qg_ref[...], kg_ref[...])
    gate_slice()
    qb, inv_rms = _attn_prep_q(q_ref[0])
    first_tile = pl.program_id(1) == 0
    scores = [_attn_scores(j, kh, qb, kdup, inv_rms, first_tile if j == 0 else None)
              for j, kh in units]
    soft = []
    for idx, (_, kh) in enumerate(units):
        if len(gate_parts) < n_u and idx % 4 != 1:
            gate_slice()
        soft.append(_attn_softmax(kh, *scores[idx], sink_ref))
    while len(gate_parts) < n_u:
        gate_slice()
    gates = jnp.concatenate(gate_parts, axis=1)
    for unit, sm in zip(units, soft):
        _attn_values(*unit, *sm, v_t, attn_ref)
    y_attn = lax.dot_general(attn_ref[...], wba_ref[...], _TN, preferred_element_type=F32)
    mixed = _sigmoid(gates[:, :d]) * y_attn + _sigmoid(gates[:, d:]) * y_rwkv
    h_ref[...] = x + _bdot(mixed, wo_ref[...])


def _attn_merge(x2, q, kv, rwkv, gain, q_gain, k_gain, sinks, w_gate, w_ba, w_br, w_out):
    n, d = x2.shape
    b, t, _ = q.shape
    tq = min(ATTN_TILE, t)
    nblk = tq // BLOCK
    nt = t // tq
    q_gain = jnp.tile(q_gain * (HEAD_DIM ** -0.5), (1, ATTN_KV_HEADS))
    k_gain = jnp.tile(k_gain, (1, ATTN_KV_HEADS))

    def full(a):
        return pl.BlockSpec(a.shape, lambda bi, i: (0, 0))

    def rows(width):
        return pl.BlockSpec((tq, width), lambda bi, i: (bi * nt + i, 0))

    return pl.pallas_call(
        _attn_merge_kernel,
        grid=(b, nt),
        in_specs=[
            rows(d),
            pl.BlockSpec((1, tq, ATTN_Q_WIDTH), lambda bi, i: (bi, i, 0)),
            pl.BlockSpec((1, tq, 2 * ATTN_KV_WIDTH), lambda bi, i: (bi, i, 0)),
            pl.BlockSpec((1, BLOCK, 2 * ATTN_KV_WIDTH),
                         lambda bi, i: (bi, jnp.maximum(i * nblk - 1, 0), 0)),
            rows(rwkv.shape[1]),
            full(gain), full(q_gain), full(k_gain),
            pl.BlockSpec(memory_space=pltpu.SMEM),
            full(w_gate), full(w_ba), full(w_br), full(w_out),
        ],
        out_specs=rows(d),
        out_shape=jax.ShapeDtypeStruct((n, d), F32),
        scratch_shapes=[pltpu.VMEM((ATTN_Q_WIDTH, tq), BF16)],
        compiler_params=pltpu.CompilerParams(
            dimension_semantics=("arbitrary", "arbitrary"), vmem_limit_bytes=VMEM_LIMIT),
        name="attn_merge",
    )(x2, q, kv, kv, rwkv, gain, q_gain, k_gain, sinks, w_gate, w_ba, w_br, w_out)


def _mlp_kernel(h_ref, g_ref, w1_ref, w2_ref, o_ref):
    h = h_ref[...]
    hidden = jnp.square(jnp.maximum(_bdot(_rms(h, g_ref[...]), w1_ref[...]), 0.0))
    o_ref[...] = h + _bdot(hidden, w2_ref[...])


def _mlp(h2, gain, w1, w2):
    n, d = h2.shape
    tm = min(ROW_TILE // 2, n)

    def full(a):
        return pl.BlockSpec(a.shape, lambda i: (0, 0))

    return pl.pallas_call(
        _mlp_kernel,
        grid=(n // tm,),
        in_specs=[pl.BlockSpec((tm, d), lambda i: (i, 0)), full(gain), full(w1), full(w2)],
        out_specs=pl.BlockSpec((tm, d), lambda i: (i, 0)),
        out_shape=jax.ShapeDtypeStruct((n, d), F32),
        compiler_params=pltpu.CompilerParams(
            dimension_semantics=("arbitrary",), vmem_limit_bytes=VMEM_LIMIT),
        name="mlp",
    )(h2, gain, w1, w2)


def _layer(h, norm1_gain, w_in, q_norm_gain, k_norm_gain, attn_sinks,
           mu_r, mu_k, mu_v, mu_w, mu_a, mu_g, decay_bias, decay_up, aaa_bias, aaa_up,
           gate_up, k_k, k_a, r_k, ln_x_gain, ln_x_bias, w_branch_attn, w_branch_rwkv,
           w_out, norm2_gain, w_ff_in, w_ff_out):
    b, t, d = h.shape
    x2 = h.reshape(b * t, d)
    row = lambda a: a.reshape(1, -1).astype(F32)
    n_proj = ATTN_Q_WIDTH + 2 * ATTN_KV_WIDTH + RWKV_IN_WIDTH

    q, kv, rw = _inproj(x2, row(norm1_gain), w_in[:, :n_proj].astype(BF16))
    mu = jnp.concatenate([mu_r, mu_k, mu_v, mu_w, mu_a, mu_g]).reshape(1, -1).astype(F32)
    rwkv = _rwkv(rw.reshape(b, t, -1), mu, row(decay_bias), decay_up.astype(BF16),
                 row(aaa_bias), aaa_up.astype(BF16), gate_up.astype(BF16),
                 row(k_k), row(k_a), row(r_k), row(ln_x_gain), row(ln_x_bias))
    h2 = _attn_merge(x2, q.reshape(b, t, -1), kv.reshape(b, t, -1), rwkv.reshape(b * t, -1),
                     row(norm1_gain), row(q_norm_gain), row(k_norm_gain), attn_sinks.astype(F32),
                     w_in[:, n_proj:].astype(BF16), w_branch_attn.astype(BF16),
                     w_branch_rwkv.astype(BF16), w_out.astype(BF16))
    out = _mlp(h2, row(norm2_gain), w_ff_in.astype(BF16), w_ff_out.astype(BF16))
    return out.reshape(b, t, d)


def kernel(x, norm1_gain, w_in, q_norm_gain, k_norm_gain, attn_sinks, mu_r, mu_k, mu_v, mu_w, mu_a, mu_g, decay_bias, decay_up, aaa_bias, aaa_up, gate_up, k_k, k_a, r_k, ln_x_gain, ln_x_bias, w_branch_attn, w_branch_rwkv, w_out, norm2_gain, w_ff_in, w_ff_out):
    h = x.astype(F32)
    params = (norm1_gain, w_in, q_norm_gain, k_norm_gain, attn_sinks, mu_r, mu_k, mu_v, mu_w,
              mu_a, mu_g, decay_bias, decay_up, aaa_bias, aaa_up, gate_up, k_k, k_a, r_k,
              ln_x_gain, ln_x_bias, w_branch_attn, w_branch_rwkv, w_out, norm2_gain,
              w_ff_in, w_ff_out)
    for l in range(norm1_gain.shape[0]):
        h = _layer(h, *(p[l] for p in params))
    return h.astype(x.dtype)
```

```python
import math

import jax
import jax.numpy as jnp
from jax import lax
from jax.experimental import pallas as pl
from jax.experimental.pallas import tpu as pltpu

F32 = jnp.float32
BF16 = jnp.bfloat16

HEAD_DIM = 64
ATTN_Q_HEADS = 8
ATTN_KV_HEADS = 2
ATTN_GROUP = ATTN_Q_HEADS // ATTN_KV_HEADS
WINDOW = 128
BLOCK = 128
RWKV_HEADS = 8
RWKV_HEAD_SIZE = 64
LORA_DECAY = 64
LORA_AAA = 64
LORA_GATE = 128
ATTN_Q_WIDTH = ATTN_Q_HEADS * HEAD_DIM
ATTN_KV_WIDTH = ATTN_KV_HEADS * HEAD_DIM
RWKV_WIDTH = RWKV_HEADS * RWKV_HEAD_SIZE
RWKV_IN_WIDTH = 3 * RWKV_WIDTH + LORA_DECAY + LORA_AAA + LORA_GATE
RMS_EPS = 1e-6
GN_EPS = 64e-5
L2_EPS = 1e-12

CHUNK = 64
ROW_TILE = 512
ATTN_TILE = 512
RWKV_TILE = 512
RWKV_GROUP = 256
VMEM_LIMIT = 56 * 1024 * 1024

_NT = (((1,), (1,)), ((), ()))
_TN = (((0,), (0,)), ((), ()))


def _bdot(a, b):
    return jnp.dot(a.astype(BF16), b.astype(BF16), preferred_element_type=F32)


def _rms(x, gain):
    return x * lax.rsqrt(jnp.mean(x * x, axis=-1, keepdims=True) + RMS_EPS) * gain


def _sigmoid(x):
    return 0.5 * jnp.tanh(0.5 * x) + 0.5


def _inproj_kernel(x_ref, g_ref, w_ref, q_ref, kv_ref, rw_ref):
    u = _rms(x_ref[...], g_ref[...])
    p = _bdot(u, w_ref[...])
    q_ref[...] = p[:, :ATTN_Q_WIDTH]
    kv_ref[...] = p[:, ATTN_Q_WIDTH:ATTN_Q_WIDTH + 2 * ATTN_KV_WIDTH]
    rw_ref[...] = p[:, ATTN_Q_WIDTH + 2 * ATTN_KV_WIDTH:]


def _inproj(x2, gain, w):
    n, d = x2.shape
    wid = w.shape[1]
    tm = min(ROW_TILE, n)
    return pl.pallas_call(
        _inproj_kernel,
        grid=(n // tm,),
        in_specs=[
            pl.BlockSpec((tm, d), lambda i: (i, 0)),
            pl.BlockSpec((1, d), lambda i: (0, 0)),
            pl.BlockSpec((d, wid), lambda i: (0, 0)),
        ],
        out_specs=[
            pl.BlockSpec((tm, ATTN_Q_WIDTH), lambda i: (i, 0)),
            pl.BlockSpec((tm, 2 * ATTN_KV_WIDTH), lambda i: (i, 0)),
            pl.BlockSpec((tm, RWKV_IN_WIDTH), lambda i: (i, 0)),
        ],
        out_shape=[
            jax.ShapeDtypeStruct((n, ATTN_Q_WIDTH), F32),
            jax.ShapeDtypeStruct((n, 2 * ATTN_KV_WIDTH), F32),
            jax.ShapeDtypeStruct((n, RWKV_IN_WIDTH), F32),
        ],
        compiler_params=pltpu.CompilerParams(
            dimension_semantics=("arbitrary",), vmem_limit_bytes=VMEM_LIMIT),
        name="inproj",
    )(x2, gain, w)


def _head_sumsq(x, ones_bd):
    x2 = x * x
    hi = x2.astype(BF16)
    lo = (x2 - hi.astype(F32)).astype(BF16)
    return (jnp.dot(hi, ones_bd, preferred_element_type=F32)
            + jnp.dot(lo, ones_bd, preferred_element_type=F32))


def _attn_prep_kv(kv_all, q_gain, k_gain):
    hp = 2 * HEAD_DIM
    even = lax.broadcasted_iota(jnp.int32, (1, hp), 1) < HEAD_DIM
    ri = lax.broadcasted_iota(jnp.int32, (hp, hp), 0)
    ci = lax.broadcasted_iota(jnp.int32, (hp, hp), 1)
    ones_bd = ((ri < HEAD_DIM) == (ci < HEAD_DIM)).astype(BF16)
    k = kv_all[:, :hp]
    kn = k * lax.rsqrt(_head_sumsq(k, ones_bd) * (1.0 / HEAD_DIM) + RMS_EPS) * k_gain * q_gain
    kn_sw = pltpu.roll(kn, HEAD_DIM, 1)
    kdup = [jnp.where(even, kn, kn_sw).astype(BF16), jnp.where(even, kn_sw, kn).astype(BF16)]
    v_t = kv_all[:, hp:].T.astype(BF16)
    return kdup, v_t


def _attn_prep_q(q):
    hp = 2 * HEAD_DIM
    si = lax.broadcasted_iota(jnp.int32, (16, hp), 0)
    sj = lax.broadcasted_iota(jnp.int32, (16, hp), 1)
    sel = (((si == 0) & (sj < HEAD_DIM)) | ((si == 1) & (sj >= HEAD_DIM))).astype(BF16)
    q2 = q * q
    hi = q2.astype(BF16)
    lo = (q2 - hi.astype(F32)).astype(BF16)
    inv_rms = []
    for m in range(ATTN_Q_WIDTH // hp):
        cs = slice(m * hp, (m + 1) * hp)
        ss = (lax.dot_general(sel, hi[:, cs], _NT, preferred_element_type=F32)
              + lax.dot_general(sel, lo[:, cs], _NT, preferred_element_type=F32))
        inv_rms.append(lax.rsqrt(ss * (1.0 / HEAD_DIM) + RMS_EPS))
    return q.astype(BF16), inv_rms


def _attn_scores(j, kh, qb, kdup, inv_rms, no_prev):
    hp = 2 * HEAD_DIM
    even = lax.broadcasted_iota(jnp.int32, (1, hp), 1) < HEAD_DIM
    zero_q = jnp.zeros((BLOCK, hp), BF16)
    parts, scale = [], []
    for g in range(ATTN_GROUP):
        h = kh * ATTN_GROUP + g
        col = qb[j * BLOCK:(j + 1) * BLOCK, (h // 2) * hp:(h // 2 + 1) * hp]
        parts.append(jnp.where(even, col, zero_q) if h % 2 == 0 else jnp.where(even, zero_q, col))
        scale.append(inv_rms[h // 2][h % 2:h % 2 + 1, j * BLOCK:(j + 1) * BLOCK])
    s_t = lax.dot_general(kdup[kh][j * BLOCK:(j + 2) * BLOCK], jnp.concatenate(parts, axis=0), _NT,
                          preferred_element_type=F32)
    key = lax.broadcasted_iota(jnp.int32, (BLOCK, ATTN_GROUP * BLOCK), 0)
    qry = lax.broadcasted_iota(jnp.int32, (BLOCK, ATTN_GROUP * BLOCK), 1) & (BLOCK - 1)
    use_prev = key > qry
    prev = s_t[:BLOCK]
    if no_prev is not None:
        prev = jnp.where(no_prev, -jnp.inf, prev)
    folded = jnp.where(use_prev, prev, s_t[BLOCK:]) * jnp.concatenate(scale, axis=1)
    return folded, use_prev


def _attn_softmax(kh, folded, use_prev, sink_ref):
    sink = jnp.concatenate(
        [jnp.full((1, BLOCK), sink_ref[kh * ATTN_GROUP + g], F32) for g in range(ATTN_GROUP)], axis=1)
    m = jnp.maximum(jnp.max(folded, axis=0, keepdims=True), sink)
    p = jnp.exp(folded - m)
    den = jnp.sum(p, axis=0, keepdims=True) + jnp.exp(sink - m)
    zero = jnp.zeros_like(p)
    p_t = jnp.concatenate([jnp.where(use_prev, p, zero), jnp.where(use_prev, zero, p)], axis=0)
    return p_t.astype(BF16), 1.0 / den


def _attn_values(j, kh, p_t, inv_den, v_t, o_ref):
    o_t = jnp.dot(v_t[kh * HEAD_DIM:(kh + 1) * HEAD_DIM, j * BLOCK:(j + 2) * BLOCK], p_t,
                  preferred_element_type=F32) * inv_den
    for g in range(ATTN_GROUP):
        h = kh * ATTN_GROUP + g
        o_ref[h * HEAD_DIM:(h + 1) * HEAD_DIM, j * BLOCK:(j + 1) * BLOCK] = (
            o_t[:, g * BLOCK:(g + 1) * BLOCK].astype(o_ref.dtype))


def _pair_blockdiag(x, even):
    zero = jnp.zeros_like(x)
    return jnp.concatenate([jnp.where(even, x, zero), jnp.where(even, zero, x)], axis=0)


def _pair_sum(x, even):
    s_e = jnp.sum(jnp.where(even, x, 0.0), axis=-1, keepdims=True)
    s_o = jnp.sum(jnp.where(even, 0.0, x), axis=-1, keepdims=True)
    return jnp.where(even, s_e, s_o)


def _rwkv_prep(xin, prev_row, prm):
    c = CHUNK
    hp = 2 * RWKV_HEAD_SIZE
    w = RWKV_WIDTH
    rows = xin.shape[0]
    nc = rows // c
    cols = [slice(p * hp, (p + 1) * hp) for p in range(w // hp)]

    first = lax.broadcasted_iota(jnp.int32, (rows, 1), 0) == 0
    prev = jnp.where(first, prev_row, pltpu.roll(xin, 1, 0))
    xs = xin + (prev - xin) * prm["mu"][...]
    r = xs[:, 0:w]
    k = xs[:, w:2 * w]
    v = xs[:, 2 * w:3 * w]
    w_lora = xs[:, 3 * w:3 * w + LORA_DECAY]
    a_lora = xs[:, 3 * w + LORA_DECAY:3 * w + LORA_DECAY + LORA_AAA]
    g_lora = xs[:, 3 * w + LORA_DECAY + LORA_AAA:]
    yield

    d = prm["dbias"][...] + _bdot(jnp.tanh(w_lora), prm["dup"][...])
    lw = (-math.exp(-0.5)) * _sigmoid(d)
    a_sig = _sigmoid(prm["abias"][...] + _bdot(a_lora, prm["aup"][...]))
    gate = _bdot(_sigmoid(g_lora), prm["gup"][...])
    yield

    kkr = k * prm["kk"][...]
    k2 = k * (1.0 + (a_sig - 1.0) * prm["ka"][...])
    rk = r * k2 * prm["rk"][...]
    even_t = lax.broadcasted_iota(jnp.int32, (rows, hp), 1) < RWKV_HEAD_SIZE
    den = jnp.concatenate(
        [jnp.maximum(jnp.sqrt(_pair_sum(kkr[:, cs] * kkr[:, cs], even_t)), L2_EPS) for cs in cols],
        axis=1)
    bonus = jnp.concatenate([_pair_sum(rk[:, cs], even_t) for cs in cols], axis=1) * v
    kk = kkr / den
    a_s = -kk
    b_s = kk * a_sig
    yield

    ti = lax.broadcasted_iota(jnp.int32, (c, 3 * c), 0)
    si = lax.broadcasted_iota(jnp.int32, (c, 3 * c), 1) & (c - 1)
    lower3 = (ti >= si).astype(BF16)
    lw_hi = lw.astype(BF16)
    lw_r = lw - lw_hi.astype(F32)
    lw_mid = lw_r.astype(BF16)
    lw_lo = (lw_r - lw_mid.astype(F32)).astype(BF16)
    cl = jnp.concatenate(
        [jnp.dot(lower3, jnp.concatenate([z[ci * c:(ci + 1) * c] for z in (lw_hi, lw_mid, lw_lo)], axis=0),
                 preferred_element_type=F32) for ci in range(nc)], axis=0)
    yield

    e_c = jnp.exp(cl)
    e_neg = jnp.exp(-cl)
    g_rows = [e_c[ci * c + c - 1:ci * c + c, :] for ci in range(nc)]
    g_full = jnp.concatenate([jnp.broadcast_to(g, (c, w)) for g in g_rows], axis=0)
    at = a_s * jnp.exp(cl - lw)
    rt = r * e_c
    yield
    bt = b_s * e_neg
    kt = k2 * e_neg
    ops = dict(rt=rt, g_rows=g_rows, gate=gate, bonus=bonus,
               bh=(bt * g_full).astype(BF16), khat=(kt * g_full).astype(BF16),
               at_b=at.astype(BF16), rt_b=rt.astype(BF16), bt_b=bt.astype(BF16),
               kt_b=kt.astype(BF16), v_b=v.astype(BF16))
    return ops


def _rwkv_solve(ops, state, out_rows, prm):
    c = CHUNK
    hp = 2 * RWKV_HEAD_SIZE
    w = RWKV_WIDTH
    npair = w // hp
    nc = ops["rt"].shape[0] // c
    cols = [slice(p * hp, (p + 1) * hp) for p in range(npair)]
    even = lax.broadcasted_iota(jnp.int32, (c, hp), 1) < RWKV_HEAD_SIZE
    gi = lax.broadcasted_iota(jnp.int32, (2 * c, 2 * hp), 0)
    g_t = gi & (c - 1)
    g_s = lax.broadcasted_iota(jnp.int32, (2 * c, 2 * hp), 1) & (c - 1)
    g_mask = (g_t > g_s) | ((gi >= c) & (g_t == g_s))
    eye2 = (lax.broadcasted_iota(jnp.int32, (c, hp), 0)
            == (lax.broadcasted_iota(jnp.int32, (c, hp), 1) & (c - 1)))
    pi = lax.broadcasted_iota(jnp.int32, (hp, 2 * hp), 0)
    pj = lax.broadcasted_iota(jnp.int32, (hp, 2 * hp), 1)
    same_head = (pi < RWKV_HEAD_SIZE) == ((pj & (hp - 1)) < RWKV_HEAD_SIZE)
    eye = (lax.broadcasted_iota(jnp.int32, (hp, hp), 0)
           == lax.broadcasted_iota(jnp.int32, (hp, hp), 1))
    units = [(ci, p) for ci in range(nc) for p in range(npair)]

    def col(name, ci, p):
        return ops[name][ci * c:(ci + 1) * c, cols[p]]

    def bd(x):
        return _pair_blockdiag(x, even)

    def lanes(*xs):
        return jnp.concatenate(xs, axis=1)

    def rows(*xs):
        return jnp.concatenate(xs, axis=0)

    def mm(a, b):
        return jnp.dot(a, b, preferred_element_type=F32)

    gm = [jnp.where(g_mask,
                    lax.dot_general(rows(col("at_b", ci, p), col("rt_b", ci, p)),
                                    rows(bd(col("bt_b", ci, p)), bd(col("kt_b", ci, p))), _NT,
                                    preferred_element_type=F32), 0.0)
          for ci, p in units]
    yield
    a_pow = [g[:c, :hp].astype(BF16) for g in gm]
    t_inv = [jnp.where(eye2, 1.0, g[:c, :hp]) for g in gm]
    a_pow = [mm(a, bd(a)).astype(BF16) for a in a_pow]
    yield
    for _ in range(int(math.log2(c)) - 2):
        res = [mm(a, lanes(bd(t.astype(BF16)), bd(a))) for a, t in zip(a_pow, t_inv)]
        t_inv = [t + z[:, :hp] for t, z in zip(t_inv, res)]
        a_pow = [z[:, hp:].astype(BF16) for z in res]
        yield
    t_inv = [t + mm(a, bd(t.astype(BF16))) for a, t in zip(a_pow, t_inv)]
    yield
    wx = [mm(t.astype(BF16), lanes(bd(col("at_b", ci, p)), bd(g[:c, hp:].astype(BF16))))
          for (ci, p), t, g in zip(units, t_inv, gm)]
    yield
    v_st = [col("v_b", ci, p) for ci, p in units]
    w_b = [z[:, :hp].astype(BF16) for z in wx]
    u0_b = [mm(z[:, hp:].astype(BF16), bd(v)).astype(BF16) for z, v in zip(wx, v_st)]
    yield
    ab_y = [mm(g[c:, :].astype(BF16),
               rows(lanes(bd(wb), bd(ub)), lanes(jnp.zeros((hp, hp), BF16), bd(v))))
            for g, wb, ub, v in zip(gm, w_b, u0_b, v_st)]
    yield
    ab_h = [jnp.where(same_head,
                      lax.dot_general(rows(col("bh", ci, p), col("khat", ci, p)),
                                      rows(lanes(wb, ub), lanes(jnp.zeros((c, hp), BF16), v)), _TN,
                                      preferred_element_type=F32), 0.0)
            for (ci, p), wb, ub, v in zip(units, w_b, u0_b, v_st)]
    yield

    for ci in range(nc):
        y_cols = []
        for p in range(npair):
            idx = ci * npair + p
            qmat = col("rt", ci, p) + ab_y[idx][:, :hp]
            g_diag = jnp.where(eye, jnp.broadcast_to(ops["g_rows"][ci][:, cols[p]], (hp, hp)), 0.0)
            mmat = g_diag + ab_h[idx][:, :hp]
            ys = mm(rows(qmat, mmat).astype(BF16), state[p].astype(BF16))
            y = ys[:c] + ab_y[idx][:, hp:]
            state[p] = ys[c:] + ab_h[idx][:, hp:]
            mean = _pair_sum(y, even) * (1.0 / RWKV_HEAD_SIZE)
            yc = y - mean
            var = _pair_sum(yc * yc, even) * (1.0 / RWKV_HEAD_SIZE)
            y_cols.append(yc * lax.rsqrt(var + GN_EPS))
        yn = jnp.concatenate(y_cols, axis=1)
        rs = slice(ci * c, (ci + 1) * c)
        out_rows(ci, (yn * prm["lng"][...] + prm["lnb"][...] + ops["bonus"][rs]) * ops["gate"][rs])
        yield


def _drain(gen):
    try:
        while True:
            next(gen)
    except StopIteration as stop:
        return stop.value


def _rwkv_kernel(xin_ref, mu_ref, dbias_ref, dup_ref, abias_ref, aup_ref, gup_ref,
                 kk_ref, ka_ref, rk_ref, lng_ref, lnb_ref, o_ref, prev_ref, state_ref):
    hp = 2 * RWKV_HEAD_SIZE
    tb = xin_ref.shape[1]
    gr = min(RWKV_GROUP, tb)
    ngroup = tb // gr
    prm = dict(mu=mu_ref, dbias=dbias_ref, dup=dup_ref, abias=abias_ref, aup=aup_ref, gup=gup_ref,
               kk=kk_ref, ka=ka_ref, rk=rk_ref, lng=lng_ref, lnb=lnb_ref)

    @pl.when(pl.program_id(1) == 0)
    def _():
        prev_ref[...] = jnp.zeros_like(prev_ref)
        state_ref[...] = jnp.zeros_like(state_ref)

    def prep(g):
        prev_row = prev_ref[...] if g == 0 else xin_ref[0, g * gr - 1:g * gr, :]
        return _rwkv_prep(xin_ref[0, g * gr:(g + 1) * gr, :], prev_row, prm)

    state = [state_ref[:, p * hp:(p + 1) * hp] for p in range(RWKV_WIDTH // hp)]
    ops = _drain(prep(0))
    for g in range(ngroup):
        def out_rows(ci, value, g=g):
            r0 = g * gr + ci * CHUNK
            o_ref[0, r0:r0 + CHUNK, :] = value.astype(o_ref.dtype)

        nxt = prep(g + 1) if g + 1 < ngroup else None
        nxt_ops = None
        for _ in _rwkv_solve(ops, state, out_rows, prm):
            if nxt is not None:
                try:
                    next(nxt)
                except StopIteration as stop:
                    nxt_ops, nxt = stop.value, None
        if nxt is not None:
            nxt_ops = _drain(nxt)
        ops = nxt_ops
    prev_ref[...] = xin_ref[0, tb - 1:tb, :]
    for p in range(RWKV_WIDTH // hp):
        state_ref[:, p * hp:(p + 1) * hp] = state[p]


def _rwkv(rw, mu, decay_bias, decay_up, aaa_bias, aaa_up, gate_up, k_k, k_a, r_k, ln_g, ln_b):
    b, t, _ = rw.shape
    tb = min(RWKV_TILE, t)
    w = RWKV_WIDTH

    def full(shape):
        return pl.BlockSpec(shape, lambda bi, ti: (0, 0))

    return pl.pallas_call(
        _rwkv_kernel,
        grid=(b, t // tb),
        in_specs=[
            pl.BlockSpec((1, tb, RWKV_IN_WIDTH), lambda bi, ti: (bi, ti, 0)),
            full((1, RWKV_IN_WIDTH)),
            full((1, w)), full((LORA_DECAY, w)),
            full((1, w)), full((LORA_AAA, w)),
            full((LORA_GATE, w)),
            full((1, w)), full((1, w)), full((1, w)), full((1, w)), full((1, w)),
        ],
        out_specs=pl.BlockSpec((1, tb, w), lambda bi, ti: (bi, ti, 0)),
        out_shape=jax.ShapeDtypeStruct((b, t, w), BF16),
        scratch_shapes=[
            pltpu.VMEM((1, RWKV_IN_WIDTH), F32),
            pltpu.VMEM((2 * RWKV_HEAD_SIZE, w), F32),
        ],
        compiler_params=pltpu.CompilerParams(
            dimension_semantics=("arbitrary", "arbitrary"), vmem_limit_bytes=VMEM_LIMIT),
        name="rwkv",
    )(rw, mu, decay_bias, decay_up, aaa_bias, aaa_up, gate_up, k_k, k_a, r_k, ln_g, ln_b)


def _attn_merge_kernel(x_ref, q_ref, kv_ref, kvp_ref, rwkv_ref, g_ref, qg_ref, kg_ref, sink_ref,
                       wg_ref, wba_ref, wbr_ref, wo_ref, h_ref, attn_ref):
    x = x_ref[...]
    d = x.shape[1]
    nblk = q_ref.shape[1] // BLOCK
    units = [(j, kh) for j in range(nblk) for kh in range(ATTN_KV_HEADS)]
    n_u = len(units)
    gw = wg_ref.shape[1] // n_u
    gate_parts = []

    def gate_slice():
        c0 = len(gate_parts) * gw
        gate_parts.append(jnp.dot(ub, wg_ref[:, c0:c0 + gw], preferred_element_type=F32))

    y_rwkv = jnp.dot(rwkv_ref[...], wbr_ref[...], preferred_element_type=F32)
    ub = _rms(x, g_ref[...]).astype(BF16)
    gate_slice()
    kv_all = jnp.concatenate([kvp_ref[0], kv_ref[0]], axis=0)
    kdup, v_t = _attn_prep_kv(kv_all, qg_ref[...], kg_ref[...])
    gate_slice()
    qb, inv_rms = _attn_prep_q(q_ref[0])
    first_tile = pl.program_id(1) == 0
    scores = [_attn_scores(j, kh, qb, kdup, inv_rms, first_tile if j == 0 else None)
              for j, kh in units]
    soft = []
    for idx, (_, kh) in enumerate(units):
        if len(gate_parts) < n_u and idx % 4 != 1:
            gate_slice()
        soft.append(_attn_softmax(kh, *scores[idx], sink_ref))
    while len(gate_parts) < n_u:
        gate_slice()
    gates = jnp.concatenate(gate_parts, axis=1)
    for unit, sm in zip(units, soft):
        _attn_values(*unit, *sm, v_t, attn_ref)
    y_attn = lax.dot_general(attn_ref[...], wba_ref[...], _TN, preferred_element_type=F32)
    mixed = _sigmoid(gates[:, :d]) * y_attn + _sigmoid(gates[:, d:]) * y_rwkv
    h_ref[...] = x + _bdot(mixed, wo_ref[...])


def _attn_merge(x2, q, kv, rwkv, gain, q_gain, k_gain, sinks, w_gate, w_ba, w_br, w_out):
    n, d = x2.shape
    b, t, _ = q.shape
    tq = min(ATTN_TILE, t)
    nblk = tq // BLOCK
    nt = t // tq
    q_gain = jnp.tile(q_gain * (HEAD_DIM ** -0.5), (1, ATTN_KV_HEADS))
    k_gain = jnp.tile(k_gain, (1, ATTN_KV_HEADS))

    def full(a):
        return pl.BlockSpec(a.shape, lambda bi, i: (0, 0))

    def rows(width):
        return pl.BlockSpec((tq, width), lambda bi, i: (bi * nt + i, 0))

    return pl.pallas_call(
        _attn_merge_kernel,
        grid=(b, nt),
        in_specs=[
            rows(d),
            pl.BlockSpec((1, tq, ATTN_Q_WIDTH), lambda bi, i: (bi, i, 0)),
            pl.BlockSpec((1, tq, 2 * ATTN_KV_WIDTH), lambda bi, i: (bi, i, 0)),
            pl.BlockSpec((1, BLOCK, 2 * ATTN_KV_WIDTH),
                         lambda bi, i: (bi, jnp.maximum(i * nblk - 1, 0), 0)),
            rows(rwkv.shape[1]),
            full(gain), full(q_gain), full(k_gain),
            pl.BlockSpec(memory_space=pltpu.SMEM),
            full(w_gate), full(w_ba), full(w_br), full(w_out),
        ],
        out_specs=rows(d),
        out_shape=jax.ShapeDtypeStruct((n, d), F32),
        scratch_shapes=[pltpu.VMEM((ATTN_Q_WIDTH, tq), BF16)],
        compiler_params=pltpu.CompilerParams(
            dimension_semantics=("arbitrary", "arbitrary"), vmem_limit_bytes=VMEM_LIMIT),
        name="attn_merge",
    )(x2, q, kv, kv, rwkv, gain, q_gain, k_gain, sinks, w_gate, w_ba, w_br, w_out)


def _mlp_kernel(h_ref, g_ref, w1_ref, w2_ref, o_ref):
    h = h_ref[...]
    hidden = jnp.square(jnp.maximum(_bdot(_rms(h, g_ref[...]), w1_ref[...]), 0.0))
    o_ref[...] = h + _bdot(hidden, w2_ref[...])


def _mlp(h2, gain, w1, w2):
    n, d = h2.shape
    tm = min(ROW_TILE // 2, n)

    def full(a):
        return pl.BlockSpec(a.shape, lambda i: (0, 0))

    return pl.pallas_call(
        _mlp_kernel,
        grid=(n // tm,),
        in_specs=[pl.BlockSpec((tm, d), lambda i: (i, 0)), full(gain), full(w1), full(w2)],
        out_specs=pl.BlockSpec((tm, d), lambda i: (i, 0)),
        out_shape=jax.ShapeDtypeStruct((n, d), F32),
        compiler_params=pltpu.CompilerParams(
            dimension_semantics=("arbitrary",), vmem_limit_bytes=VMEM_LIMIT),
        name="mlp",
    )(h2, gain, w1, w2)


def _layer(h, norm1_gain, w_in, q_norm_gain, k_norm_gain, attn_sinks,
           mu_r, mu_k, mu_v, mu_w, mu_a, mu_g, decay_bias, decay_up, aaa_bias, aaa_up,
           gate_up, k_k, k_a, r_k, ln_x_gain, ln_x_bias, w_branch_attn, w_branch_rwkv,
           w_out, norm2_gain, w_ff_in, w_ff_out):
    b, t, d = h.shape
    x2 = h.reshape(b * t, d)
    row = lambda a: a.reshape(1, -1).astype(F32)
    n_proj = ATTN_Q_WIDTH + 2 * ATTN_KV_WIDTH + RWKV_IN_WIDTH

    q, kv, rw = _inproj(x2, row(norm1_gain), w_in[:, :n_proj].astype(BF16))
    mu = jnp.concatenate([mu_r, mu_k, mu_v, mu_w, mu_a, mu_g]).reshape(1, -1).astype(F32)
    rwkv = _rwkv(rw.reshape(b, t, -1), mu, row(decay_bias), decay_up.astype(BF16),
                 row(aaa_bias), aaa_up.astype(BF16), gate_up.astype(BF16),
                 row(k_k), row(k_a), row(r_k), row(ln_x_gain), row(ln_x_bias))
    h2 = _attn_merge(x2, q.reshape(b, t, -1), kv.reshape(b, t, -1), rwkv.reshape(b * t, -1),
                     row(norm1_gain), row(q_norm_gain), row(k_norm_gain), attn_sinks.astype(F32),
                     w_in[:, n_proj:].astype(BF16), w_branch_attn.astype(BF16),
                     w_branch_rwkv.astype(BF16), w_out.astype(BF16))
    out = _mlp(h2, row(norm2_gain), w_ff_in.astype(BF16), w_ff_out.astype(BF16))
    return out.reshape(b, t, d)


def kernel(x, norm1_gain, w_in, q_norm_gain, k_norm_gain, attn_sinks, mu_r, mu_k, mu_v, mu_w, mu_a, mu_g, decay_bias, decay_up, aaa_bias, aaa_up, gate_up, k_k, k_a, r_k, ln_x_gain, ln_x_bias, w_branch_attn, w_branch_rwkv, w_out, norm2_gain, w_ff_in, w_ff_out):
    h = x.astype(F32)
    params = (norm1_gain, w_in, q_norm_gain, k_norm_gain, attn_sinks, mu_r, mu_k, mu_v, mu_w,
              mu_a, mu_g, decay_bias, decay_up, aaa_bias, aaa_up, gate_up, k_k, k_a, r_k,
              ln_x_gain, ln_x_bias, w_branch_attn, w_branch_rwkv, w_out, norm2_gain,
              w_ff_in, w_ff_out)
    for l in range(norm1_gain.shape[0]):
        h = _layer(h, *(p[l] for p in params))
    return h.astype(x.dtype)
```

```python
import functools
import math

import jax
import jax.numpy as jnp
from jax import lax
from jax.experimental import pallas as pl
from jax.experimental.pallas import tpu as pltpu

F32 = jnp.float32
BF16 = jnp.bfloat16

HEAD_DIM = 64
ATTN_Q_HEADS = 8
ATTN_KV_HEADS = 2
ATTN_GROUP = ATTN_Q_HEADS // ATTN_KV_HEADS
WINDOW = 128
BLOCK = 128
RWKV_HEADS = 8
RWKV_HEAD_SIZE = 64
LORA_DECAY = 64
LORA_AAA = 64
LORA_GATE = 128
ATTN_Q_WIDTH = ATTN_Q_HEADS * HEAD_DIM
ATTN_KV_WIDTH = ATTN_KV_HEADS * HEAD_DIM
RWKV_WIDTH = RWKV_HEADS * RWKV_HEAD_SIZE
RWKV_IN_WIDTH = 3 * RWKV_WIDTH + LORA_DECAY + LORA_AAA + LORA_GATE
RMS_EPS = 1e-6
GN_EPS = 64e-5
L2_EPS = 1e-12

CHUNK = 64
ROW_TILE = 512
ATTN_TILE = 512
RWKV_TILE = 512
RWKV_GROUP = 256
VMEM_LIMIT = 56 * 1024 * 1024

_NT = (((1,), (1,)), ((), ()))
_TN = (((0,), (0,)), ((), ()))


def _bdot(a, b):
    return jnp.dot(a.astype(BF16), b.astype(BF16), preferred_element_type=F32)


def _rms(x, gain):
    return x * lax.rsqrt(jnp.mean(x * x, axis=-1, keepdims=True) + RMS_EPS) * gain


def _sigmoid(x):
    return 0.5 * jnp.tanh(0.5 * x) + 0.5


def _inproj_kernel(x_ref, g_ref, w_ref, mu_ref, q_ref, kv_ref, rw_ref, last_ref, *, tiles_per_seq):
    u = _rms(x_ref[...], g_ref[...])
    p = _bdot(u, w_ref[...])
    q_ref[...] = p[:, :ATTN_Q_WIDTH]
    kv_ref[...] = p[:, ATTN_Q_WIDTH:ATTN_Q_WIDTH + 2 * ATTN_KV_WIDTH]
    rw = p[:, ATTN_Q_WIDTH + 2 * ATTN_KV_WIDTH:]
    tm = rw.shape[0]
    seq_start = pl.program_id(0) % tiles_per_seq == 0
    before = jnp.where(seq_start, 0.0, last_ref[...])
    first = lax.broadcasted_iota(jnp.int32, (tm, 1), 0) == 0
    prev = jnp.where(first, before, pltpu.roll(rw, 1, 0))
    last_ref[...] = rw[tm - 1:tm, :]
    rw_ref[...] = rw + (prev - rw) * mu_ref[...]


def _inproj(x2, gain, w, mu, seq_len):
    n, d = x2.shape
    wid = w.shape[1]
    tm = min(ROW_TILE, seq_len)
    return pl.pallas_call(
        functools.partial(_inproj_kernel, tiles_per_seq=seq_len // tm),
        grid=(n // tm,),
        in_specs=[
            pl.BlockSpec((tm, d), lambda i: (i, 0)),
            pl.BlockSpec((1, d), lambda i: (0, 0)),
            pl.BlockSpec((d, wid), lambda i: (0, 0)),
            pl.BlockSpec((1, RWKV_IN_WIDTH), lambda i: (0, 0)),
        ],
        out_specs=[
            pl.BlockSpec((tm, ATTN_Q_WIDTH), lambda i: (i, 0)),
            pl.BlockSpec((tm, 2 * ATTN_KV_WIDTH), lambda i: (i, 0)),
            pl.BlockSpec((tm, RWKV_IN_WIDTH), lambda i: (i, 0)),
        ],
        out_shape=[
            jax.ShapeDtypeStruct((n, ATTN_Q_WIDTH), F32),
            jax.ShapeDtypeStruct((n, 2 * ATTN_KV_WIDTH), F32),
            jax.ShapeDtypeStruct((n, RWKV_IN_WIDTH), F32),
        ],
        scratch_shapes=[pltpu.VMEM((1, RWKV_IN_WIDTH), F32)],
        compiler_params=pltpu.CompilerParams(
            dimension_semantics=("arbitrary",), vmem_limit_bytes=VMEM_LIMIT),
        name="inproj",
    )(x2, gain, w, mu)


def _head_sumsq(x, ones_bd):
    x2 = x * x
    hi = x2.astype(BF16)
    lo = (x2 - hi.astype(F32)).astype(BF16)
    return (jnp.dot(hi, ones_bd, preferred_element_type=F32)
            + jnp.dot(lo, ones_bd, preferred_element_type=F32))


def _attn_prep_kv(kv_all, q_gain, k_gain):
    hp = 2 * HEAD_DIM
    even = lax.broadcasted_iota(jnp.int32, (1, hp), 1) < HEAD_DIM
    ri = lax.broadcasted_iota(jnp.int32, (hp, hp), 0)
    ci = lax.broadcasted_iota(jnp.int32, (hp, hp), 1)
    ones_bd = ((ri < HEAD_DIM) == (ci < HEAD_DIM)).astype(BF16)
    k = kv_all[:, :hp]
    kn = k * lax.rsqrt(_head_sumsq(k, ones_bd) * (1.0 / HEAD_DIM) + RMS_EPS) * k_gain * q_gain
    kn_sw = pltpu.roll(kn, HEAD_DIM, 1)
    kdup = [jnp.where(even, kn, kn_sw).astype(BF16), jnp.where(even, kn_sw, kn).astype(BF16)]
    v_t = kv_all[:, hp:].T.astype(BF16)
    return kdup, v_t


def _attn_prep_q(q):
    hp = 2 * HEAD_DIM
    si = lax.broadcasted_iota(jnp.int32, (16, hp), 0)
    sj = lax.broadcasted_iota(jnp.int32, (16, hp), 1)
    sel = (((si == 0) & (sj < HEAD_DIM)) | ((si == 1) & (sj >= HEAD_DIM))).astype(BF16)
    q2 = q * q
    hi = q2.astype(BF16)
    lo = (q2 - hi.astype(F32)).astype(BF16)
    inv_rms = []
    for m in range(ATTN_Q_WIDTH // hp):
        cs = slice(m * hp, (m + 1) * hp)
        ss = (lax.dot_general(sel, hi[:, cs], _NT, preferred_element_type=F32)
              + lax.dot_general(sel, lo[:, cs], _NT, preferred_element_type=F32))
        inv_rms.append(lax.rsqrt(ss * (1.0 / HEAD_DIM) + RMS_EPS))
    return q.astype(BF16), inv_rms


def _attn_scores(j, kh, qb, kdup, inv_rms, no_prev):
    hp = 2 * HEAD_DIM
    even = lax.broadcasted_iota(jnp.int32, (1, hp), 1) < HEAD_DIM
    zero_q = jnp.zeros((BLOCK, hp), BF16)
    parts, scale = [], []
    for g in range(ATTN_GROUP):
        h = kh * ATTN_GROUP + g
        col = qb[j * BLOCK:(j + 1) * BLOCK, (h // 2) * hp:(h // 2 + 1) * hp]
        parts.append(jnp.where(even, col, zero_q) if h % 2 == 0 else jnp.where(even, zero_q, col))
        scale.append(inv_rms[h // 2][h % 2:h % 2 + 1, j * BLOCK:(j + 1) * BLOCK])
    s_t = lax.dot_general(kdup[kh][j * BLOCK:(j + 2) * BLOCK], jnp.concatenate(parts, axis=0), _NT,
                          preferred_element_type=F32)
    key = lax.broadcasted_iota(jnp.int32, (BLOCK, ATTN_GROUP * BLOCK), 0)
    qry = lax.broadcasted_iota(jnp.int32, (BLOCK, ATTN_GROUP * BLOCK), 1) & (BLOCK - 1)
    use_prev = key > qry
    prev = s_t[:BLOCK]
    if no_prev is not None:
        prev = jnp.where(no_prev, -jnp.inf, prev)
    folded = jnp.where(use_prev, prev, s_t[BLOCK:]) * jnp.concatenate(scale, axis=1)
    return folded, use_prev


def _attn_softmax(kh, folded, use_prev, sink_ref):
    sink = jnp.concatenate(
        [jnp.full((1, BLOCK), sink_ref[kh * ATTN_GROUP + g], F32) for g in range(ATTN_GROUP)], axis=1)
    m = jnp.maximum(jnp.max(folded, axis=0, keepdims=True), sink)
    p = jnp.exp(folded - m)
    den = jnp.sum(p, axis=0, keepdims=True) + jnp.exp(sink - m)
    zero = jnp.zeros_like(p)
    p_t = jnp.concatenate([jnp.where(use_prev, p, zero), jnp.where(use_prev, zero, p)], axis=0)
    return p_t.astype(BF16), 1.0 / den


def _attn_values(j, kh, p_t, inv_den, v_t, o_ref):
    o_t = jnp.dot(v_t[kh * HEAD_DIM:(kh + 1) * HEAD_DIM, j * BLOCK:(j + 2) * BLOCK], p_t,
                  preferred_element_type=F32) * inv_den
    for g in range(ATTN_GROUP):
        h = kh * ATTN_GROUP + g
        o_ref[h * HEAD_DIM:(h + 1) * HEAD_DIM, j * BLOCK:(j + 1) * BLOCK] = (
            o_t[:, g * BLOCK:(g + 1) * BLOCK].astype(o_ref.dtype))


def _pair_blockdiag(x, even):
    zero = jnp.zeros_like(x)
    return jnp.concatenate([jnp.where(even, x, zero), jnp.where(even, zero, x)], axis=0)


def _pair_sum(x, even):
    s_e = jnp.sum(jnp.where(even, x, 0.0), axis=-1, keepdims=True)
    s_o = jnp.sum(jnp.where(even, 0.0, x), axis=-1, keepdims=True)
    return jnp.where(even, s_e, s_o)


def _rwkv_prep(xs, prm):
    c = CHUNK
    hp = 2 * RWKV_HEAD_SIZE
    w = RWKV_WIDTH
    rows = xs.shape[0]
    nc = rows // c
    cols = [slice(p * hp, (p + 1) * hp) for p in range(w // hp)]
    r = xs[:, 0:w]
    k = xs[:, w:2 * w]
    v = xs[:, 2 * w:3 * w]
    w_lora = xs[:, 3 * w:3 * w + LORA_DECAY]
    a_lora = xs[:, 3 * w + LORA_DECAY:3 * w + LORA_DECAY + LORA_AAA]
    g_lora = xs[:, 3 * w + LORA_DECAY + LORA_AAA:]

    d = prm["dbias"][...] + _bdot(jnp.tanh(w_lora), prm["dup"][...])
    lw = (-math.exp(-0.5)) * _sigmoid(d)
    a_sig = _sigmoid(prm["abias"][...] + _bdot(a_lora, prm["aup"][...]))
    gate = _bdot(_sigmoid(g_lora), prm["gup"][...])
    yield

    kkr = k * prm["kk"][...]
    k2 = k * (1.0 + (a_sig - 1.0) * prm["ka"][...])
    rk = r * k2 * prm["rk"][...]
    even_t = lax.broadcasted_iota(jnp.int32, (rows, hp), 1) < RWKV_HEAD_SIZE
    den = jnp.concatenate(
        [jnp.maximum(jnp.sqrt(_pair_sum(kkr[:, cs] * kkr[:, cs], even_t)), L2_EPS) for cs in cols],
        axis=1)
    bonus = jnp.concatenate([_pair_sum(rk[:, cs], even_t) for cs in cols], axis=1) * v
    kk = kkr / den
    a_s = -kk
    b_s = kk * a_sig
    yield

    ti = lax.broadcasted_iota(jnp.int32, (c, 3 * c), 0)
    si = lax.broadcasted_iota(jnp.int32, (c, 3 * c), 1) & (c - 1)
    lower3 = (ti >= si).astype(BF16)
    lw_hi = lw.astype(BF16)
    lw_r = lw - lw_hi.astype(F32)
    lw_mid = lw_r.astype(BF16)
    lw_lo = (lw_r - lw_mid.astype(F32)).astype(BF16)
    cl = jnp.concatenate(
        [jnp.dot(lower3, jnp.concatenate([z[ci * c:(ci + 1) * c] for z in (lw_hi, lw_mid, lw_lo)], axis=0),
                 preferred_element_type=F32) for ci in range(nc)], axis=0)
    yield

    e_c = jnp.exp(cl)
    e_neg = jnp.exp(-cl)
    g_rows = [e_c[ci * c + c - 1:ci * c + c, :] for ci in range(nc)]
    g_full = jnp.concatenate([jnp.broadcast_to(g, (c, w)) for g in g_rows], axis=0)
    at = a_s * jnp.exp(cl - lw)
    rt = r * e_c
    yield
    bt = b_s * e_neg
    kt = k2 * e_neg
    ops = dict(rt=rt, g_rows=g_rows, gate=gate, bonus=bonus,
               bh=(bt * g_full).astype(BF16), khat=(kt * g_full).astype(BF16),
               at_b=at.astype(BF16), rt_b=rt.astype(BF16), bt_b=bt.astype(BF16),
               kt_b=kt.astype(BF16), v_b=v.astype(BF16))
    return ops


def _rwkv_solve(ops, state, out_rows, prm):
    c = CHUNK
    hp = 2 * RWKV_HEAD_SIZE
    w = RWKV_WIDTH
    npair = w // hp
    nc = ops["rt"].shape[0] // c
    cols = [slice(p * hp, (p + 1) * hp) for p in range(npair)]
    even = lax.broadcasted_iota(jnp.int32, (c, hp), 1) < RWKV_HEAD_SIZE
    gi = lax.broadcasted_iota(jnp.int32, (2 * c, 2 * hp), 0)
    g_t = gi & (c - 1)
    g_s = lax.broadcasted_iota(jnp.int32, (2 * c, 2 * hp), 1) & (c - 1)
    g_mask = (g_t > g_s) | ((gi >= c) & (g_t == g_s))
    eye2 = (lax.broadcasted_iota(jnp.int32, (c, hp), 0)
            == (lax.broadcasted_iota(jnp.int32, (c, hp), 1) & (c - 1)))
    pi = lax.broadcasted_iota(jnp.int32, (hp, 2 * hp), 0)
    pj = lax.broadcasted_iota(jnp.int32, (hp, 2 * hp), 1)
    same_head = (pi < RWKV_HEAD_SIZE) == ((pj & (hp - 1)) < RWKV_HEAD_SIZE)
    eye = (lax.broadcasted_iota(jnp.int32, (hp, hp), 0)
           == lax.broadcasted_iota(jnp.int32, (hp, hp), 1))
    units = [(ci, p) for ci in range(nc) for p in range(npair)]

    def col(name, ci, p):
        return ops[name][ci * c:(ci + 1) * c, cols[p]]

    def bd(x):
        return _pair_blockdiag(x, even)

    def lanes(*xs):
        return jnp.concatenate(xs, axis=1)

    def rows(*xs):
        return jnp.concatenate(xs, axis=0)

    def mm(a, b):
        return jnp.dot(a, b, preferred_element_type=F32)

    gm = [jnp.where(g_mask,
                    lax.dot_general(rows(col("at_b", ci, p), col("rt_b", ci, p)),
                                    rows(bd(col("bt_b", ci, p)), bd(col("kt_b", ci, p))), _NT,
                                    preferred_element_type=F32), 0.0)
          for ci, p in units]
    yield
    a_pow = [g[:c, :hp].astype(BF16) for g in gm]
    t_inv = [jnp.where(eye2, 1.0, g[:c, :hp]) for g in gm]
    a_pow = [mm(a, bd(a)).astype(BF16) for a in a_pow]
    yield
    for _ in range(int(math.log2(c)) - 2):
        res = [mm(a, lanes(bd(t.astype(BF16)), bd(a))) for a, t in zip(a_pow, t_inv)]
        t_inv = [t + z[:, :hp] for t, z in zip(t_inv, res)]
        a_pow = [z[:, hp:].astype(BF16) for z in res]
        yield
    t_inv = [t + mm(a, bd(t.astype(BF16))) for a, t in zip(a_pow, t_inv)]
    yield
    wx = [mm(t.astype(BF16), lanes(bd(col("at_b", ci, p)), bd(g[:c, hp:].astype(BF16))))
          for (ci, p), t, g in zip(units, t_inv, gm)]
    yield
    v_st = [col("v_b", ci, p) for ci, p in units]
    w_b = [z[:, :hp].astype(BF16) for z in wx]
    u0_b = [mm(z[:, hp:].astype(BF16), bd(v)).astype(BF16) for z, v in zip(wx, v_st)]
    yield
    ab_y = [mm(g[c:, :].astype(BF16),
               rows(lanes(bd(wb), bd(ub)), lanes(jnp.zeros((hp, hp), BF16), bd(v))))
            for g, wb, ub, v in zip(gm, w_b, u0_b, v_st)]
    yield
    ab_h = [jnp.where(same_head,
                      lax.dot_general(rows(col("bh", ci, p), col("khat", ci, p)),
                                      rows(lanes(wb, ub), lanes(jnp.zeros((c, hp), BF16), v)), _TN,
                                      preferred_element_type=F32), 0.0)
            for (ci, p), wb, ub, v in zip(units, w_b, u0_b, v_st)]
    yield

    for ci in range(nc):
        y_cols = []
        for p in range(npair):
            idx = ci * npair + p
            qmat = col("rt", ci, p) + ab_y[idx][:, :hp]
            g_diag = jnp.where(eye, jnp.broadcast_to(ops["g_rows"][ci][:, cols[p]], (hp, hp)), 0.0)
            mmat = g_diag + ab_h[idx][:, :hp]
            ys = mm(rows(qmat, mmat).astype(BF16), state[p].astype(BF16))
            y = ys[:c] + ab_y[idx][:, hp:]
            state[p] = ys[c:] + ab_h[idx][:, hp:]
            mean = _pair_sum(y, even) * (1.0 / RWKV_HEAD_SIZE)
            yc = y - mean
            var = _pair_sum(yc * yc, even) * (1.0 / RWKV_HEAD_SIZE)
            y_cols.append(yc * lax.rsqrt(var + GN_EPS))
        yn = jnp.concatenate(y_cols, axis=1)
        rs = slice(ci * c, (ci + 1) * c)
        out_rows(ci, (yn * prm["lng"][...] + prm["lnb"][...] + ops["bonus"][rs]) * ops["gate"][rs])
        yield


def _drain(gen):
    try:
        while True:
            next(gen)
    except StopIteration as stop:
        return stop.value


def _rwkv_kernel(xs_ref, dbias_ref, dup_ref, abias_ref, aup_ref, gup_ref,
                 kk_ref, ka_ref, rk_ref, lng_ref, lnb_ref, o_ref, state_ref):
    hp = 2 * RWKV_HEAD_SIZE
    tb = xs_ref.shape[1]
    gr = min(RWKV_GROUP, tb)
    ngroup = tb // gr
    prm = dict(dbias=dbias_ref, dup=dup_ref, abias=abias_ref, aup=aup_ref, gup=gup_ref,
               kk=kk_ref, ka=ka_ref, rk=rk_ref, lng=lng_ref, lnb=lnb_ref)

    @pl.when(pl.program_id(1) == 0)
    def _():
        state_ref[...] = jnp.zeros_like(state_ref)

    def prep(g):
        return _rwkv_prep(xs_ref[0, g * gr:(g + 1) * gr, :], prm)

    state = [state_ref[:, p * hp:(p + 1) * hp] for p in range(RWKV_WIDTH // hp)]
    ops = _drain(prep(0))
    for g in range(ngroup):
        def out_rows(ci, value, g=g):
            r0 = g * gr + ci * CHUNK
            o_ref[0, r0:r0 + CHUNK, :] = value.astype(o_ref.dtype)

        nxt = prep(g + 1) if g + 1 < ngroup else None
        nxt_ops = None
        for _ in _rwkv_solve(ops, state, out_rows, prm):
            if nxt is not None:
                try:
                    next(nxt)
                except StopIteration as stop:
                    nxt_ops, nxt = stop.value, None
        if nxt is not None:
            nxt_ops = _drain(nxt)
        ops = nxt_ops
    for p in range(RWKV_WIDTH // hp):
        state_ref[:, p * hp:(p + 1) * hp] = state[p]


def _rwkv(rw, decay_bias, decay_up, aaa_bias, aaa_up, gate_up, k_k, k_a, r_k, ln_g, ln_b):
    b, t, _ = rw.shape
    tb = min(RWKV_TILE, t)
    w = RWKV_WIDTH

    def full(shape):
        return pl.BlockSpec(shape, lambda bi, ti: (0, 0))

    return pl.pallas_call(
        _rwkv_kernel,
        grid=(b, t // tb),
        in_specs=[
            pl.BlockSpec((1, tb, RWKV_IN_WIDTH), lambda bi, ti: (bi, ti, 0)),
            full((1, w)), full((LORA_DECAY, w)),
            full((1, w)), full((LORA_AAA, w)),
            full((LORA_GATE, w)),
            full((1, w)), full((1, w)), full((1, w)), full((1, w)), full((1, w)),
        ],
        out_specs=pl.BlockSpec((1, tb, w), lambda bi, ti: (bi, ti, 0)),
        out_shape=jax.ShapeDtypeStruct((b, t, w), BF16),
        scratch_shapes=[pltpu.VMEM((2 * RWKV_HEAD_SIZE, w), F32)],
        compiler_params=pltpu.CompilerParams(
            dimension_semantics=("arbitrary", "arbitrary"), vmem_limit_bytes=VMEM_LIMIT),
        name="rwkv",
    )(rw, decay_bias, decay_up, aaa_bias, aaa_up, gate_up, k_k, k_a, r_k, ln_g, ln_b)


def _attn_merge_kernel(x_ref, q_ref, kv_ref, kvp_ref, rwkv_ref, g_ref, qg_ref, kg_ref, sink_ref,
                       wg_ref, wba_ref, wbr_ref, wo_ref, h_ref, attn_ref):
    x = x_ref[...]
    d = x.shape[1]
    nblk = q_ref.shape[1] // BLOCK
    units = [(j, kh) for j in range(nblk) for kh in range(ATTN_KV_HEADS)]
    n_u = len(units)
    gw = wg_ref.shape[1] // n_u
    gate_parts = []

    def gate_slice():
        c0 = len(gate_parts) * gw
        gate_parts.append(jnp.dot(ub, wg_ref[:, c0:c0 + gw], preferred_element_type=F32))

    y_rwkv = jnp.dot(rwkv_ref[...], wbr_ref[...], preferred_element_type=F32)
    ub = _rms(x, g_ref[...]).astype(BF16)
    gate_slice()
    kv_all = jnp.concatenate([kvp_ref[0], kv_ref[0]], axis=0)
    kdup, v_t = _attn_prep_kv(kv_all, qg_ref[...], kg_ref[...])
    gate_slice()
    qb, inv_rms = _attn_prep_q(q_ref[0])
    first_tile = pl.program_id(1) == 0
    scores = [_attn_scores(j, kh, qb, kdup, inv_rms, first_tile if j == 0 else None)
              for j, kh in units]
    soft = []
    for idx, (_, kh) in enumerate(units):
        if len(gate_parts) < n_u and idx % 4 != 1:
            gate_slice()
        soft.append(_attn_softmax(kh, *scores[idx], sink_ref))
    while len(gate_parts) < n_u:
        gate_slice()
    gates = jnp.concatenate(gate_parts, axis=1)
    for unit, sm in zip(units, soft):
        _attn_values(*unit, *sm, v_t, attn_ref)
    y_attn = lax.dot_general(attn_ref[...], wba_ref[...], _TN, preferred_element_type=F32)
    mixed = _sigmoid(gates[:, :d]) * y_attn + _sigmoid(gates[:, d:]) * y_rwkv
    h_ref[...] = x + _bdot(mixed, wo_ref[...])


def _attn_merge(x2, q, kv, rwkv, gain, q_gain, k_gain, sinks, w_gate, w_ba, w_br, w_out):
    n, d = x2.shape
    b, t, _ = q.shape
    tq = min(ATTN_TILE, t)
    nblk = tq // BLOCK
    nt = t // tq
    q_gain = jnp.tile(q_gain * (HEAD_DIM ** -0.5), (1, ATTN_KV_HEADS))
    k_gain = jnp.tile(k_gain, (1, ATTN_KV_HEADS))

    def full(a):
        return pl.BlockSpec(a.shape, lambda bi, i: (0, 0))

    def rows(width):
        return pl.BlockSpec((tq, width), lambda bi, i: (bi * nt + i, 0))

    return pl.pallas_call(
        _attn_merge_kernel,
        grid=(b, nt),
        in_specs=[
            rows(d),
            pl.BlockSpec((1, tq, ATTN_Q_WIDTH), lambda bi, i: (bi, i, 0)),
            pl.BlockSpec((1, tq, 2 * ATTN_KV_WIDTH), lambda bi, i: (bi, i, 0)),
            pl.BlockSpec((1, BLOCK, 2 * ATTN_KV_WIDTH),
                         lambda bi, i: (bi, jnp.maximum(i * nblk - 1, 0), 0)),
            rows(rwkv.shape[1]),
            full(gain), full(q_gain), full(k_gain),
            pl.BlockSpec(memory_space=pltpu.SMEM),
            full(w_gate), full(w_ba), full(w_br), full(w_out),
        ],
        out_specs=rows(d),
        out_shape=jax.ShapeDtypeStruct((n, d), F32),
        scratch_shapes=[pltpu.VMEM((ATTN_Q_WIDTH, tq), BF16)],
        compiler_params=pltpu.CompilerParams(
            dimension_semantics=("arbitrary", "arbitrary"), vmem_limit_bytes=VMEM_LIMIT),
        name="attn_merge",
    )(x2, q, kv, kv, rwkv, gain, q_gain, k_gain, sinks, w_gate, w_ba, w_br, w_out)


def _mlp_kernel(h_ref, g_ref, w1_ref, w2_ref, o_ref):
    h = h_ref[...]
    hidden = jnp.square(jnp.maximum(_bdot(_rms(h, g_ref[...]), w1_ref[...]), 0.0))
    o_ref[...] = h + _bdot(hidden, w2_ref[...])


def _mlp(h2, gain, w1, w2):
    n, d = h2.shape
    tm = min(ROW_TILE // 2, n)

    def full(a):
        return pl.BlockSpec(a.shape, lambda i: (0, 0))

    return pl.pallas_call(
        _mlp_kernel,
        grid=(n // tm,),
        in_specs=[pl.BlockSpec((tm, d), lambda i: (i, 0)), full(gain), full(w1), full(w2)],
        out_specs=pl.BlockSpec((tm, d), lambda i: (i, 0)),
        out_shape=jax.ShapeDtypeStruct((n, d), F32),
        compiler_params=pltpu.CompilerParams(
            dimension_semantics=("arbitrary",), vmem_limit_bytes=VMEM_LIMIT),
        name="mlp",
    )(h2, gain, w1, w2)


def _layer(h, norm1_gain, w_in, q_norm_gain, k_norm_gain, attn_sinks,
           mu_r, mu_k, mu_v, mu_w, mu_a, mu_g, decay_bias, decay_up, aaa_bias, aaa_up,
           gate_up, k_k, k_a, r_k, ln_x_gain, ln_x_bias, w_branch_attn, w_branch_rwkv,
           w_out, norm2_gain, w_ff_in, w_ff_out):
    b, t, d = h.shape
    x2 = h.reshape(b * t, d)
    row = lambda a: a.reshape(1, -1).astype(F32)
    n_proj = ATTN_Q_WIDTH + 2 * ATTN_KV_WIDTH + RWKV_IN_WIDTH

    mu = jnp.concatenate([mu_r, mu_k, mu_v, mu_w, mu_a, mu_g]).reshape(1, -1).astype(F32)
    q, kv, rw = _inproj(x2, row(norm1_gain), w_in[:, :n_proj].astype(BF16), mu, t)
    rwkv = _rwkv(rw.reshape(b, t, -1), row(decay_bias), decay_up.astype(BF16),
                 row(aaa_bias), aaa_up.astype(BF16), gate_up.astype(BF16),
                 row(k_k), row(k_a), row(r_k), row(ln_x_gain), row(ln_x_bias))
    h2 = _attn_merge(x2, q.reshape(b, t, -1), kv.reshape(b, t, -1), rwkv.reshape(b * t, -1),
                     row(norm1_gain), row(q_norm_gain), row(k_norm_gain), attn_sinks.astype(F32),
                     w_in[:, n_proj:].astype(BF16), w_branch_attn.astype(BF16),
                     w_branch_rwkv.astype(BF16), w_out.astype(BF16))
    out = _mlp(h2, row(norm2_gain), w_ff_in.astype(BF16), w_ff_out.astype(BF16))
    return out.reshape(b, t, d)


def kernel(x, norm1_gain, w_in, q_norm_gain, k_norm_gain, attn_sinks, mu_r, mu_k, mu_v, mu_w, mu_a, mu_g, decay_bias, decay_up, aaa_bias, aaa_up, gate_up, k_k, k_a, r_k, ln_x_gain, ln_x_bias, w_branch_attn, w_branch_rwkv, w_out, norm2_gain, w_ff_in, w_ff_out):
    h = x.astype(F32)
    params = (norm1_gain, w_in, q_norm_gain, k_norm_gain, attn_sinks, mu_r, mu_k, mu_v, mu_w,
              mu_a, mu_g, decay_bias, decay_up, aaa_bias, aaa_up, gate_up, k_k, k_a, r_k,
              ln_x_gain, ln_x_bias, w_branch_attn, w_branch_rwkv, w_out, norm2_gain,
              w_ff_in, w_ff_out)
    for l in range(norm1_gain.shape[0]):
        h = _layer(h, *(p[l] for p in params))
    return h.astype(x.dtype)
```

```python
import functools
import math

import jax
import jax.numpy as jnp
from jax import lax
from jax.experimental import pallas as pl
from jax.experimental.pallas import tpu as pltpu

F32 = jnp.float32
BF16 = jnp.bfloat16

HEAD_DIM = 64
ATTN_Q_HEADS = 8
ATTN_KV_HEADS = 2
ATTN_GROUP = ATTN_Q_HEADS // ATTN_KV_HEADS
WINDOW = 128
BLOCK = 128
RWKV_HEADS = 8
RWKV_HEAD_SIZE = 64
LORA_DECAY = 64
LORA_AAA = 64
LORA_GATE = 128
ATTN_Q_WIDTH = ATTN_Q_HEADS * HEAD_DIM
ATTN_KV_WIDTH = ATTN_KV_HEADS * HEAD_DIM
RWKV_WIDTH = RWKV_HEADS * RWKV_HEAD_SIZE
RWKV_IN_WIDTH = 3 * RWKV_WIDTH + LORA_DECAY + LORA_AAA + LORA_GATE
RMS_EPS = 1e-6
GN_EPS = 64e-5
L2_EPS = 1e-12

CHUNK = 64
ROW_TILE = 512
ATTN_TILE = 512
RWKV_TILE = 512
RWKV_GROUP = 256
VMEM_LIMIT = 56 * 1024 * 1024

_NT = (((1,), (1,)), ((), ()))
_TN = (((0,), (0,)), ((), ()))


def _bdot(a, b):
    return jnp.dot(a.astype(BF16), b.astype(BF16), preferred_element_type=F32)


def _rms(x, gain):
    return x * lax.rsqrt(jnp.mean(x * x, axis=-1, keepdims=True) + RMS_EPS) * gain


def _sigmoid(x):
    return 0.5 * jnp.tanh(0.5 * x) + 0.5


def _inproj_kernel(x_ref, g_ref, w_ref, mu_ref, q_ref, kv_ref, rw_ref, last_ref, *, tiles_per_seq):
    u = _rms(x_ref[...], g_ref[...])
    p = _bdot(u, w_ref[...])
    q_ref[...] = p[:, :ATTN_Q_WIDTH]
    kv_ref[...] = p[:, ATTN_Q_WIDTH:ATTN_Q_WIDTH + 2 * ATTN_KV_WIDTH]
    rw = p[:, ATTN_Q_WIDTH + 2 * ATTN_KV_WIDTH:]
    tm = rw.shape[0]
    seq_start = pl.program_id(0) % tiles_per_seq == 0
    before = jnp.where(seq_start, 0.0, last_ref[...])
    first = lax.broadcasted_iota(jnp.int32, (tm, 1), 0) == 0
    prev = jnp.where(first, before, pltpu.roll(rw, 1, 0))
    last_ref[...] = rw[tm - 1:tm, :]
    rw_ref[...] = rw + (prev - rw) * mu_ref[...]


def _inproj(x2, gain, w, mu, seq_len):
    n, d = x2.shape
    wid = w.shape[1]
    tm = min(2 * ROW_TILE, seq_len)

    def full(shape):
        return pl.BlockSpec(shape, lambda i: (0, 0), pipeline_mode=pl.Buffered(1))

    return pl.pallas_call(
        functools.partial(_inproj_kernel, tiles_per_seq=seq_len // tm),
        grid=(n // tm,),
        in_specs=[
            pl.BlockSpec((tm, d), lambda i: (i, 0)),
            full((1, d)), full((d, wid)), full((1, RWKV_IN_WIDTH)),
        ],
        out_specs=[
            pl.BlockSpec((tm, ATTN_Q_WIDTH), lambda i: (i, 0)),
            pl.BlockSpec((tm, 2 * ATTN_KV_WIDTH), lambda i: (i, 0)),
            pl.BlockSpec((tm, RWKV_IN_WIDTH), lambda i: (i, 0)),
        ],
        out_shape=[
            jax.ShapeDtypeStruct((n, ATTN_Q_WIDTH), F32),
            jax.ShapeDtypeStruct((n, 2 * ATTN_KV_WIDTH), F32),
            jax.ShapeDtypeStruct((n, RWKV_IN_WIDTH), F32),
        ],
        scratch_shapes=[pltpu.VMEM((1, RWKV_IN_WIDTH), F32)],
        compiler_params=pltpu.CompilerParams(
            dimension_semantics=("arbitrary",), vmem_limit_bytes=VMEM_LIMIT),
        name="inproj",
    )(x2, gain, w, mu)


def _head_sumsq(x, ones_bd):
    x2 = x * x
    hi = x2.astype(BF16)
    lo = (x2 - hi.astype(F32)).astype(BF16)
    return (jnp.dot(hi, ones_bd, preferred_element_type=F32)
            + jnp.dot(lo, ones_bd, preferred_element_type=F32))


def _attn_prep_kv(kv_all, q_gain, k_gain):
    hp = 2 * HEAD_DIM
    even = lax.broadcasted_iota(jnp.int32, (1, hp), 1) < HEAD_DIM
    ri = lax.broadcasted_iota(jnp.int32, (hp, hp), 0)
    ci = lax.broadcasted_iota(jnp.int32, (hp, hp), 1)
    ones_bd = ((ri < HEAD_DIM) == (ci < HEAD_DIM)).astype(BF16)
    k = kv_all[:, :hp]
    kn = k * lax.rsqrt(_head_sumsq(k, ones_bd) * (1.0 / HEAD_DIM) + RMS_EPS) * k_gain * q_gain
    kn_sw = pltpu.roll(kn, HEAD_DIM, 1)
    kdup = [jnp.where(even, kn, kn_sw).astype(BF16), jnp.where(even, kn_sw, kn).astype(BF16)]
    v_t = kv_all[:, hp:].T.astype(BF16)
    return kdup, v_t


def _attn_prep_q(q):
    hp = 2 * HEAD_DIM
    si = lax.broadcasted_iota(jnp.int32, (16, hp), 0)
    sj = lax.broadcasted_iota(jnp.int32, (16, hp), 1)
    sel = (((si == 0) & (sj < HEAD_DIM)) | ((si == 1) & (sj >= HEAD_DIM))).astype(BF16)
    q2 = q * q
    hi = q2.astype(BF16)
    lo = (q2 - hi.astype(F32)).astype(BF16)
    inv_rms = []
    for m in range(ATTN_Q_WIDTH // hp):
        cs = slice(m * hp, (m + 1) * hp)
        ss = (lax.dot_general(sel, hi[:, cs], _NT, preferred_element_type=F32)
              + lax.dot_general(sel, lo[:, cs], _NT, preferred_element_type=F32))
        inv_rms.append(lax.rsqrt(ss * (1.0 / HEAD_DIM) + RMS_EPS))
    return q.astype(BF16), inv_rms


def _attn_scores(j, kh, qb, kdup, inv_rms, no_prev):
    hp = 2 * HEAD_DIM
    even = lax.broadcasted_iota(jnp.int32, (1, hp), 1) < HEAD_DIM
    zero_q = jnp.zeros((BLOCK, hp), BF16)
    parts, scale = [], []
    for g in range(ATTN_GROUP):
        h = kh * ATTN_GROUP + g
        col = qb[j * BLOCK:(j + 1) * BLOCK, (h // 2) * hp:(h // 2 + 1) * hp]
        parts.append(jnp.where(even, col, zero_q) if h % 2 == 0 else jnp.where(even, zero_q, col))
        scale.append(inv_rms[h // 2][h % 2:h % 2 + 1, j * BLOCK:(j + 1) * BLOCK])
    s_t = lax.dot_general(kdup[kh][j * BLOCK:(j + 2) * BLOCK], jnp.concatenate(parts, axis=0), _NT,
                          preferred_element_type=F32)
    key = lax.broadcasted_iota(jnp.int32, (BLOCK, ATTN_GROUP * BLOCK), 0)
    qry = lax.broadcasted_iota(jnp.int32, (BLOCK, ATTN_GROUP * BLOCK), 1) & (BLOCK - 1)
    use_prev = key > qry
    prev = s_t[:BLOCK]
    if no_prev is not None:
        prev = jnp.where(no_prev, -jnp.inf, prev)
    folded = jnp.where(use_prev, prev, s_t[BLOCK:]) * jnp.concatenate(scale, axis=1)
    return folded, use_prev


def _attn_softmax(kh, folded, use_prev, sink_ref):
    sink = jnp.concatenate(
        [jnp.full((1, BLOCK), sink_ref[kh * ATTN_GROUP + g], F32) for g in range(ATTN_GROUP)], axis=1)
    m = jnp.maximum(jnp.max(folded, axis=0, keepdims=True), sink)
    p = jnp.exp(folded - m)
    den = jnp.sum(p, axis=0, keepdims=True) + jnp.exp(sink - m)
    zero = jnp.zeros_like(p)
    p_t = jnp.concatenate([jnp.where(use_prev, p, zero), jnp.where(use_prev, zero, p)], axis=0)
    return p_t.astype(BF16), 1.0 / den


def _attn_values(j, kh, p_t, inv_den, v_t, o_ref):
    o_t = jnp.dot(v_t[kh * HEAD_DIM:(kh + 1) * HEAD_DIM, j * BLOCK:(j + 2) * BLOCK], p_t,
                  preferred_element_type=F32) * inv_den
    for g in range(ATTN_GROUP):
        h = kh * ATTN_GROUP + g
        o_ref[h * HEAD_DIM:(h + 1) * HEAD_DIM, j * BLOCK:(j + 1) * BLOCK] = (
            o_t[:, g * BLOCK:(g + 1) * BLOCK].astype(o_ref.dtype))


def _pair_blockdiag(x, even):
    zero = jnp.zeros_like(x)
    return jnp.concatenate([jnp.where(even, x, zero), jnp.where(even, zero, x)], axis=0)


def _pair_sum(x, even):
    s_e = jnp.sum(jnp.where(even, x, 0.0), axis=-1, keepdims=True)
    s_o = jnp.sum(jnp.where(even, 0.0, x), axis=-1, keepdims=True)
    return jnp.where(even, s_e, s_o)


def _rwkv_prep(xs, prm):
    c = CHUNK
    hp = 2 * RWKV_HEAD_SIZE
    w = RWKV_WIDTH
    rows = xs.shape[0]
    nc = rows // c
    cols = [slice(p * hp, (p + 1) * hp) for p in range(w // hp)]
    r = xs[:, 0:w]
    k = xs[:, w:2 * w]
    v = xs[:, 2 * w:3 * w]
    w_lora = xs[:, 3 * w:3 * w + LORA_DECAY]
    a_lora = xs[:, 3 * w + LORA_DECAY:3 * w + LORA_DECAY + LORA_AAA]
    g_lora = xs[:, 3 * w + LORA_DECAY + LORA_AAA:]

    d = prm["dbias"][...] + _bdot(jnp.tanh(w_lora), prm["dup"][...])
    lw = (-math.exp(-0.5)) * _sigmoid(d)
    a_sig = _sigmoid(prm["abias"][...] + _bdot(a_lora, prm["aup"][...]))
    gate = _bdot(_sigmoid(g_lora), prm["gup"][...])
    yield

    kkr = k * prm["kk"][...]
    k2 = k * (1.0 + (a_sig - 1.0) * prm["ka"][...])
    rk = r * k2 * prm["rk"][...]
    even_t = lax.broadcasted_iota(jnp.int32, (rows, hp), 1) < RWKV_HEAD_SIZE
    den = jnp.concatenate(
        [jnp.maximum(jnp.sqrt(_pair_sum(kkr[:, cs] * kkr[:, cs], even_t)), L2_EPS) for cs in cols],
        axis=1)
    bonus = jnp.concatenate([_pair_sum(rk[:, cs], even_t) for cs in cols], axis=1) * v
    kk = kkr / den
    a_s = -kk
    b_s = kk * a_sig
    yield

    ti = lax.broadcasted_iota(jnp.int32, (c, 3 * c), 0)
    si = lax.broadcasted_iota(jnp.int32, (c, 3 * c), 1) & (c - 1)
    lower3 = (ti >= si).astype(BF16)
    lw_hi = lw.astype(BF16)
    lw_r = lw - lw_hi.astype(F32)
    lw_mid = lw_r.astype(BF16)
    lw_lo = (lw_r - lw_mid.astype(F32)).astype(BF16)
    cl = jnp.concatenate(
        [jnp.dot(lower3, jnp.concatenate([z[ci * c:(ci + 1) * c] for z in (lw_hi, lw_mid, lw_lo)], axis=0),
                 preferred_element_type=F32) for ci in range(nc)], axis=0)
    yield

    e_c = jnp.exp(cl)
    e_neg = jnp.exp(-cl)
    g_rows = [e_c[ci * c + c - 1:ci * c + c, :] for ci in range(nc)]
    g_full = jnp.concatenate([jnp.broadcast_to(g, (c, w)) for g in g_rows], axis=0)
    at = a_s * jnp.exp(cl - lw)
    rt = r * e_c
    yield
    bt = b_s * e_neg
    kt = k2 * e_neg
    ops = dict(rt=rt, g_rows=g_rows, gate=gate, bonus=bonus,
               bh=(bt * g_full).astype(BF16), khat=(kt * g_full).astype(BF16),
               at_b=at.astype(BF16), rt_b=rt.astype(BF16), bt_b=bt.astype(BF16),
               kt_b=kt.astype(BF16), v_b=v.astype(BF16))
    return ops


def _rwkv_solve(ops, state, out_rows, prm):
    c = CHUNK
    hp = 2 * RWKV_HEAD_SIZE
    w = RWKV_WIDTH
    npair = w // hp
    nc = ops["rt"].shape[0] // c
    cols = [slice(p * hp, (p + 1) * hp) for p in range(npair)]
    even = lax.broadcasted_iota(jnp.int32, (c, hp), 1) < RWKV_HEAD_SIZE
    gi = lax.broadcasted_iota(jnp.int32, (2 * c, 2 * hp), 0)
    g_t = gi & (c - 1)
    g_s = lax.broadcasted_iota(jnp.int32, (2 * c, 2 * hp), 1) & (c - 1)
    g_mask = (g_t > g_s) | ((gi >= c) & (g_t == g_s))
    eye2 = (lax.broadcasted_iota(jnp.int32, (c, hp), 0)
            == (lax.broadcasted_iota(jnp.int32, (c, hp), 1) & (c - 1)))
    pi = lax.broadcasted_iota(jnp.int32, (hp, 2 * hp), 0)
    pj = lax.broadcasted_iota(jnp.int32, (hp, 2 * hp), 1)
    same_head = (pi < RWKV_HEAD_SIZE) == ((pj & (hp - 1)) < RWKV_HEAD_SIZE)
    eye = (lax.broadcasted_iota(jnp.int32, (hp, hp), 0)
           == lax.broadcasted_iota(jnp.int32, (hp, hp), 1))
    units = [(ci, p) for ci in range(nc) for p in range(npair)]

    def col(name, ci, p):
        return ops[name][ci * c:(ci + 1) * c, cols[p]]

    def bd(x):
        return _pair_blockdiag(x, even)

    def lanes(*xs):
        return jnp.concatenate(xs, axis=1)

    def rows(*xs):
        return jnp.concatenate(xs, axis=0)

    def mm(a, b):
        return jnp.dot(a, b, preferred_element_type=F32)

    gm = [jnp.where(g_mask,
                    lax.dot_general(rows(col("at_b", ci, p), col("rt_b", ci, p)),
                                    rows(bd(col("bt_b", ci, p)), bd(col("kt_b", ci, p))), _NT,
                                    preferred_element_type=F32), 0.0)
          for ci, p in units]
    yield
    a_pow = [g[:c, :hp].astype(BF16) for g in gm]
    t_inv = [jnp.where(eye2, 1.0, g[:c, :hp]) for g in gm]
    a_pow = [mm(a, bd(a)).astype(BF16) for a in a_pow]
    yield
    for _ in range(int(math.log2(c)) - 2):
        res = [mm(a, lanes(bd(t.astype(BF16)), bd(a))) for a, t in zip(a_pow, t_inv)]
        t_inv = [t + z[:, :hp] for t, z in zip(t_inv, res)]
        a_pow = [z[:, hp:].astype(BF16) for z in res]
        yield
    t_inv = [t + mm(a, bd(t.astype(BF16))) for a, t in zip(a_pow, t_inv)]
    yield
    wx = [mm(t.astype(BF16), lanes(bd(col("at_b", ci, p)), bd(g[:c, hp:].astype(BF16))))
          for (ci, p), t, g in zip(units, t_inv, gm)]
    yield
    v_st = [col("v_b", ci, p) for ci, p in units]
    w_b = [z[:, :hp].astype(BF16) for z in wx]
    u0_b = [mm(z[:, hp:].astype(BF16), bd(v)).astype(BF16) for z, v in zip(wx, v_st)]
    yield
    ab_y = [mm(g[c:, :].astype(BF16),
               rows(lanes(bd(wb), bd(ub)), lanes(jnp.zeros((hp, hp), BF16), bd(v))))
            for g, wb, ub, v in zip(gm, w_b, u0_b, v_st)]
    yield
    ab_h = [jnp.where(same_head,
                      lax.dot_general(rows(col("bh", ci, p), col("khat", ci, p)),
                                      rows(lanes(wb, ub), lanes(jnp.zeros((c, hp), BF16), v)), _TN,
                                      preferred_element_type=F32), 0.0)
            for (ci, p), wb, ub, v in zip(units, w_b, u0_b, v_st)]
    yield

    for ci in range(nc):
        y_cols = []
        for p in range(npair):
            idx = ci * npair + p
            qmat = col("rt", ci, p) + ab_y[idx][:, :hp]
            g_diag = jnp.where(eye, jnp.broadcast_to(ops["g_rows"][ci][:, cols[p]], (hp, hp)), 0.0)
            mmat = g_diag + ab_h[idx][:, :hp]
            ys = mm(rows(qmat, mmat).astype(BF16), state[p].astype(BF16))
            y = ys[:c] + ab_y[idx][:, hp:]
            state[p] = ys[c:] + ab_h[idx][:, hp:]
            mean = _pair_sum(y, even) * (1.0 / RWKV_HEAD_SIZE)
            yc = y - mean
            var = _pair_sum(yc * yc, even) * (1.0 / RWKV_HEAD_SIZE)
            y_cols.append(yc * lax.rsqrt(var + GN_EPS))
        yn = jnp.concatenate(y_cols, axis=1)
        rs = slice(ci * c, (ci + 1) * c)
        out_rows(ci, (yn * prm["lng"][...] + prm["lnb"][...] + ops["bonus"][rs]) * ops["gate"][rs])
        yield


def _drain(gen):
    try:
        while True:
            next(gen)
    except StopIteration as stop:
        return stop.value


def _rwkv_kernel(xs_ref, dbias_ref, dup_ref, abias_ref, aup_ref, gup_ref,
                 kk_ref, ka_ref, rk_ref, lng_ref, lnb_ref, o_ref, state_ref):
    hp = 2 * RWKV_HEAD_SIZE
    tb = xs_ref.shape[1]
    gr = min(RWKV_GROUP, tb)
    ngroup = tb // gr
    prm = dict(dbias=dbias_ref, dup=dup_ref, abias=abias_ref, aup=aup_ref, gup=gup_ref,
               kk=kk_ref, ka=ka_ref, rk=rk_ref, lng=lng_ref, lnb=lnb_ref)

    @pl.when(pl.program_id(1) == 0)
    def _():
        state_ref[...] = jnp.zeros_like(state_ref)

    def prep(g):
        return _rwkv_prep(xs_ref[0, g * gr:(g + 1) * gr, :], prm)

    state = [state_ref[:, p * hp:(p + 1) * hp] for p in range(RWKV_WIDTH // hp)]
    ops = _drain(prep(0))
    for g in range(ngroup):
        def out_rows(ci, value, g=g):
            r0 = g * gr + ci * CHUNK
            o_ref[0, r0:r0 + CHUNK, :] = value.astype(o_ref.dtype)

        nxt = prep(g + 1) if g + 1 < ngroup else None
        nxt_ops = None
        for _ in _rwkv_solve(ops, state, out_rows, prm):
            if nxt is not None:
                try:
                    next(nxt)
                except StopIteration as stop:
                    nxt_ops, nxt = stop.value, None
        if nxt is not None:
            nxt_ops = _drain(nxt)
        ops = nxt_ops
    for p in range(RWKV_WIDTH // hp):
        state_ref[:, p * hp:(p + 1) * hp] = state[p]


def _rwkv(rw, decay_bias, decay_up, aaa_bias, aaa_up, gate_up, k_k, k_a, r_k, ln_g, ln_b):
    b, t, _ = rw.shape
    tb = min(RWKV_TILE, t)
    w = RWKV_WIDTH

    def full(shape):
        return pl.BlockSpec(shape, lambda bi, ti: (0, 0))

    return pl.pallas_call(
        _rwkv_kernel,
        grid=(b, t // tb),
        in_specs=[
            pl.BlockSpec((1, tb, RWKV_IN_WIDTH), lambda bi, ti: (bi, ti, 0)),
            full((1, w)), full((LORA_DECAY, w)),
            full((1, w)), full((LORA_AAA, w)),
            full((LORA_GATE, w)),
            full((1, w)), full((1, w)), full((1, w)), full((1, w)), full((1, w)),
        ],
        out_specs=pl.BlockSpec((1, tb, w), lambda bi, ti: (bi, ti, 0)),
        out_shape=jax.ShapeDtypeStruct((b, t, w), BF16),
        scratch_shapes=[pltpu.VMEM((2 * RWKV_HEAD_SIZE, w), F32)],
        compiler_params=pltpu.CompilerParams(
            dimension_semantics=("arbitrary", "arbitrary"), vmem_limit_bytes=VMEM_LIMIT),
        name="rwkv",
    )(rw, decay_bias, decay_up, aaa_bias, aaa_up, gate_up, k_k, k_a, r_k, ln_g, ln_b)


def _attn_merge_kernel(x_ref, q_ref, kv_ref, kvp_ref, rwkv_ref, g_ref, qg_ref, kg_ref, sink_ref,
                       wg_ref, wba_ref, wbr_ref, wo_ref, h_ref, attn_ref):
    x = x_ref[...]
    d = x.shape[1]
    nblk = q_ref.shape[1] // BLOCK
    units = [(j, kh) for j in range(nblk) for kh in range(ATTN_KV_HEADS)]
    n_u = len(units)
    gw = wg_ref.shape[1] // n_u
    gate_parts = []

    def gate_slice():
        c0 = len(gate_parts) * gw
        gate_parts.append(jnp.dot(ub, wg_ref[:, c0:c0 + gw], preferred_element_type=F32))

    y_rwkv = jnp.dot(rwkv_ref[...], wbr_ref[...], preferred_element_type=F32)
    ub = _rms(x, g_ref[...]).astype(BF16)
    gate_slice()
    kv_all = jnp.concatenate([kvp_ref[0], kv_ref[0]], axis=0)
    kdup, v_t = _attn_prep_kv(kv_all, qg_ref[...], kg_ref[...])
    gate_slice()
    qb, inv_rms = _attn_prep_q(q_ref[0])
    first_tile = pl.program_id(1) == 0
    scores = [_attn_scores(j, kh, qb, kdup, inv_rms, first_tile if j == 0 else None)
              for j, kh in units]
    soft = []
    for idx, (_, kh) in enumerate(units):
        if len(gate_parts) < n_u and idx % 4 != 1:
            gate_slice()
        soft.append(_attn_softmax(kh, *scores[idx], sink_ref))
    while len(gate_parts) < n_u:
        gate_slice()
    gates = jnp.concatenate(gate_parts, axis=1)
    for unit, sm in zip(units, soft):
        _attn_values(*unit, *sm, v_t, attn_ref)
    y_attn = lax.dot_general(attn_ref[...], wba_ref[...], _TN, preferred_element_type=F32)
    mixed = _sigmoid(gates[:, :d]) * y_attn + _sigmoid(gates[:, d:]) * y_rwkv
    h_ref[...] = x + _bdot(mixed, wo_ref[...])


def _attn_merge(x2, q, kv, rwkv, gain, q_gain, k_gain, sinks, w_gate, w_ba, w_br, w_out):
    n, d = x2.shape
    b, t, _ = q.shape
    tq = min(ATTN_TILE, t)
    nblk = tq // BLOCK
    nt = t // tq
    q_gain = jnp.tile(q_gain * (HEAD_DIM ** -0.5), (1, ATTN_KV_HEADS))
    k_gain = jnp.tile(k_gain, (1, ATTN_KV_HEADS))

    def full(a):
        return pl.BlockSpec(a.shape, lambda bi, i: (0, 0), pipeline_mode=pl.Buffered(1))

    def rows(width):
        return pl.BlockSpec((tq, width), lambda bi, i: (bi * nt + i, 0))

    return pl.pallas_call(
        _attn_merge_kernel,
        grid=(b, nt),
        in_specs=[
            rows(d),
            pl.BlockSpec((1, tq, ATTN_Q_WIDTH), lambda bi, i: (bi, i, 0)),
            pl.BlockSpec((1, tq, 2 * ATTN_KV_WIDTH), lambda bi, i: (bi, i, 0)),
            pl.BlockSpec((1, BLOCK, 2 * ATTN_KV_WIDTH),
                         lambda bi, i: (bi, jnp.maximum(i * nblk - 1, 0), 0)),
            rows(rwkv.shape[1]),
            full(gain), full(q_gain), full(k_gain),
            pl.BlockSpec(memory_space=pltpu.SMEM),
            full(w_gate), full(w_ba), full(w_br), full(w_out),
        ],
        out_specs=rows(d),
        out_shape=jax.ShapeDtypeStruct((n, d), F32),
        scratch_shapes=[pltpu.VMEM((ATTN_Q_WIDTH, tq), BF16)],
        compiler_params=pltpu.CompilerParams(
            dimension_semantics=("arbitrary", "arbitrary"), vmem_limit_bytes=VMEM_LIMIT),
        name="attn_merge",
    )(x2, q, kv, kv, rwkv, gain, q_gain, k_gain, sinks, w_gate, w_ba, w_br, w_out)


def _mlp_kernel(h_ref, g_ref, w1_ref, w2_ref, o_ref):
    h = h_ref[...]
    hidden = jnp.square(jnp.maximum(_bdot(_rms(h, g_ref[...]), w1_ref[...]), 0.0))
    o_ref[...] = h + _bdot(hidden, w2_ref[...])


def _mlp(h2, gain, w1, w2):
    n, d = h2.shape
    tm = min(ROW_TILE, n)

    def full(a):
        return pl.BlockSpec(a.shape, lambda i: (0, 0), pipeline_mode=pl.Buffered(1))

    return pl.pallas_call(
        _mlp_kernel,
        grid=(n // tm,),
        in_specs=[pl.BlockSpec((tm, d), lambda i: (i, 0)), full(gain), full(w1), full(w2)],
        out_specs=pl.BlockSpec((tm, d), lambda i: (i, 0)),
        out_shape=jax.ShapeDtypeStruct((n, d), F32),
        compiler_params=pltpu.CompilerParams(
            dimension_semantics=("arbitrary",), vmem_limit_bytes=VMEM_LIMIT),
        name="mlp",
    )(h2, gain, w1, w2)


def _layer(h, norm1_gain, w_in, q_norm_gain, k_norm_gain, attn_sinks,
           mu_r, mu_k, mu_v, mu_w, mu_a, mu_g, decay_bias, decay_up, aaa_bias, aaa_up,
           gate_up, k_k, k_a, r_k, ln_x_gain, ln_x_bias, w_branch_attn, w_branch_rwkv,
           w_out, norm2_gain, w_ff_in, w_ff_out):
    b, t, d = h.shape
    x2 = h.reshape(b * t, d)
    row = lambda a: a.reshape(1, -1).astype(F32)
    n_proj = ATTN_Q_WIDTH + 2 * ATTN_KV_WIDTH + RWKV_IN_WIDTH

    mu = jnp.concatenate([mu_r, mu_k, mu_v, mu_w, mu_a, mu_g]).reshape(1, -1).astype(F32)
    q, kv, rw = _inproj(x2, row(norm1_gain), w_in[:, :n_proj].astype(BF16), mu, t)
    rwkv = _rwkv(rw.reshape(b, t, -1), row(decay_bias), decay_up.astype(BF16),
                 row(aaa_bias), aaa_up.astype(BF16), gate_up.astype(BF16),
                 row(k_k), row(k_a), row(r_k), row(ln_x_gain), row(ln_x_bias))
    h2 = _attn_merge(x2, q.reshape(b, t, -1), kv.reshape(b, t, -1), rwkv.reshape(b * t, -1),
                     row(norm1_gain), row(q_norm_gain), row(k_norm_gain), attn_sinks.astype(F32),
                     w_in[:, n_proj:].astype(BF16), w_branch_attn.astype(BF16),
                     w_branch_rwkv.astype(BF16), w_out.astype(BF16))
    out = _mlp(h2, row(norm2_gain), w_ff_in.astype(BF16), w_ff_out.astype(BF16))
    return out.reshape(b, t, d)


def kernel(x, norm1_gain, w_in, q_norm_gain, k_norm_gain, attn_sinks, mu_r, mu_k, mu_v, mu_w, mu_a, mu_g, decay_bias, decay_up, aaa_bias, aaa_up, gate_up, k_k, k_a, r_k, ln_x_gain, ln_x_bias, w_branch_attn, w_branch_rwkv, w_out, norm2_gain, w_ff_in, w_ff_out):
    h = x.astype(F32)
    params = (norm1_gain, w_in, q_norm_gain, k_norm_gain, attn_sinks, mu_r, mu_k, mu_v, mu_w,
              mu_a, mu_g, decay_bias, decay_up, aaa_bias, aaa_up, gate_up, k_k, k_a, r_k,
              ln_x_gain, ln_x_bias, w_branch_attn, w_branch_rwkv, w_out, norm2_gain,
              w_ff_in, w_ff_out)
    for l in range(norm1_gain.shape[0]):
        h = _layer(h, *(p[l] for p in params))
    return h.astype(x.dtype)
```

```python
import functools
import math

import jax
import jax.numpy as jnp
from jax import lax
from jax.experimental import pallas as pl
from jax.experimental.pallas import tpu as pltpu

F32 = jnp.float32
BF16 = jnp.bfloat16

HEAD_DIM = 64
ATTN_Q_HEADS = 8
ATTN_KV_HEADS = 2
ATTN_GROUP = ATTN_Q_HEADS // ATTN_KV_HEADS
WINDOW = 128
BLOCK = 128
RWKV_HEADS = 8
RWKV_HEAD_SIZE = 64
LORA_DECAY = 64
LORA_AAA = 64
LORA_GATE = 128
ATTN_Q_WIDTH = ATTN_Q_HEADS * HEAD_DIM
ATTN_KV_WIDTH = ATTN_KV_HEADS * HEAD_DIM
RWKV_WIDTH = RWKV_HEADS * RWKV_HEAD_SIZE
RWKV_IN_WIDTH = 3 * RWKV_WIDTH + LORA_DECAY + LORA_AAA + LORA_GATE
RMS_EPS = 1e-6
GN_EPS = 64e-5
L2_EPS = 1e-12

CHUNK = 64
ROW_TILE = 512
ATTN_TILE = 512
RWKV_TILE = 1024
RWKV_GROUP = 256
VMEM_LIMIT = 56 * 1024 * 1024

_NT = (((1,), (1,)), ((), ()))
_TN = (((0,), (0,)), ((), ()))


def _bdot(a, b):
    return jnp.dot(a.astype(BF16), b.astype(BF16), preferred_element_type=F32)


def _rms(x, gain):
    return x * lax.rsqrt(jnp.mean(x * x, axis=-1, keepdims=True) + RMS_EPS) * gain


def _sigmoid(x):
    return 0.5 * jnp.tanh(0.5 * x) + 0.5


def _inproj_kernel(x_ref, g_ref, w_ref, mu_ref, q_ref, kv_ref, rw_ref, last_ref, *, tiles_per_seq):
    u = _rms(x_ref[...], g_ref[...])
    p = _bdot(u, w_ref[...])
    q_ref[...] = p[:, :ATTN_Q_WIDTH]
    kv_ref[...] = p[:, ATTN_Q_WIDTH:ATTN_Q_WIDTH + 2 * ATTN_KV_WIDTH]
    rw = p[:, ATTN_Q_WIDTH + 2 * ATTN_KV_WIDTH:]
    tm = rw.shape[0]
    seq_start = pl.program_id(0) % tiles_per_seq == 0
    before = jnp.where(seq_start, 0.0, last_ref[...])
    first = lax.broadcasted_iota(jnp.int32, (tm, 1), 0) == 0
    prev = jnp.where(first, before, pltpu.roll(rw, 1, 0))
    last_ref[...] = rw[tm - 1:tm, :]
    rw_ref[...] = rw + (prev - rw) * mu_ref[...]


def _inproj(x2, gain, w, mu, seq_len):
    n, d = x2.shape
    wid = w.shape[1]
    tm = min(2 * ROW_TILE, seq_len)

    def full(shape):
        return pl.BlockSpec(shape, lambda i: (0, 0), pipeline_mode=pl.Buffered(1))

    return pl.pallas_call(
        functools.partial(_inproj_kernel, tiles_per_seq=seq_len // tm),
        grid=(n // tm,),
        in_specs=[
            pl.BlockSpec((tm, d), lambda i: (i, 0)),
            full((1, d)), full((d, wid)), full((1, RWKV_IN_WIDTH)),
        ],
        out_specs=[
            pl.BlockSpec((tm, ATTN_Q_WIDTH), lambda i: (i, 0)),
            pl.BlockSpec((tm, 2 * ATTN_KV_WIDTH), lambda i: (i, 0)),
            pl.BlockSpec((tm, RWKV_IN_WIDTH), lambda i: (i, 0)),
        ],
        out_shape=[
            jax.ShapeDtypeStruct((n, ATTN_Q_WIDTH), F32),
            jax.ShapeDtypeStruct((n, 2 * ATTN_KV_WIDTH), F32),
            jax.ShapeDtypeStruct((n, RWKV_IN_WIDTH), F32),
        ],
        scratch_shapes=[pltpu.VMEM((1, RWKV_IN_WIDTH), F32)],
        compiler_params=pltpu.CompilerParams(
            dimension_semantics=("arbitrary",), vmem_limit_bytes=VMEM_LIMIT),
        name="inproj",
    )(x2, gain, w, mu)


def _head_sumsq(x, ones_bd):
    x2 = x * x
    hi = x2.astype(BF16)
    lo = (x2 - hi.astype(F32)).astype(BF16)
    return (jnp.dot(hi, ones_bd, preferred_element_type=F32)
            + jnp.dot(lo, ones_bd, preferred_element_type=F32))


def _attn_prep_kv(kv_all, q_gain, k_gain):
    hp = 2 * HEAD_DIM
    even = lax.broadcasted_iota(jnp.int32, (1, hp), 1) < HEAD_DIM
    ri = lax.broadcasted_iota(jnp.int32, (hp, hp), 0)
    ci = lax.broadcasted_iota(jnp.int32, (hp, hp), 1)
    ones_bd = ((ri < HEAD_DIM) == (ci < HEAD_DIM)).astype(BF16)
    k = kv_all[:, :hp]
    kn = k * lax.rsqrt(_head_sumsq(k, ones_bd) * (1.0 / HEAD_DIM) + RMS_EPS) * k_gain * q_gain
    kn_sw = pltpu.roll(kn, HEAD_DIM, 1)
    kdup = [jnp.where(even, kn, kn_sw).astype(BF16), jnp.where(even, kn_sw, kn).astype(BF16)]
    v_t = kv_all[:, hp:].T.astype(BF16)
    return kdup, v_t


def _attn_prep_q(q):
    hp = 2 * HEAD_DIM
    si = lax.broadcasted_iota(jnp.int32, (16, hp), 0)
    sj = lax.broadcasted_iota(jnp.int32, (16, hp), 1)
    sel = (((si == 0) & (sj < HEAD_DIM)) | ((si == 1) & (sj >= HEAD_DIM))).astype(BF16)
    q2 = q * q
    hi = q2.astype(BF16)
    lo = (q2 - hi.astype(F32)).astype(BF16)
    inv_rms = []
    for m in range(ATTN_Q_WIDTH // hp):
        cs = slice(m * hp, (m + 1) * hp)
        ss = (lax.dot_general(sel, hi[:, cs], _NT, preferred_element_type=F32)
              + lax.dot_general(sel, lo[:, cs], _NT, preferred_element_type=F32))
        inv_rms.append(lax.rsqrt(ss * (1.0 / HEAD_DIM) + RMS_EPS))
    return q.astype(BF16), inv_rms


def _attn_scores(j, kh, qb, kdup, inv_rms, no_prev):
    hp = 2 * HEAD_DIM
    even = lax.broadcasted_iota(jnp.int32, (1, hp), 1) < HEAD_DIM
    zero_q = jnp.zeros((BLOCK, hp), BF16)
    parts, scale = [], []
    for g in range(ATTN_GROUP):
        h = kh * ATTN_GROUP + g
        col = qb[j * BLOCK:(j + 1) * BLOCK, (h // 2) * hp:(h // 2 + 1) * hp]
        parts.append(jnp.where(even, col, zero_q) if h % 2 == 0 else jnp.where(even, zero_q, col))
        scale.append(inv_rms[h // 2][h % 2:h % 2 + 1, j * BLOCK:(j + 1) * BLOCK])
    s_t = lax.dot_general(kdup[kh][j * BLOCK:(j + 2) * BLOCK], jnp.concatenate(parts, axis=0), _NT,
                          preferred_element_type=F32)
    key = lax.broadcasted_iota(jnp.int32, (BLOCK, ATTN_GROUP * BLOCK), 0)
    qry = lax.broadcasted_iota(jnp.int32, (BLOCK, ATTN_GROUP * BLOCK), 1) & (BLOCK - 1)
    use_prev = key > qry
    prev = s_t[:BLOCK]
    if no_prev is not None:
        prev = jnp.where(no_prev, -jnp.inf, prev)
    folded = jnp.where(use_prev, prev, s_t[BLOCK:]) * jnp.concatenate(scale, axis=1)
    return folded, use_prev


def _attn_softmax(kh, folded, use_prev, sink_ref):
    sink = jnp.concatenate(
        [jnp.full((1, BLOCK), sink_ref[kh * ATTN_GROUP + g], F32) for g in range(ATTN_GROUP)], axis=1)
    m = jnp.maximum(jnp.max(folded, axis=0, keepdims=True), sink)
    p = jnp.exp(folded - m)
    den = jnp.sum(p, axis=0, keepdims=True) + jnp.exp(sink - m)
    zero = jnp.zeros_like(p)
    p_t = jnp.concatenate([jnp.where(use_prev, p, zero), jnp.where(use_prev, zero, p)], axis=0)
    return p_t.astype(BF16), 1.0 / den


def _attn_values(j, kh, p_t, inv_den, v_t, o_ref):
    o_t = jnp.dot(v_t[kh * HEAD_DIM:(kh + 1) * HEAD_DIM, j * BLOCK:(j + 2) * BLOCK], p_t,
                  preferred_element_type=F32) * inv_den
    for g in range(ATTN_GROUP):
        h = kh * ATTN_GROUP + g
        o_ref[h * HEAD_DIM:(h + 1) * HEAD_DIM, j * BLOCK:(j + 1) * BLOCK] = (
            o_t[:, g * BLOCK:(g + 1) * BLOCK].astype(o_ref.dtype))


def _pair_blockdiag(x, even):
    zero = jnp.zeros_like(x)
    return jnp.concatenate([jnp.where(even, x, zero), jnp.where(even, zero, x)], axis=0)


def _pair_sum(x, even):
    s_e = jnp.sum(jnp.where(even, x, 0.0), axis=-1, keepdims=True)
    s_o = jnp.sum(jnp.where(even, 0.0, x), axis=-1, keepdims=True)
    return jnp.where(even, s_e, s_o)


def _rwkv_prep(xs, prm):
    c = CHUNK
    hp = 2 * RWKV_HEAD_SIZE
    w = RWKV_WIDTH
    rows = xs.shape[0]
    nc = rows // c
    cols = [slice(p * hp, (p + 1) * hp) for p in range(w // hp)]
    r = xs[:, 0:w]
    k = xs[:, w:2 * w]
    v = xs[:, 2 * w:3 * w]
    w_lora = xs[:, 3 * w:3 * w + LORA_DECAY]
    a_lora = xs[:, 3 * w + LORA_DECAY:3 * w + LORA_DECAY + LORA_AAA]
    g_lora = xs[:, 3 * w + LORA_DECAY + LORA_AAA:]

    d = prm["dbias"][...] + _bdot(jnp.tanh(w_lora), prm["dup"][...])
    lw = (-math.exp(-0.5)) * _sigmoid(d)
    a_sig = _sigmoid(prm["abias"][...] + _bdot(a_lora, prm["aup"][...]))
    gate = _bdot(_sigmoid(g_lora), prm["gup"][...])
    yield

    kkr = k * prm["kk"][...]
    ka = prm["ka"][...]
    k2 = k * (a_sig * ka + (1.0 - ka))
    rk = r * k2 * prm["rk"][...]
    even_t = lax.broadcasted_iota(jnp.int32, (rows, hp), 1) < RWKV_HEAD_SIZE
    den = jnp.concatenate(
        [jnp.maximum(jnp.sqrt(_pair_sum(kkr[:, cs] * kkr[:, cs], even_t)), L2_EPS) for cs in cols],
        axis=1)
    bonus = jnp.concatenate([_pair_sum(rk[:, cs], even_t) for cs in cols], axis=1) * v
    kk = kkr / den
    a_s = -kk
    b_s = kk * a_sig
    yield

    ti = lax.broadcasted_iota(jnp.int32, (c, 3 * c), 0)
    si = lax.broadcasted_iota(jnp.int32, (c, 3 * c), 1) & (c - 1)
    lower3 = (ti >= si).astype(BF16)
    lw_hi = lw.astype(BF16)
    lw_r = lw - lw_hi.astype(F32)
    lw_mid = lw_r.astype(BF16)
    lw_lo = (lw_r - lw_mid.astype(F32)).astype(BF16)
    cl = jnp.concatenate(
        [jnp.dot(lower3, jnp.concatenate([z[ci * c:(ci + 1) * c] for z in (lw_hi, lw_mid, lw_lo)], axis=0),
                 preferred_element_type=F32) for ci in range(nc)], axis=0)
    yield

    e_c = jnp.exp(cl)
    e_neg = 1.0 / e_c
    g_rows = [e_c[ci * c + c - 1:ci * c + c, :] for ci in range(nc)]
    g_full = jnp.concatenate([jnp.broadcast_to(g, (c, w)) for g in g_rows], axis=0)
    at = a_s * jnp.exp(cl - lw)
    rt = r * e_c
    yield
    bt = b_s * e_neg
    kt = k2 * e_neg
    ops = dict(rt=rt, g_rows=g_rows, gate=gate, bonus=bonus,
               bh=(bt * g_full).astype(BF16), khat=(kt * g_full).astype(BF16),
               at_b=at.astype(BF16), rt_b=rt.astype(BF16), bt_b=bt.astype(BF16),
               kt_b=kt.astype(BF16), v_b=v.astype(BF16))
    return ops


def _rwkv_solve(ops, prm):
    c = CHUNK
    hp = 2 * RWKV_HEAD_SIZE
    w = RWKV_WIDTH
    npair = w // hp
    nc = ops["rt"].shape[0] // c
    cols = [slice(p * hp, (p + 1) * hp) for p in range(npair)]
    even = lax.broadcasted_iota(jnp.int32, (c, hp), 1) < RWKV_HEAD_SIZE
    gi = lax.broadcasted_iota(jnp.int32, (2 * c, 2 * hp), 0)
    g_t = gi & (c - 1)
    g_s = lax.broadcasted_iota(jnp.int32, (2 * c, 2 * hp), 1) & (c - 1)
    g_mask = (g_t > g_s) | ((gi >= c) & (g_t == g_s))
    eye2 = (lax.broadcasted_iota(jnp.int32, (c, hp), 0)
            == (lax.broadcasted_iota(jnp.int32, (c, hp), 1) & (c - 1)))
    pi = lax.broadcasted_iota(jnp.int32, (hp, 2 * hp), 0)
    pj = lax.broadcasted_iota(jnp.int32, (hp, 2 * hp), 1)
    same_head = (pi < RWKV_HEAD_SIZE) == ((pj & (hp - 1)) < RWKV_HEAD_SIZE)
    eye = (lax.broadcasted_iota(jnp.int32, (hp, hp), 0)
           == lax.broadcasted_iota(jnp.int32, (hp, hp), 1))
    units = [(ci, p) for ci in range(nc) for p in range(npair)]

    def col(name, ci, p):
        return ops[name][ci * c:(ci + 1) * c, cols[p]]

    def bd(x):
        return _pair_blockdiag(x, even)

    def lanes(*xs):
        return jnp.concatenate(xs, axis=1)

    def rows(*xs):
        return jnp.concatenate(xs, axis=0)

    def mm(a, b):
        return jnp.dot(a, b, preferred_element_type=F32)

    gm = [jnp.where(g_mask,
                    lax.dot_general(rows(col("at_b", ci, p), col("rt_b", ci, p)),
                                    rows(bd(col("bt_b", ci, p)), bd(col("kt_b", ci, p))), _NT,
                                    preferred_element_type=F32), 0.0)
          for ci, p in units]
    yield
    a_pow = [g[:c, :hp].astype(BF16) for g in gm]
    t_inv = [jnp.where(eye2, 1.0, g[:c, :hp]) for g in gm]
    a_pow = [mm(a, bd(a)).astype(BF16) for a in a_pow]
    yield
    for _ in range(int(math.log2(c)) - 2):
        res = [mm(a, lanes(bd(t.astype(BF16)), bd(a))) for a, t in zip(a_pow, t_inv)]
        t_inv = [t + z[:, :hp] for t, z in zip(t_inv, res)]
        a_pow = [z[:, hp:].astype(BF16) for z in res]
        yield
    t_inv = [t + mm(a, bd(t.astype(BF16))) for a, t in zip(a_pow, t_inv)]
    yield
    wx = [mm(t.astype(BF16), lanes(bd(col("at_b", ci, p)), bd(g[:c, hp:].astype(BF16))))
          for (ci, p), t, g in zip(units, t_inv, gm)]
    yield
    v_st = [col("v_b", ci, p) for ci, p in units]
    w_b = [z[:, :hp].astype(BF16) for z in wx]
    u0_b = [mm(z[:, hp:].astype(BF16), bd(v)).astype(BF16) for z, v in zip(wx, v_st)]
    yield
    ab_y = [mm(g[c:, :].astype(BF16),
               rows(lanes(bd(wb), bd(ub)), lanes(jnp.zeros((hp, hp), BF16), bd(v))))
            for g, wb, ub, v in zip(gm, w_b, u0_b, v_st)]
    yield
    ab_h = [jnp.where(same_head,
                      lax.dot_general(rows(col("bh", ci, p), col("khat", ci, p)),
                                      rows(lanes(wb, ub), lanes(jnp.zeros((c, hp), BF16), v)), _TN,
                                      preferred_element_type=F32), 0.0)
            for (ci, p), wb, ub, v in zip(units, w_b, u0_b, v_st)]
    yield

    def tail(state, out_rows):
        for ci in range(nc):
            y_cols = []
            for p in range(npair):
                idx = ci * npair + p
                qmat = col("rt", ci, p) + ab_y[idx][:, :hp]
                g_diag = jnp.where(eye, jnp.broadcast_to(ops["g_rows"][ci][:, cols[p]], (hp, hp)), 0.0)
                mmat = g_diag + ab_h[idx][:, :hp]
                ys = mm(rows(qmat, mmat).astype(BF16), state[p].astype(BF16))
                y = ys[:c] + ab_y[idx][:, hp:]
                state[p] = ys[c:] + ab_h[idx][:, hp:]
                mean = _pair_sum(y, even) * (1.0 / RWKV_HEAD_SIZE)
                yc = y - mean
                var = _pair_sum(yc * yc, even) * (1.0 / RWKV_HEAD_SIZE)
                y_cols.append(yc * lax.rsqrt(var + GN_EPS))
            yn = jnp.concatenate(y_cols, axis=1)
            rs = slice(ci * c, (ci + 1) * c)
            out_rows(ci, (yn * prm["lng"][...] + prm["lnb"][...] + ops["bonus"][rs]) * ops["gate"][rs])
            yield

    return tail


def _round_robin(gens):
    values = [None] * len(gens)
    live = list(range(len(gens)))
    while live:
        for i in list(live):
            try:
                next(gens[i])
            except StopIteration as stop:
                values[i] = stop.value
                live.remove(i)
    return values


def _rwkv_kernel(xs_ref, dbias_ref, dup_ref, abias_ref, aup_ref, gup_ref,
                 kk_ref, ka_ref, rk_ref, lng_ref, lnb_ref, o_ref, state_ref):
    hp = 2 * RWKV_HEAD_SIZE
    tb = xs_ref.shape[1]
    gr = min(RWKV_GROUP, tb)
    ngroup = tb // gr
    prm = dict(dbias=dbias_ref, dup=dup_ref, abias=abias_ref, aup=aup_ref, gup=gup_ref,
               kk=kk_ref, ka=ka_ref, rk=rk_ref, lng=lng_ref, lnb=lnb_ref)

    @pl.when(pl.program_id(1) == 0)
    def _():
        state_ref[...] = jnp.zeros_like(state_ref)

    def prep(g):
        return _rwkv_prep(xs_ref[0, g * gr:(g + 1) * gr, :], prm)

    state = [state_ref[:, p * hp:(p + 1) * hp] for p in range(RWKV_WIDTH // hp)]
    (ops,) = _round_robin([prep(0)])
    done = _round_robin([_rwkv_solve(ops, prm)] + ([prep(1)] if ngroup > 1 else []))
    tail, ops = done[0], (done[1] if ngroup > 1 else None)
    for g in range(ngroup):
        def out_rows(ci, value, g=g):
            r0 = g * gr + ci * CHUNK
            o_ref[0, r0:r0 + CHUNK, :] = value.astype(o_ref.dtype)

        gens = [tail(state, out_rows)]
        if g + 1 < ngroup:
            gens.append(_rwkv_solve(ops, prm))
        if g + 2 < ngroup:
            gens.append(prep(g + 2))
        done = _round_robin(gens)
        tail = done[1] if g + 1 < ngroup else None
        ops = done[2] if g + 2 < ngroup else None
    for p in range(RWKV_WIDTH // hp):
        state_ref[:, p * hp:(p + 1) * hp] = state[p]


def _rwkv(rw, decay_bias, decay_up, aaa_bias, aaa_up, gate_up, k_k, k_a, r_k, ln_g, ln_b):
    b, t, _ = rw.shape
    tb = min(RWKV_TILE, t)
    w = RWKV_WIDTH

    def full(shape):
        return pl.BlockSpec(shape, lambda bi, ti: (0, 0))

    return pl.pallas_call(
        _rwkv_kernel,
        grid=(b, t // tb),
        in_specs=[
            pl.BlockSpec((1, tb, RWKV_IN_WIDTH), lambda bi, ti: (bi, ti, 0)),
            full((1, w)), full((LORA_DECAY, w)),
            full((1, w)), full((LORA_AAA, w)),
            full((LORA_GATE, w)),
            full((1, w)), full((1, w)), full((1, w)), full((1, w)), full((1, w)),
        ],
        out_specs=pl.BlockSpec((1, tb, w), lambda bi, ti: (bi, ti, 0)),
        out_shape=jax.ShapeDtypeStruct((b, t, w), BF16),
        scratch_shapes=[pltpu.VMEM((2 * RWKV_HEAD_SIZE, w), F32)],
        compiler_params=pltpu.CompilerParams(
            dimension_semantics=("arbitrary", "arbitrary"), vmem_limit_bytes=VMEM_LIMIT),
        name="rwkv",
    )(rw, decay_bias, decay_up, aaa_bias, aaa_up, gate_up, k_k, k_a, r_k, ln_g, ln_b)


def _attn_merge_kernel(x_ref, q_ref, kv_ref, kvp_ref, rwkv_ref, g_ref, qg_ref, kg_ref, sink_ref,
                       wg_ref, wba_ref, wbr_ref, wo_ref, h_ref, attn_ref):
    x = x_ref[...]
    d = x.shape[1]
    nblk = q_ref.shape[1] // BLOCK
    units = [(j, kh) for j in range(nblk) for kh in range(ATTN_KV_HEADS)]
    n_u = len(units)
    gw = wg_ref.shape[1] // n_u
    gate_parts = []

    def gate_slice():
        c0 = len(gate_parts) * gw
        gate_parts.append(jnp.dot(ub, wg_ref[:, c0:c0 + gw], preferred_element_type=F32))

    y_rwkv = jnp.dot(rwkv_ref[...], wbr_ref[...], preferred_element_type=F32)
    ub = _rms(x, g_ref[...]).astype(BF16)
    gate_slice()
    kv_all = jnp.concatenate([kvp_ref[0], kv_ref[0]], axis=0)
    kdup, v_t = _attn_prep_kv(kv_all, qg_ref[...], kg_ref[...])
    gate_slice()
    qb, inv_rms = _attn_prep_q(q_ref[0])
    first_tile = pl.program_id(1) == 0
    scores = [_attn_scores(j, kh, qb, kdup, inv_rms, first_tile if j == 0 else None)
              for j, kh in units]
    soft = []
    for idx, (_, kh) in enumerate(units):
        if len(gate_parts) < n_u and idx % 4 != 1:
            gate_slice()
        soft.append(_attn_softmax(kh, *scores[idx], sink_ref))
    while len(gate_parts) < n_u:
        gate_slice()
    gates = jnp.concatenate(gate_parts, axis=1)
    for unit, sm in zip(units, soft):
        _attn_values(*unit, *sm, v_t, attn_ref)
    y_attn = lax.dot_general(attn_ref[...], wba_ref[...], _TN, preferred_element_type=F32)
    mixed = _sigmoid(gates[:, :d]) * y_attn + _sigmoid(gates[:, d:]) * y_rwkv
    h_ref[...] = x + _bdot(mixed, wo_ref[...])


def _attn_merge(x2, q, kv, rwkv, gain, q_gain, k_gain, sinks, w_gate, w_ba, w_br, w_out):
    n, d = x2.shape
    b, t, _ = q.shape
    tq = min(ATTN_TILE, t)
    nblk = tq // BLOCK
    nt = t // tq
    q_gain = jnp.tile(q_gain * (HEAD_DIM ** -0.5), (1, ATTN_KV_HEADS))
    k_gain = jnp.tile(k_gain, (1, ATTN_KV_HEADS))

    def full(a):
        return pl.BlockSpec(a.shape, lambda bi, i: (0, 0), pipeline_mode=pl.Buffered(1))

    def rows(width):
        return pl.BlockSpec((tq, width), lambda bi, i: (bi * nt + i, 0))

    return pl.pallas_call(
        _attn_merge_kernel,
        grid=(b, nt),
        in_specs=[
            rows(d),
            pl.BlockSpec((1, tq, ATTN_Q_WIDTH), lambda bi, i: (bi, i, 0)),
            pl.BlockSpec((1, tq, 2 * ATTN_KV_WIDTH), lambda bi, i: (bi, i, 0)),
            pl.BlockSpec((1, BLOCK, 2 * ATTN_KV_WIDTH),
                         lambda bi, i: (bi, jnp.maximum(i * nblk - 1, 0), 0)),
            rows(rwkv.shape[1]),
            full(gain), full(q_gain), full(k_gain),
            pl.BlockSpec(memory_space=pltpu.SMEM),
            full(w_gate), full(w_ba), full(w_br), full(w_out),
        ],
        out_specs=rows(d),
        out_shape=jax.ShapeDtypeStruct((n, d), F32),
        scratch_shapes=[pltpu.VMEM((ATTN_Q_WIDTH, tq), BF16)],
        compiler_params=pltpu.CompilerParams(
            dimension_semantics=("arbitrary", "arbitrary"), vmem_limit_bytes=VMEM_LIMIT),
        name="attn_merge",
    )(x2, q, kv, kv, rwkv, gain, q_gain, k_gain, sinks, w_gate, w_ba, w_br, w_out)


def _mlp_kernel(h_ref, g_ref, w1_ref, w2_ref, o_ref):
    h = h_ref[...]
    hidden = jnp.square(jnp.maximum(_bdot(_rms(h, g_ref[...]), w1_ref[...]), 0.0))
    o_ref[...] = h + _bdot(hidden, w2_ref[...])


def _mlp(h2, gain, w1, w2):
    n, d = h2.shape
    tm = min(ROW_TILE, n)

    def full(a):
        return pl.BlockSpec(a.shape, lambda i: (0, 0), pipeline_mode=pl.Buffered(1))

    return pl.pallas_call(
        _mlp_kernel,
        grid=(n // tm,),
        in_specs=[pl.BlockSpec((tm, d), lambda i: (i, 0)), full(gain), full(w1), full(w2)],
        out_specs=pl.BlockSpec((tm, d), lambda i: (i, 0)),
        out_shape=jax.ShapeDtypeStruct((n, d), F32),
        compiler_params=pltpu.CompilerParams(
            dimension_semantics=("arbitrary",), vmem_limit_bytes=VMEM_LIMIT),
        name="mlp",
    )(h2, gain, w1, w2)


def _layer(h, norm1_gain, w_in, q_norm_gain, k_norm_gain, attn_sinks,
           mu_r, mu_k, mu_v, mu_w, mu_a, mu_g, decay_bias, decay_up, aaa_bias, aaa_up,
           gate_up, k_k, k_a, r_k, ln_x_gain, ln_x_bias, w_branch_attn, w_branch_rwkv,
           w_out, norm2_gain, w_ff_in, w_ff_out):
    b, t, d = h.shape
    x2 = h.reshape(b * t, d)
    row = lambda a: a.reshape(1, -1).astype(F32)
    n_proj = ATTN_Q_WIDTH + 2 * ATTN_KV_WIDTH + RWKV_IN_WIDTH

    mu = jnp.concatenate([mu_r, mu_k, mu_v, mu_w, mu_a, mu_g]).reshape(1, -1).astype(F32)
    q, kv, rw = _inproj(x2, row(norm1_gain), w_in[:, :n_proj].astype(BF16), mu, t)
    rwkv = _rwkv(rw.reshape(b, t, -1), row(decay_bias), decay_up.astype(BF16),
                 row(aaa_bias), aaa_up.astype(BF16), gate_up.astype(BF16),
                 row(k_k), row(k_a), row(r_k), row(ln_x_gain), row(ln_x_bias))
    h2 = _attn_merge(x2, q.reshape(b, t, -1), kv.reshape(b, t, -1), rwkv.reshape(b * t, -1),
                     row(norm1_gain), row(q_norm_gain), row(k_norm_gain), attn_sinks.astype(F32),
                     w_in[:, n_proj:].astype(BF16), w_branch_attn.astype(BF16),
                     w_branch_rwkv.astype(BF16), w_out.astype(BF16))
    out = _mlp(h2, row(norm2_gain), w_ff_in.astype(BF16), w_ff_out.astype(BF16))
    return out.reshape(b, t, d)


def kernel(x, norm1_gain, w_in, q_norm_gain, k_norm_gain, attn_sinks, mu_r, mu_k, mu_v, mu_w, mu_a, mu_g, decay_bias, decay_up, aaa_bias, aaa_up, gate_up, k_k, k_a, r_k, ln_x_gain, ln_x_bias, w_branch_attn, w_branch_rwkv, w_out, norm2_gain, w_ff_in, w_ff_out):
    h = x.astype(F32)
    params = (norm1_gain, w_in, q_norm_gain, k_norm_gain, attn_sinks, mu_r, mu_k, mu_v, mu_w,
              mu_a, mu_g, decay_bias, decay_up, aaa_bias, aaa_up, gate_up, k_k, k_a, r_k,
              ln_x_gain, ln_x_bias, w_branch_attn, w_branch_rwkv, w_out, norm2_gain,
              w_ff_in, w_ff_out)
    for l in range(norm1_gain.shape[0]):
        h = _layer(h, *(p[l] for p in params))
    return h.astype(x.dtype)
```

```python
import functools
import math

import jax
import jax.numpy as jnp
from jax import lax
from jax.experimental import pallas as pl
from jax.experimental.pallas import tpu as pltpu

F32 = jnp.float32
BF16 = jnp.bfloat16

HEAD_DIM = 64
ATTN_Q_HEADS = 8
ATTN_KV_HEADS = 2
ATTN_GROUP = ATTN_Q_HEADS // ATTN_KV_HEADS
WINDOW = 128
BLOCK = 128
RWKV_HEADS = 8
RWKV_HEAD_SIZE = 64
LORA_DECAY = 64
LORA_AAA = 64
LORA_GATE = 128
ATTN_Q_WIDTH = ATTN_Q_HEADS * HEAD_DIM
ATTN_KV_WIDTH = ATTN_KV_HEADS * HEAD_DIM
RWKV_WIDTH = RWKV_HEADS * RWKV_HEAD_SIZE
RWKV_IN_WIDTH = 3 * RWKV_WIDTH + LORA_DECAY + LORA_AAA + LORA_GATE
RMS_EPS = 1e-6
GN_EPS = 64e-5
L2_EPS = 1e-12

CHUNK = 64
ROW_TILE = 512
ATTN_TILE = 1024
RWKV_TILE = 1024
RWKV_GROUP = 256
VMEM_LIMIT = 56 * 1024 * 1024

_NT = (((1,), (1,)), ((), ()))
_TN = (((0,), (0,)), ((), ()))


def _bdot(a, b):
    return jnp.dot(a.astype(BF16), b.astype(BF16), preferred_element_type=F32)


def _rms(x, gain):
    return x * lax.rsqrt(jnp.mean(x * x, axis=-1, keepdims=True) + RMS_EPS) * gain


def _sigmoid(x):
    return 0.5 * jnp.tanh(0.5 * x) + 0.5


def _inproj_kernel(x_ref, g_ref, w_ref, mu_ref, q_ref, kv_ref, rw_ref, last_ref, *, tiles_per_seq):
    u = _rms(x_ref[...], g_ref[...])
    p = _bdot(u, w_ref[...])
    q_ref[...] = p[:, :ATTN_Q_WIDTH]
    kv_ref[...] = p[:, ATTN_Q_WIDTH:ATTN_Q_WIDTH + 2 * ATTN_KV_WIDTH]
    rw = p[:, ATTN_Q_WIDTH + 2 * ATTN_KV_WIDTH:]
    tm = rw.shape[0]
    seq_start = pl.program_id(0) % tiles_per_seq == 0
    before = jnp.where(seq_start, 0.0, last_ref[...])
    first = lax.broadcasted_iota(jnp.int32, (tm, 1), 0) == 0
    prev = jnp.where(first, before, pltpu.roll(rw, 1, 0))
    last_ref[...] = rw[tm - 1:tm, :]
    rw_ref[...] = rw + (prev - rw) * mu_ref[...]


def _inproj(x2, gain, w, mu, seq_len):
    n, d = x2.shape
    wid = w.shape[1]
    tm = min(2 * ROW_TILE, seq_len)

    def full(shape):
        return pl.BlockSpec(shape, lambda i: (0, 0), pipeline_mode=pl.Buffered(1))

    return pl.pallas_call(
        functools.partial(_inproj_kernel, tiles_per_seq=seq_len // tm),
        grid=(n // tm,),
        in_specs=[
            pl.BlockSpec((tm, d), lambda i: (i, 0)),
            full((1, d)), full((d, wid)), full((1, RWKV_IN_WIDTH)),
        ],
        out_specs=[
            pl.BlockSpec((tm, ATTN_Q_WIDTH), lambda i: (i, 0)),
            pl.BlockSpec((tm, 2 * ATTN_KV_WIDTH), lambda i: (i, 0)),
            pl.BlockSpec((tm, RWKV_IN_WIDTH), lambda i: (i, 0)),
        ],
        out_shape=[
            jax.ShapeDtypeStruct((n, ATTN_Q_WIDTH), F32),
            jax.ShapeDtypeStruct((n, 2 * ATTN_KV_WIDTH), F32),
            jax.ShapeDtypeStruct((n, RWKV_IN_WIDTH), F32),
        ],
        scratch_shapes=[pltpu.VMEM((1, RWKV_IN_WIDTH), F32)],
        compiler_params=pltpu.CompilerParams(
            dimension_semantics=("arbitrary",), vmem_limit_bytes=VMEM_LIMIT),
        name="inproj",
    )(x2, gain, w, mu)


def _head_sumsq(x, ones_bd):
    x2 = x * x
    hi = x2.astype(BF16)
    lo = (x2 - hi.astype(F32)).astype(BF16)
    return (jnp.dot(hi, ones_bd, preferred_element_type=F32)
            + jnp.dot(lo, ones_bd, preferred_element_type=F32))


def _attn_prep_kv(kv_all, q_gain, k_gain):
    hp = 2 * HEAD_DIM
    even = lax.broadcasted_iota(jnp.int32, (1, hp), 1) < HEAD_DIM
    ri = lax.broadcasted_iota(jnp.int32, (hp, hp), 0)
    ci = lax.broadcasted_iota(jnp.int32, (hp, hp), 1)
    ones_bd = ((ri < HEAD_DIM) == (ci < HEAD_DIM)).astype(BF16)
    k = kv_all[:, :hp]
    kn = k * lax.rsqrt(_head_sumsq(k, ones_bd) * (1.0 / HEAD_DIM) + RMS_EPS) * k_gain * q_gain
    kn_sw = pltpu.roll(kn, HEAD_DIM, 1)
    kdup = [jnp.where(even, kn, kn_sw).astype(BF16), jnp.where(even, kn_sw, kn).astype(BF16)]
    v_t = kv_all[:, hp:].T.astype(BF16)
    return kdup, v_t


def _attn_prep_q(q):
    hp = 2 * HEAD_DIM
    si = lax.broadcasted_iota(jnp.int32, (16, hp), 0)
    sj = lax.broadcasted_iota(jnp.int32, (16, hp), 1)
    sel = (((si == 0) & (sj < HEAD_DIM)) | ((si == 1) & (sj >= HEAD_DIM))).astype(BF16)
    q2 = q * q
    hi = q2.astype(BF16)
    lo = (q2 - hi.astype(F32)).astype(BF16)
    inv_rms = []
    for m in range(ATTN_Q_WIDTH // hp):
        cs = slice(m * hp, (m + 1) * hp)
        ss = (lax.dot_general(sel, hi[:, cs], _NT, preferred_element_type=F32)
              + lax.dot_general(sel, lo[:, cs], _NT, preferred_element_type=F32))
        inv_rms.append(lax.rsqrt(ss * (1.0 / HEAD_DIM) + RMS_EPS))
    return q.astype(BF16), inv_rms


def _attn_scores(j, kh, qb, kdup, inv_rms, no_prev):
    hp = 2 * HEAD_DIM
    even = lax.broadcasted_iota(jnp.int32, (1, hp), 1) < HEAD_DIM
    zero_q = jnp.zeros((BLOCK, hp), BF16)
    parts, scale = [], []
    for g in range(ATTN_GROUP):
        h = kh * ATTN_GROUP + g
        col = qb[j * BLOCK:(j + 1) * BLOCK, (h // 2) * hp:(h // 2 + 1) * hp]
        parts.append(jnp.where(even, col, zero_q) if h % 2 == 0 else jnp.where(even, zero_q, col))
        scale.append(inv_rms[h // 2][h % 2:h % 2 + 1, j * BLOCK:(j + 1) * BLOCK])
    s_t = lax.dot_general(kdup[kh][j * BLOCK:(j + 2) * BLOCK], jnp.concatenate(parts, axis=0), _NT,
                          preferred_element_type=F32)
    key = lax.broadcasted_iota(jnp.int32, (BLOCK, ATTN_GROUP * BLOCK), 0)
    qry = lax.broadcasted_iota(jnp.int32, (BLOCK, ATTN_GROUP * BLOCK), 1) & (BLOCK - 1)
    use_prev = key > qry
    prev = s_t[:BLOCK]
    if no_prev is not None:
        prev = jnp.where(no_prev, -jnp.inf, prev)
    folded = jnp.where(use_prev, prev, s_t[BLOCK:]) * jnp.concatenate(scale, axis=1)
    return folded, use_prev


def _attn_softmax(kh, folded, use_prev, sink_ref):
    sink = jnp.concatenate(
        [jnp.full((1, BLOCK), sink_ref[kh * ATTN_GROUP + g], F32) for g in range(ATTN_GROUP)], axis=1)
    m = jnp.maximum(jnp.max(folded, axis=0, keepdims=True), sink)
    p = jnp.exp(folded - m)
    den = jnp.sum(p, axis=0, keepdims=True) + jnp.exp(sink - m)
    zero = jnp.zeros_like(p)
    p_t = jnp.concatenate([jnp.where(use_prev, p, zero), jnp.where(use_prev, zero, p)], axis=0)
    return p_t.astype(BF16), 1.0 / den


def _attn_values(j, kh, p_t, inv_den, v_t, o_ref):
    o_t = jnp.dot(v_t[kh * HEAD_DIM:(kh + 1) * HEAD_DIM, j * BLOCK:(j + 2) * BLOCK], p_t,
                  preferred_element_type=F32) * inv_den
    for g in range(ATTN_GROUP):
        h = kh * ATTN_GROUP + g
        o_ref[h * HEAD_DIM:(h + 1) * HEAD_DIM, j * BLOCK:(j + 1) * BLOCK] = (
            o_t[:, g * BLOCK:(g + 1) * BLOCK].astype(o_ref.dtype))


def _pair_blockdiag(x, even):
    zero = jnp.zeros_like(x)
    return jnp.concatenate([jnp.where(even, x, zero), jnp.where(even, zero, x)], axis=0)


def _pair_sum(x, even):
    s_e = jnp.sum(jnp.where(even, x, 0.0), axis=-1, keepdims=True)
    s_o = jnp.sum(jnp.where(even, 0.0, x), axis=-1, keepdims=True)
    return jnp.where(even, s_e, s_o)


def _rwkv_prep(xs, prm):
    c = CHUNK
    hp = 2 * RWKV_HEAD_SIZE
    w = RWKV_WIDTH
    rows = xs.shape[0]
    nc = rows // c
    cols = [slice(p * hp, (p + 1) * hp) for p in range(w // hp)]
    r = xs[:, 0:w]
    k = xs[:, w:2 * w]
    v = xs[:, 2 * w:3 * w]
    w_lora = xs[:, 3 * w:3 * w + LORA_DECAY]
    a_lora = xs[:, 3 * w + LORA_DECAY:3 * w + LORA_DECAY + LORA_AAA]
    g_lora = xs[:, 3 * w + LORA_DECAY + LORA_AAA:]

    d = prm["dbias"][...] + _bdot(jnp.tanh(w_lora), prm["dup"][...])
    lw = (-math.exp(-0.5)) * _sigmoid(d)
    a_sig = _sigmoid(prm["abias"][...] + _bdot(a_lora, prm["aup"][...]))
    gate = _bdot(_sigmoid(g_lora), prm["gup"][...])
    yield

    kkr = k * prm["kk"][...]
    ka = prm["ka"][...]
    k2 = k * (a_sig * ka + (1.0 - ka))
    rk = r * k2 * prm["rk"][...]
    even_t = lax.broadcasted_iota(jnp.int32, (rows, hp), 1) < RWKV_HEAD_SIZE
    den = jnp.concatenate(
        [jnp.maximum(jnp.sqrt(_pair_sum(kkr[:, cs] * kkr[:, cs], even_t)), L2_EPS) for cs in cols],
        axis=1)
    bonus = jnp.concatenate([_pair_sum(rk[:, cs], even_t) for cs in cols], axis=1) * v
    kk = kkr / den
    a_s = -kk
    b_s = kk * a_sig
    yield

    ti = lax.broadcasted_iota(jnp.int32, (c, 3 * c), 0)
    si = lax.broadcasted_iota(jnp.int32, (c, 3 * c), 1) & (c - 1)
    lower3 = (ti >= si).astype(BF16)
    lw_hi = lw.astype(BF16)
    lw_r = lw - lw_hi.astype(F32)
    lw_mid = lw_r.astype(BF16)
    lw_lo = (lw_r - lw_mid.astype(F32)).astype(BF16)
    cl = jnp.concatenate(
        [jnp.dot(lower3, jnp.concatenate([z[ci * c:(ci + 1) * c] for z in (lw_hi, lw_mid, lw_lo)], axis=0),
                 preferred_element_type=F32) for ci in range(nc)], axis=0)
    yield

    e_c = jnp.exp(cl)
    e_neg = 1.0 / e_c
    g_rows = [e_c[ci * c + c - 1:ci * c + c, :] for ci in range(nc)]
    g_full = jnp.concatenate([jnp.broadcast_to(g, (c, w)) for g in g_rows], axis=0)
    at = a_s * jnp.exp(cl - lw)
    rt = r * e_c
    yield
    bt = b_s * e_neg
    kt = k2 * e_neg
    ops = dict(rt=rt, g_rows=g_rows, gate=gate, bonus=bonus,
               bh=(bt * g_full).astype(BF16), khat=(kt * g_full).astype(BF16),
               at_b=at.astype(BF16), rt_b=rt.astype(BF16), bt_b=bt.astype(BF16),
               kt_b=kt.astype(BF16), v_b=v.astype(BF16))
    return ops


def _rwkv_solve(ops, prm):
    c = CHUNK
    hp = 2 * RWKV_HEAD_SIZE
    w = RWKV_WIDTH
    npair = w // hp
    nc = ops["rt"].shape[0] // c
    cols = [slice(p * hp, (p + 1) * hp) for p in range(npair)]
    even = lax.broadcasted_iota(jnp.int32, (c, hp), 1) < RWKV_HEAD_SIZE
    gi = lax.broadcasted_iota(jnp.int32, (2 * c, 2 * hp), 0)
    g_t = gi & (c - 1)
    g_s = lax.broadcasted_iota(jnp.int32, (2 * c, 2 * hp), 1) & (c - 1)
    g_mask = (g_t > g_s) | ((gi >= c) & (g_t == g_s))
    eye2 = (lax.broadcasted_iota(jnp.int32, (c, hp), 0)
            == (lax.broadcasted_iota(jnp.int32, (c, hp), 1) & (c - 1)))
    pi = lax.broadcasted_iota(jnp.int32, (hp, 2 * hp), 0)
    pj = lax.broadcasted_iota(jnp.int32, (hp, 2 * hp), 1)
    same_head = (pi < RWKV_HEAD_SIZE) == ((pj & (hp - 1)) < RWKV_HEAD_SIZE)
    eye = (lax.broadcasted_iota(jnp.int32, (hp, hp), 0)
           == lax.broadcasted_iota(jnp.int32, (hp, hp), 1))
    units = [(ci, p) for ci in range(nc) for p in range(npair)]

    def col(name, ci, p):
        return ops[name][ci * c:(ci + 1) * c, cols[p]]

    def bd(x):
        return _pair_blockdiag(x, even)

    def lanes(*xs):
        return jnp.concatenate(xs, axis=1)

    def rows(*xs):
        return jnp.concatenate(xs, axis=0)

    def mm(a, b):
        return jnp.dot(a, b, preferred_element_type=F32)

    gm = [jnp.where(g_mask,
                    lax.dot_general(rows(col("at_b", ci, p), col("rt_b", ci, p)),
                                    rows(bd(col("bt_b", ci, p)), bd(col("kt_b", ci, p))), _NT,
                                    preferred_element_type=F32), 0.0)
          for ci, p in units]
    yield
    a_pow = [g[:c, :hp].astype(BF16) for g in gm]
    t_inv = [jnp.where(eye2, 1.0, g[:c, :hp]) for g in gm]
    a_pow = [mm(a, bd(a)).astype(BF16) for a in a_pow]
    yield
    for _ in range(int(math.log2(c)) - 2):
        res = [mm(a, lanes(bd(t.astype(BF16)), bd(a))) for a, t in zip(a_pow, t_inv)]
        t_inv = [t + z[:, :hp] for t, z in zip(t_inv, res)]
        a_pow = [z[:, hp:].astype(BF16) for z in res]
        yield
    t_inv = [t + mm(a, bd(t.astype(BF16))) for a, t in zip(a_pow, t_inv)]
    yield
    wx = [mm(t.astype(BF16), lanes(bd(col("at_b", ci, p)), bd(g[:c, hp:].astype(BF16))))
          for (ci, p), t, g in zip(units, t_inv, gm)]
    yield
    v_st = [col("v_b", ci, p) for ci, p in units]
    w_b = [z[:, :hp].astype(BF16) for z in wx]
    u0_b = [mm(z[:, hp:].astype(BF16), bd(v)).astype(BF16) for z, v in zip(wx, v_st)]
    yield
    ab_y = [mm(g[c:, :].astype(BF16),
               rows(lanes(bd(wb), bd(ub)), lanes(jnp.zeros((hp, hp), BF16), bd(v))))
            for g, wb, ub, v in zip(gm, w_b, u0_b, v_st)]
    yield
    ab_h = [jnp.where(same_head,
                      lax.dot_general(rows(col("bh", ci, p), col("khat", ci, p)),
                                      rows(lanes(wb, ub), lanes(jnp.zeros((c, hp), BF16), v)), _TN,
                                      preferred_element_type=F32), 0.0)
            for (ci, p), wb, ub, v in zip(units, w_b, u0_b, v_st)]
    yield

    def tail(state, out_rows):
        for ci in range(nc):
            y_cols = []
            for p in range(npair):
                idx = ci * npair + p
                qmat = col("rt", ci, p) + ab_y[idx][:, :hp]
                g_diag = jnp.where(eye, jnp.broadcast_to(ops["g_rows"][ci][:, cols[p]], (hp, hp)), 0.0)
                mmat = g_diag + ab_h[idx][:, :hp]
                ys = mm(rows(qmat, mmat).astype(BF16), state[p].astype(BF16))
                y = ys[:c] + ab_y[idx][:, hp:]
                state[p] = ys[c:] + ab_h[idx][:, hp:]
                mean = _pair_sum(y, even) * (1.0 / RWKV_HEAD_SIZE)
                yc = y - mean
                var = _pair_sum(yc * yc, even) * (1.0 / RWKV_HEAD_SIZE)
                y_cols.append(yc * lax.rsqrt(var + GN_EPS))
            yn = jnp.concatenate(y_cols, axis=1)
            rs = slice(ci * c, (ci + 1) * c)
            out_rows(ci, (yn * prm["lng"][...] + prm["lnb"][...] + ops["bonus"][rs]) * ops["gate"][rs])
            yield

    return tail


def _round_robin(gens):
    values = [None] * len(gens)
    live = list(range(len(gens)))
    while live:
        for i in list(live):
            try:
                next(gens[i])
            except StopIteration as stop:
                values[i] = stop.value
                live.remove(i)
    return values


def _rwkv_kernel(xs_ref, dbias_ref, dup_ref, abias_ref, aup_ref, gup_ref,
                 kk_ref, ka_ref, rk_ref, lng_ref, lnb_ref, o_ref, state_ref):
    hp = 2 * RWKV_HEAD_SIZE
    tb = xs_ref.shape[1]
    gr = min(RWKV_GROUP, tb)
    ngroup = tb // gr
    prm = dict(dbias=dbias_ref, dup=dup_ref, abias=abias_ref, aup=aup_ref, gup=gup_ref,
               kk=kk_ref, ka=ka_ref, rk=rk_ref, lng=lng_ref, lnb=lnb_ref)

    @pl.when(pl.program_id(1) == 0)
    def _():
        state_ref[...] = jnp.zeros_like(state_ref)

    def prep(g):
        return _rwkv_prep(xs_ref[0, g * gr:(g + 1) * gr, :], prm)

    state = [state_ref[:, p * hp:(p + 1) * hp] for p in range(RWKV_WIDTH // hp)]
    (ops,) = _round_robin([prep(0)])
    done = _round_robin([_rwkv_solve(ops, prm)] + ([prep(1)] if ngroup > 1 else []))
    tail, ops = done[0], (done[1] if ngroup > 1 else None)
    for g in range(ngroup):
        def out_rows(ci, value, g=g):
            r0 = g * gr + ci * CHUNK
            o_ref[0, r0:r0 + CHUNK, :] = value.astype(o_ref.dtype)

        gens = [tail(state, out_rows)]
        if g + 1 < ngroup:
            gens.append(_rwkv_solve(ops, prm))
        if g + 2 < ngroup:
            gens.append(prep(g + 2))
        done = _round_robin(gens)
        tail = done[1] if g + 1 < ngroup else None
        ops = done[2] if g + 2 < ngroup else None
    for p in range(RWKV_WIDTH // hp):
        state_ref[:, p * hp:(p + 1) * hp] = state[p]


def _rwkv(rw, decay_bias, decay_up, aaa_bias, aaa_up, gate_up, k_k, k_a, r_k, ln_g, ln_b):
    b, t, _ = rw.shape
    tb = min(RWKV_TILE, t)
    w = RWKV_WIDTH

    def full(shape):
        return pl.BlockSpec(shape, lambda bi, ti: (0, 0))

    return pl.pallas_call(
        _rwkv_kernel,
        grid=(b, t // tb),
        in_specs=[
            pl.BlockSpec((1, tb, RWKV_IN_WIDTH), lambda bi, ti: (bi, ti, 0)),
            full((1, w)), full((LORA_DECAY, w)),
            full((1, w)), full((LORA_AAA, w)),
            full((LORA_GATE, w)),
            full((1, w)), full((1, w)), full((1, w)), full((1, w)), full((1, w)),
        ],
        out_specs=pl.BlockSpec((1, tb, w), lambda bi, ti: (bi, ti, 0)),
        out_shape=jax.ShapeDtypeStruct((b, t, w), BF16),
        scratch_shapes=[pltpu.VMEM((2 * RWKV_HEAD_SIZE, w), F32)],
        compiler_params=pltpu.CompilerParams(
            dimension_semantics=("arbitrary", "arbitrary"), vmem_limit_bytes=VMEM_LIMIT),
        name="rwkv",
    )(rw, decay_bias, decay_up, aaa_bias, aaa_up, gate_up, k_k, k_a, r_k, ln_g, ln_b)


def _attn_merge_kernel(x_ref, q_ref, kv_ref, kvp_ref, rwkv_ref, g_ref, qg_ref, kg_ref, sink_ref,
                       wg_ref, wba_ref, wbr_ref, wo_ref, h_ref, attn_ref):
    x = x_ref[...]
    d = x.shape[1]
    nblk = q_ref.shape[1] // BLOCK
    units = [(j, kh) for j in range(nblk) for kh in range(ATTN_KV_HEADS)]
    n_u = len(units)
    gw = 2 * BLOCK
    n_g = wg_ref.shape[1] // gw
    gate_parts = []

    def gate_slice():
        c0 = len(gate_parts) * gw
        gate_parts.append(jnp.dot(ub, wg_ref[:, c0:c0 + gw], preferred_element_type=F32))

    y_rwkv = jnp.dot(rwkv_ref[...], wbr_ref[...], preferred_element_type=F32)
    ub = _rms(x, g_ref[...]).astype(BF16)
    gate_slice()
    kv_all = jnp.concatenate([kvp_ref[0], kv_ref[0]], axis=0)
    kdup, v_t = _attn_prep_kv(kv_all, qg_ref[...], kg_ref[...])
    gate_slice()
    qb, inv_rms = _attn_prep_q(q_ref[0])
    first_tile = pl.program_id(1) == 0
    scores = [_attn_scores(j, kh, qb, kdup, inv_rms, first_tile if j == 0 else None)
              for j, kh in units]
    soft = []
    for idx, (_, kh) in enumerate(units):
        while len(gate_parts) < 2 + (idx + 1) * (n_g - 2) // n_u:
            gate_slice()
        soft.append(_attn_softmax(kh, *scores[idx], sink_ref))
    while len(gate_parts) < n_g:
        gate_slice()
    gates = jnp.concatenate(gate_parts, axis=1)
    for unit, sm in zip(units, soft):
        _attn_values(*unit, *sm, v_t, attn_ref)
    y_attn = lax.dot_general(attn_ref[...], wba_ref[...], _TN, preferred_element_type=F32)
    mixed = _sigmoid(gates[:, :d]) * y_attn + _sigmoid(gates[:, d:]) * y_rwkv
    h_ref[...] = x + _bdot(mixed, wo_ref[...])


def _attn_merge(x2, q, kv, rwkv, gain, q_gain, k_gain, sinks, w_gate, w_ba, w_br, w_out):
    n, d = x2.shape
    b, t, _ = q.shape
    tq = min(ATTN_TILE, t)
    nblk = tq // BLOCK
    nt = t // tq
    q_gain = jnp.tile(q_gain * (HEAD_DIM ** -0.5), (1, ATTN_KV_HEADS))
    k_gain = jnp.tile(k_gain, (1, ATTN_KV_HEADS))

    def full(a):
        return pl.BlockSpec(a.shape, lambda bi, i: (0, 0), pipeline_mode=pl.Buffered(1))

    def rows(width):
        return pl.BlockSpec((tq, width), lambda bi, i: (bi * nt + i, 0))

    return pl.pallas_call(
        _attn_merge_kernel,
        grid=(b, nt),
        in_specs=[
            rows(d),
            pl.BlockSpec((1, tq, ATTN_Q_WIDTH), lambda bi, i: (bi, i, 0)),
            pl.BlockSpec((1, tq, 2 * ATTN_KV_WIDTH), lambda bi, i: (bi, i, 0)),
            pl.BlockSpec((1, BLOCK, 2 * ATTN_KV_WIDTH),
                         lambda bi, i: (bi, jnp.maximum(i * nblk - 1, 0), 0)),
            rows(rwkv.shape[1]),
            full(gain), full(q_gain), full(k_gain),
            pl.BlockSpec(memory_space=pltpu.SMEM),
            full(w_gate), full(w_ba), full(w_br), full(w_out),
        ],
        out_specs=rows(d),
        out_shape=jax.ShapeDtypeStruct((n, d), F32),
        scratch_shapes=[pltpu.VMEM((ATTN_Q_WIDTH, tq), BF16)],
        compiler_params=pltpu.CompilerParams(
            dimension_semantics=("arbitrary", "arbitrary"), vmem_limit_bytes=VMEM_LIMIT),
        name="attn_merge",
    )(x2, q, kv, kv, rwkv, gain, q_gain, k_gain, sinks, w_gate, w_ba, w_br, w_out)


def _mlp_kernel(h_ref, g_ref, w1_ref, w2_ref, o_ref):
    h = h_ref[...]
    hidden = jnp.square(jnp.maximum(_bdot(_rms(h, g_ref[...]), w1_ref[...]), 0.0))
    o_ref[...] = h + _bdot(hidden, w2_ref[...])


def _mlp(h2, gain, w1, w2):
    n, d = h2.shape
    tm = min(ROW_TILE, n)

    def full(a):
        return pl.BlockSpec(a.shape, lambda i: (0, 0), pipeline_mode=pl.Buffered(1))

    return pl.pallas_call(
        _mlp_kernel,
        grid=(n // tm,),
        in_specs=[pl.BlockSpec((tm, d), lambda i: (i, 0)), full(gain), full(w1), full(w2)],
        out_specs=pl.BlockSpec((tm, d), lambda i: (i, 0)),
        out_shape=jax.ShapeDtypeStruct((n, d), F32),
        compiler_params=pltpu.CompilerParams(
            dimension_semantics=("arbitrary",), vmem_limit_bytes=VMEM_LIMIT),
        name="mlp",
    )(h2, gain, w1, w2)


def _layer(h, norm1_gain, w_in, q_norm_gain, k_norm_gain, attn_sinks,
           mu_r, mu_k, mu_v, mu_w, mu_a, mu_g, decay_bias, decay_up, aaa_bias, aaa_up,
           gate_up, k_k, k_a, r_k, ln_x_gain, ln_x_bias, w_branch_attn, w_branch_rwkv,
           w_out, norm2_gain, w_ff_in, w_ff_out):
    b, t, d = h.shape
    x2 = h.reshape(b * t, d)
    row = lambda a: a.reshape(1, -1).astype(F32)
    n_proj = ATTN_Q_WIDTH + 2 * ATTN_KV_WIDTH + RWKV_IN_WIDTH

    mu = jnp.concatenate([mu_r, mu_k, mu_v, mu_w, mu_a, mu_g]).reshape(1, -1).astype(F32)
    q, kv, rw = _inproj(x2, row(norm1_gain), w_in[:, :n_proj].astype(BF16), mu, t)
    rwkv = _rwkv(rw.reshape(b, t, -1), row(decay_bias), decay_up.astype(BF16),
                 row(aaa_bias), aaa_up.astype(BF16), gate_up.astype(BF16),
                 row(k_k), row(k_a), row(r_k), row(ln_x_gain), row(ln_x_bias))
    h2 = _attn_merge(x2, q.reshape(b, t, -1), kv.reshape(b, t, -1), rwkv.reshape(b * t, -1),
                     row(norm1_gain), row(q_norm_gain), row(k_norm_gain), attn_sinks.astype(F32),
                     w_in[:, n_proj:].astype(BF16), w_branch_attn.astype(BF16),
                     w_branch_rwkv.astype(BF16), w_out.astype(BF16))
    out = _mlp(h2, row(norm2_gain), w_ff_in.astype(BF16), w_ff_out.astype(BF16))
    return out.reshape(b, t, d)


def kernel(x, norm1_gain, w_in, q_norm_gain, k_norm_gain, attn_sinks, mu_r, mu_k, mu_v, mu_w, mu_a, mu_g, decay_bias, decay_up, aaa_bias, aaa_up, gate_up, k_k, k_a, r_k, ln_x_gain, ln_x_bias, w_branch_attn, w_branch_rwkv, w_out, norm2_gain, w_ff_in, w_ff_out):
    h = x.astype(F32)
    params = (norm1_gain, w_in, q_norm_gain, k_norm_gain, attn_sinks, mu_r, mu_k, mu_v, mu_w,
              mu_a, mu_g, decay_bias, decay_up, aaa_bias, aaa_up, gate_up, k_k, k_a, r_k,
              ln_x_gain, ln_x_bias, w_branch_attn, w_branch_rwkv, w_out, norm2_gain,
              w_ff_in, w_ff_out)
    for l in range(norm1_gain.shape[0]):
        h = _layer(h, *(p[l] for p in params))
    return h.astype(x.dtype)
```

```python
import functools
import math

import jax
import jax.numpy as jnp
from jax import lax
from jax.experimental import pallas as pl
from jax.experimental.pallas import tpu as pltpu

F32 = jnp.float32
BF16 = jnp.bfloat16

HEAD_DIM = 64
ATTN_Q_HEADS = 8
ATTN_KV_HEADS = 2
ATTN_GROUP = ATTN_Q_HEADS // ATTN_KV_HEADS
WINDOW = 128
BLOCK = 128
RWKV_HEADS = 8
RWKV_HEAD_SIZE = 64
LORA_DECAY = 64
LORA_AAA = 64
LORA_GATE = 128
ATTN_Q_WIDTH = ATTN_Q_HEADS * HEAD_DIM
ATTN_KV_WIDTH = ATTN_KV_HEADS * HEAD_DIM
RWKV_WIDTH = RWKV_HEADS * RWKV_HEAD_SIZE
RWKV_IN_WIDTH = 3 * RWKV_WIDTH + LORA_DECAY + LORA_AAA + LORA_GATE
N_PROJ = ATTN_Q_WIDTH + 2 * ATTN_KV_WIDTH + RWKV_IN_WIDTH
RMS_EPS = 1e-6
GN_EPS = 64e-5
L2_EPS = 1e-12

CHUNK = 64
ROW_TILE = 512
ATTN_TILE = 1024
RWKV_TILE = 1024
RWKV_GROUP = 256
VMEM_LIMIT = 56 * 1024 * 1024

_NT = (((1,), (1,)), ((), ()))
_TN = (((0,), (0,)), ((), ()))


def _bdot(a, b):
    return jnp.dot(a.astype(BF16), b.astype(BF16), preferred_element_type=F32)


def _rms(x, gain):
    return x * lax.rsqrt(jnp.mean(x * x, axis=-1, keepdims=True) + RMS_EPS) * gain


def _sigmoid(x):
    return 0.5 * jnp.tanh(0.5 * x) + 0.5


def _inproj_kernel(x_ref, g_ref, w_ref, mu_ref, q_ref, kv_ref, rw_ref, last_ref, *, tiles_per_seq):
    u = _rms(x_ref[...], g_ref[...])
    p = _bdot(u, w_ref[...])
    q_ref[...] = p[:, :ATTN_Q_WIDTH]
    kv_ref[...] = p[:, ATTN_Q_WIDTH:ATTN_Q_WIDTH + 2 * ATTN_KV_WIDTH]
    rw = p[:, ATTN_Q_WIDTH + 2 * ATTN_KV_WIDTH:]
    tm = rw.shape[0]
    seq_start = pl.program_id(0) % tiles_per_seq == 0
    before = jnp.where(seq_start, 0.0, last_ref[...])
    first = lax.broadcasted_iota(jnp.int32, (tm, 1), 0) == 0
    prev = jnp.where(first, before, pltpu.roll(rw, 1, 0))
    last_ref[...] = rw[tm - 1:tm, :]
    rw_ref[...] = rw + (prev - rw) * mu_ref[...]


def _inproj(x2, gain, w, mu, seq_len):
    n, d = x2.shape
    wid = N_PROJ
    tm = min(2 * ROW_TILE, seq_len)

    def full(shape):
        return pl.BlockSpec(shape, lambda i: (0, 0), pipeline_mode=pl.Buffered(1))

    return pl.pallas_call(
        functools.partial(_inproj_kernel, tiles_per_seq=seq_len // tm),
        grid=(n // tm,),
        in_specs=[
            pl.BlockSpec((tm, d), lambda i: (i, 0)),
            full((1, d)), full((d, wid)), full((1, RWKV_IN_WIDTH)),
        ],
        out_specs=[
            pl.BlockSpec((tm, ATTN_Q_WIDTH), lambda i: (i, 0)),
            pl.BlockSpec((tm, 2 * ATTN_KV_WIDTH), lambda i: (i, 0)),
            pl.BlockSpec((tm, RWKV_IN_WIDTH), lambda i: (i, 0)),
        ],
        out_shape=[
            jax.ShapeDtypeStruct((n, ATTN_Q_WIDTH), F32),
            jax.ShapeDtypeStruct((n, 2 * ATTN_KV_WIDTH), F32),
            jax.ShapeDtypeStruct((n, RWKV_IN_WIDTH), F32),
        ],
        scratch_shapes=[pltpu.VMEM((1, RWKV_IN_WIDTH), F32)],
        compiler_params=pltpu.CompilerParams(
            dimension_semantics=("arbitrary",), vmem_limit_bytes=VMEM_LIMIT),
        name="inproj",
    )(x2, gain, w, mu)


def _head_sumsq(x, ones_bd):
    x2 = x * x
    hi = x2.astype(BF16)
    lo = (x2 - hi.astype(F32)).astype(BF16)
    return (jnp.dot(hi, ones_bd, preferred_element_type=F32)
            + jnp.dot(lo, ones_bd, preferred_element_type=F32))


def _attn_prep_kv(kv_all, q_gain, k_gain):
    hp = 2 * HEAD_DIM
    even = lax.broadcasted_iota(jnp.int32, (1, hp), 1) < HEAD_DIM
    ri = lax.broadcasted_iota(jnp.int32, (hp, hp), 0)
    ci = lax.broadcasted_iota(jnp.int32, (hp, hp), 1)
    ones_bd = ((ri < HEAD_DIM) == (ci < HEAD_DIM)).astype(BF16)
    k = kv_all[:, :hp]
    kn = k * lax.rsqrt(_head_sumsq(k, ones_bd) * (1.0 / HEAD_DIM) + RMS_EPS) * k_gain * q_gain
    kn_sw = pltpu.roll(kn, HEAD_DIM, 1)
    kdup = [jnp.where(even, kn, kn_sw).astype(BF16), jnp.where(even, kn_sw, kn).astype(BF16)]
    v_t = kv_all[:, hp:].T.astype(BF16)
    return kdup, v_t


def _attn_prep_q(q):
    hp = 2 * HEAD_DIM
    si = lax.broadcasted_iota(jnp.int32, (16, hp), 0)
    sj = lax.broadcasted_iota(jnp.int32, (16, hp), 1)
    sel = (((si == 0) & (sj < HEAD_DIM)) | ((si == 1) & (sj >= HEAD_DIM))).astype(BF16)
    q2 = q * q
    hi = q2.astype(BF16)
    lo = (q2 - hi.astype(F32)).astype(BF16)
    inv_rms = []
    for m in range(ATTN_Q_WIDTH // hp):
        cs = slice(m * hp, (m + 1) * hp)
        ss = (lax.dot_general(sel, hi[:, cs], _NT, preferred_element_type=F32)
              + lax.dot_general(sel, lo[:, cs], _NT, preferred_element_type=F32))
        inv_rms.append(lax.rsqrt(ss * (1.0 / HEAD_DIM) + RMS_EPS))
    return q.astype(BF16), inv_rms


def _attn_scores(j, kh, qb, kdup, inv_rms, no_prev):
    hp = 2 * HEAD_DIM
    even = lax.broadcasted_iota(jnp.int32, (1, hp), 1) < HEAD_DIM
    zero_q = jnp.zeros((BLOCK, hp), BF16)
    parts, scale = [], []
    for g in range(ATTN_GROUP):
        h = kh * ATTN_GROUP + g
        col = qb[j * BLOCK:(j + 1) * BLOCK, (h // 2) * hp:(h // 2 + 1) * hp]
        parts.append(jnp.where(even, col, zero_q) if h % 2 == 0 else jnp.where(even, zero_q, col))
        scale.append(inv_rms[h // 2][h % 2:h % 2 + 1, j * BLOCK:(j + 1) * BLOCK])
    s_t = lax.dot_general(kdup[kh][j * BLOCK:(j + 2) * BLOCK], jnp.concatenate(parts, axis=0), _NT,
                          preferred_element_type=F32)
    key = lax.broadcasted_iota(jnp.int32, (BLOCK, ATTN_GROUP * BLOCK), 0)
    qry = lax.broadcasted_iota(jnp.int32, (BLOCK, ATTN_GROUP * BLOCK), 1) & (BLOCK - 1)
    use_prev = key > qry
    prev = s_t[:BLOCK]
    if no_prev is not None:
        prev = jnp.where(no_prev, -jnp.inf, prev)
    folded = jnp.where(use_prev, prev, s_t[BLOCK:]) * jnp.concatenate(scale, axis=1)
    return folded, use_prev


def _attn_softmax(kh, folded, use_prev, sink_ref):
    sink = jnp.concatenate(
        [jnp.full((1, BLOCK), sink_ref[kh * ATTN_GROUP + g], F32) for g in range(ATTN_GROUP)], axis=1)
    m = jnp.maximum(jnp.max(folded, axis=0, keepdims=True), sink)
    p = jnp.exp(folded - m)
    den = jnp.sum(p, axis=0, keepdims=True) + jnp.exp(sink - m)
    zero = jnp.zeros_like(p)
    p_t = jnp.concatenate([jnp.where(use_prev, p, zero), jnp.where(use_prev, zero, p)], axis=0)
    return p_t.astype(BF16), 1.0 / den


def _attn_values(j, kh, p_t, inv_den, v_t, o_ref):
    o_t = jnp.dot(v_t[kh * HEAD_DIM:(kh + 1) * HEAD_DIM, j * BLOCK:(j + 2) * BLOCK], p_t,
                  preferred_element_type=F32) * inv_den
    for g in range(ATTN_GROUP):
        h = kh * ATTN_GROUP + g
        o_ref[h * HEAD_DIM:(h + 1) * HEAD_DIM, j * BLOCK:(j + 1) * BLOCK] = (
            o_t[:, g * BLOCK:(g + 1) * BLOCK].astype(o_ref.dtype))


def _pair_blockdiag(x, even):
    zero = jnp.zeros_like(x)
    return jnp.concatenate([jnp.where(even, x, zero), jnp.where(even, zero, x)], axis=0)


def _pair_sum(x, even):
    s_e = jnp.sum(jnp.where(even, x, 0.0), axis=-1, keepdims=True)
    s_o = jnp.sum(jnp.where(even, 0.0, x), axis=-1, keepdims=True)
    return jnp.where(even, s_e, s_o)


def _rwkv_prep(xs, prm):
    c = CHUNK
    hp = 2 * RWKV_HEAD_SIZE
    w = RWKV_WIDTH
    rows = xs.shape[0]
    nc = rows // c
    cols = [slice(p * hp, (p + 1) * hp) for p in range(w // hp)]
    r = xs[:, 0:w]
    k = xs[:, w:2 * w]
    v = xs[:, 2 * w:3 * w]
    w_lora = xs[:, 3 * w:3 * w + LORA_DECAY]
    a_lora = xs[:, 3 * w + LORA_DECAY:3 * w + LORA_DECAY + LORA_AAA]
    g_lora = xs[:, 3 * w + LORA_DECAY + LORA_AAA:]

    d = prm["dbias"][...] + _bdot(jnp.tanh(w_lora), prm["dup"][...])
    lw = (-math.exp(-0.5)) * _sigmoid(d)
    a_sig = _sigmoid(prm["abias"][...] + _bdot(a_lora, prm["aup"][...]))
    gate = _bdot(_sigmoid(g_lora), prm["gup"][...])
    yield

    kkr = k * prm["kk"][...]
    ka = prm["ka"][...]
    k2 = k * (a_sig * ka + (1.0 - ka))
    rk = r * k2 * prm["rk"][...]
    even_t = lax.broadcasted_iota(jnp.int32, (rows, hp), 1) < RWKV_HEAD_SIZE
    den = jnp.concatenate(
        [jnp.maximum(jnp.sqrt(_pair_sum(kkr[:, cs] * kkr[:, cs], even_t)), L2_EPS) for cs in cols],
        axis=1)
    bonus = jnp.concatenate([_pair_sum(rk[:, cs], even_t) for cs in cols], axis=1) * v
    kk = kkr / den
    a_s = -kk
    b_s = kk * a_sig
    yield

    ti = lax.broadcasted_iota(jnp.int32, (c, 3 * c), 0)
    si = lax.broadcasted_iota(jnp.int32, (c, 3 * c), 1) & (c - 1)
    lower3 = (ti >= si).astype(BF16)
    lw_hi = lw.astype(BF16)
    lw_r = lw - lw_hi.astype(F32)
    lw_mid = lw_r.astype(BF16)
    lw_lo = (lw_r - lw_mid.astype(F32)).astype(BF16)
    cl = jnp.concatenate(
        [jnp.dot(lower3, jnp.concatenate([z[ci * c:(ci + 1) * c] for z in (lw_hi, lw_mid, lw_lo)], axis=0),
                 preferred_element_type=F32) for ci in range(nc)], axis=0)
    yield

    e_c = jnp.exp(cl)
    e_neg = 1.0 / e_c
    g_rows = [e_c[ci * c + c - 1:ci * c + c, :] for ci in range(nc)]
    g_full = jnp.concatenate([jnp.broadcast_to(g, (c, w)) for g in g_rows], axis=0)
    at = a_s * jnp.exp(cl - lw)
    rt = r * e_c
    yield
    bt = b_s * e_neg
    kt = k2 * e_neg
    ops = dict(rt=rt, g_rows=g_rows, gate=gate, bonus=bonus,
               bh=(bt * g_full).astype(BF16), khat=(kt * g_full).astype(BF16),
               at_b=at.astype(BF16), rt_b=rt.astype(BF16), bt_b=bt.astype(BF16),
               kt_b=kt.astype(BF16), v_b=v.astype(BF16))
    return ops


def _rwkv_solve(ops, prm):
    c = CHUNK
    hp = 2 * RWKV_HEAD_SIZE
    w = RWKV_WIDTH
    npair = w // hp
    nc = ops["rt"].shape[0] // c
    cols = [slice(p * hp, (p + 1) * hp) for p in range(npair)]
    even = lax.broadcasted_iota(jnp.int32, (c, hp), 1) < RWKV_HEAD_SIZE
    gi = lax.broadcasted_iota(jnp.int32, (2 * c, 2 * hp), 0)
    g_t = gi & (c - 1)
    g_s = lax.broadcasted_iota(jnp.int32, (2 * c, 2 * hp), 1) & (c - 1)
    g_mask = (g_t > g_s) | ((gi >= c) & (g_t == g_s))
    eye2 = (lax.broadcasted_iota(jnp.int32, (c, hp), 0)
            == (lax.broadcasted_iota(jnp.int32, (c, hp), 1) & (c - 1)))
    pi = lax.broadcasted_iota(jnp.int32, (hp, 2 * hp), 0)
    pj = lax.broadcasted_iota(jnp.int32, (hp, 2 * hp), 1)
    same_head = (pi < RWKV_HEAD_SIZE) == ((pj & (hp - 1)) < RWKV_HEAD_SIZE)
    eye = (lax.broadcasted_iota(jnp.int32, (hp, hp), 0)
           == lax.broadcasted_iota(jnp.int32, (hp, hp), 1))
    units = [(ci, p) for ci in range(nc) for p in range(npair)]

    def col(name, ci, p):
        return ops[name][ci * c:(ci + 1) * c, cols[p]]

    def bd(x):
        return _pair_blockdiag(x, even)

    def lanes(*xs):
        return jnp.concatenate(xs, axis=1)

    def rows(*xs):
        return jnp.concatenate(xs, axis=0)

    def mm(a, b):
        return jnp.dot(a, b, preferred_element_type=F32)

    gm = [jnp.where(g_mask,
                    lax.dot_general(rows(col("at_b", ci, p), col("rt_b", ci, p)),
                                    rows(bd(col("bt_b", ci, p)), bd(col("kt_b", ci, p))), _NT,
                                    preferred_element_type=F32), 0.0)
          for ci, p in units]
    yield
    a_pow = [g[:c, :hp].astype(BF16) for g in gm]
    t_inv = [jnp.where(eye2, 1.0, g[:c, :hp]) for g in gm]
    a_pow = [mm(a, bd(a)).astype(BF16) for a in a_pow]
    v_st = [col("v_b", ci, p) for ci, p in units]
    akv = [mm(g[:c, hp:].astype(BF16), bd(v)).astype(BF16) for g, v in zip(gm, v_st)]
    yield
    for _ in range(int(math.log2(c)) - 2):
        res = [mm(a, lanes(bd(t.astype(BF16)), bd(a))) for a, t in zip(a_pow, t_inv)]
        t_inv = [t + z[:, :hp] for t, z in zip(t_inv, res)]
        a_pow = [z[:, hp:].astype(BF16) for z in res]
        yield
    t_inv = [t + mm(a, bd(t.astype(BF16))) for a, t in zip(a_pow, t_inv)]
    yield
    wu = [mm(t.astype(BF16), lanes(bd(col("at_b", ci, p)), bd(x)))
          for (ci, p), t, x in zip(units, t_inv, akv)]
    w_b = [z[:, :hp].astype(BF16) for z in wu]
    u0_b = [z[:, hp:].astype(BF16) for z in wu]
    yield
    ab_y = [mm(g[c:, :].astype(BF16),
               rows(lanes(bd(wb), bd(ub)), lanes(jnp.zeros((hp, hp), BF16), bd(v))))
            for g, wb, ub, v in zip(gm, w_b, u0_b, v_st)]
    yield
    ab_h = [jnp.where(same_head,
                      lax.dot_general(rows(col("bh", ci, p), col("khat", ci, p)),
                                      rows(lanes(wb, ub), lanes(jnp.zeros((c, hp), BF16), v)), _TN,
                                      preferred_element_type=F32), 0.0)
            for (ci, p), wb, ub, v in zip(units, w_b, u0_b, v_st)]
    yield

    def tail(state, out_rows):
        for ci in range(nc):
            y_cols = []
            for p in range(npair):
                idx = ci * npair + p
                qmat = col("rt", ci, p) + ab_y[idx][:, :hp]
                g_diag = jnp.where(eye, jnp.broadcast_to(ops["g_rows"][ci][:, cols[p]], (hp, hp)), 0.0)
                mmat = g_diag + ab_h[idx][:, :hp]
                ys = mm(rows(qmat, mmat).astype(BF16), state[p].astype(BF16))
                y = ys[:c] + ab_y[idx][:, hp:]
                state[p] = ys[c:] + ab_h[idx][:, hp:]
                mean = _pair_sum(y, even) * (1.0 / RWKV_HEAD_SIZE)
                yc = y - mean
                var = _pair_sum(yc * yc, even) * (1.0 / RWKV_HEAD_SIZE)
                y_cols.append(yc * lax.rsqrt(var + GN_EPS))
            yn = jnp.concatenate(y_cols, axis=1)
            rs = slice(ci * c, (ci + 1) * c)
            out_rows(ci, (yn * prm["lng"][...] + prm["lnb"][...] + ops["bonus"][rs]) * ops["gate"][rs])
            yield

    return tail


def _round_robin(gens):
    values = [None] * len(gens)
    live = list(range(len(gens)))
    while live:
        for i in list(live):
            try:
                next(gens[i])
            except StopIteration as stop:
                values[i] = stop.value
                live.remove(i)
    return values


def _rwkv_kernel(xs_ref, dbias_ref, dup_ref, abias_ref, aup_ref, gup_ref,
                 kk_ref, ka_ref, rk_ref, lng_ref, lnb_ref, o_ref, state_ref):
    hp = 2 * RWKV_HEAD_SIZE
    tb = xs_ref.shape[1]
    gr = min(RWKV_GROUP, tb)
    ngroup = tb // gr
    prm = dict(dbias=dbias_ref, dup=dup_ref, abias=abias_ref, aup=aup_ref, gup=gup_ref,
               kk=kk_ref, ka=ka_ref, rk=rk_ref, lng=lng_ref, lnb=lnb_ref)

    @pl.when(pl.program_id(1) == 0)
    def _():
        state_ref[...] = jnp.zeros_like(state_ref)

    def prep(g):
        return _rwkv_prep(xs_ref[0, g * gr:(g + 1) * gr, :], prm)

    state = [state_ref[:, p * hp:(p + 1) * hp] for p in range(RWKV_WIDTH // hp)]
    (ops,) = _round_robin([prep(0)])
    done = _round_robin([_rwkv_solve(ops, prm)] + ([prep(1)] if ngroup > 1 else []))
    tail, ops = done[0], (done[1] if ngroup > 1 else None)
    for g in range(ngroup):
        def out_rows(ci, value, g=g):
            r0 = g * gr + ci * CHUNK
            o_ref[0, r0:r0 + CHUNK, :] = value.astype(o_ref.dtype)

        gens = [tail(state, out_rows)]
        if g + 1 < ngroup:
            gens.append(_rwkv_solve(ops, prm))
        if g + 2 < ngroup:
            gens.append(prep(g + 2))
        done = _round_robin(gens)
        tail = done[1] if g + 1 < ngroup else None
        ops = done[2] if g + 2 < ngroup else None
    for p in range(RWKV_WIDTH // hp):
        state_ref[:, p * hp:(p + 1) * hp] = state[p]


def _rwkv(rw, decay_bias, decay_up, aaa_bias, aaa_up, gate_up, k_k, k_a, r_k, ln_g, ln_b):
    b, t, _ = rw.shape
    tb = min(RWKV_TILE, t)
    w = RWKV_WIDTH

    def full(shape):
        return pl.BlockSpec(shape, lambda bi, ti: (0, 0))

    return pl.pallas_call(
        _rwkv_kernel,
        grid=(b, t // tb),
        in_specs=[
            pl.BlockSpec((1, tb, RWKV_IN_WIDTH), lambda bi, ti: (bi, ti, 0)),
            full((1, w)), full((LORA_DECAY, w)),
            full((1, w)), full((LORA_AAA, w)),
            full((LORA_GATE, w)),
            full((1, w)), full((1, w)), full((1, w)), full((1, w)), full((1, w)),
        ],
        out_specs=pl.BlockSpec((1, tb, w), lambda bi, ti: (bi, ti, 0)),
        out_shape=jax.ShapeDtypeStruct((b, t, w), BF16),
        scratch_shapes=[pltpu.VMEM((2 * RWKV_HEAD_SIZE, w), F32)],
        compiler_params=pltpu.CompilerParams(
            dimension_semantics=("arbitrary", "arbitrary"), vmem_limit_bytes=VMEM_LIMIT),
        name="rwkv",
    )(rw, decay_bias, decay_up, aaa_bias, aaa_up, gate_up, k_k, k_a, r_k, ln_g, ln_b)


def _attn_merge_kernel(x_ref, q_ref, kv_ref, kvp_ref, rwkv_ref, g_ref, qg_ref, kg_ref, sink_ref,
                       wg_ref, wba_ref, wbr_ref, wo_ref, h_ref, attn_ref):
    x = x_ref[...]
    d = x.shape[1]
    nblk = q_ref.shape[1] // BLOCK
    units = [(j, kh) for j in range(nblk) for kh in range(ATTN_KV_HEADS)]
    n_u = len(units)
    gw = 2 * BLOCK
    n_g = (wg_ref.shape[1] - N_PROJ) // gw
    gate_parts = []

    def gate_slice():
        c0 = N_PROJ + len(gate_parts) * gw
        gate_parts.append(jnp.dot(ub, wg_ref[:, c0:c0 + gw], preferred_element_type=F32))

    y_rwkv = jnp.dot(rwkv_ref[...], wbr_ref[...], preferred_element_type=F32)
    ub = _rms(x, g_ref[...]).astype(BF16)
    gate_slice()
    kv_all = jnp.concatenate([kvp_ref[0], kv_ref[0]], axis=0)
    kdup, v_t = _attn_prep_kv(kv_all, qg_ref[...], kg_ref[...])
    gate_slice()
    qb, inv_rms = _attn_prep_q(q_ref[0])
    first_tile = pl.program_id(1) == 0
    scores = [_attn_scores(j, kh, qb, kdup, inv_rms, first_tile if j == 0 else None)
              for j, kh in units]
    soft = []
    for idx, (_, kh) in enumerate(units):
        while len(gate_parts) < 2 + (idx + 1) * (n_g - 2) // n_u:
            gate_slice()
        soft.append(_attn_softmax(kh, *scores[idx], sink_ref))
    while len(gate_parts) < n_g:
        gate_slice()
    gates = jnp.concatenate(gate_parts, axis=1)
    for unit, sm in zip(units, soft):
        _attn_values(*unit, *sm, v_t, attn_ref)
    y_attn = lax.dot_general(attn_ref[...], wba_ref[...], _TN, preferred_element_type=F32)
    mixed = _sigmoid(gates[:, :d]) * y_attn + _sigmoid(gates[:, d:]) * y_rwkv
    h_ref[...] = x + _bdot(mixed, wo_ref[...])


def _attn_merge(x2, q, kv, rwkv, gain, q_gain, k_gain, sinks, w_in, w_ba, w_br, w_out):
    n, d = x2.shape
    b, t, _ = q.shape
    tq = min(ATTN_TILE, t)
    nblk = tq // BLOCK
    nt = t // tq
    q_gain = jnp.tile(q_gain * (HEAD_DIM ** -0.5), (1, ATTN_KV_HEADS))
    k_gain = jnp.tile(k_gain, (1, ATTN_KV_HEADS))

    def full(a):
        return pl.BlockSpec(a.shape, lambda bi, i: (0, 0), pipeline_mode=pl.Buffered(1))

    def rows(width):
        return pl.BlockSpec((tq, width), lambda bi, i: (bi * nt + i, 0))

    return pl.pallas_call(
        _attn_merge_kernel,
        grid=(b, nt),
        in_specs=[
            rows(d),
            pl.BlockSpec((1, tq, ATTN_Q_WIDTH), lambda bi, i: (bi, i, 0)),
            pl.BlockSpec((1, tq, 2 * ATTN_KV_WIDTH), lambda bi, i: (bi, i, 0)),
            pl.BlockSpec((1, BLOCK, 2 * ATTN_KV_WIDTH),
                         lambda bi, i: (bi, jnp.maximum(i * nblk - 1, 0), 0)),
            rows(rwkv.shape[1]),
            full(gain), full(q_gain), full(k_gain),
            pl.BlockSpec(memory_space=pltpu.SMEM),
            full(w_in), full(w_ba), full(w_br), full(w_out),
        ],
        out_specs=rows(d),
        out_shape=jax.ShapeDtypeStruct((n, d), F32),
        scratch_shapes=[pltpu.VMEM((ATTN_Q_WIDTH, tq), BF16)],
        compiler_params=pltpu.CompilerParams(
            dimension_semantics=("arbitrary", "arbitrary"), vmem_limit_bytes=VMEM_LIMIT),
        name="attn_merge",
    )(x2, q, kv, kv, rwkv, gain, q_gain, k_gain, sinks, w_in, w_ba, w_br, w_out)


def _mlp_kernel(h_ref, g_ref, w1_ref, w2_ref, o_ref):
    h = h_ref[...]
    hidden = jnp.square(jnp.maximum(_bdot(_rms(h, g_ref[...]), w1_ref[...]), 0.0))
    o_ref[...] = h + _bdot(hidden, w2_ref[...])


def _mlp(h2, gain, w1, w2):
    n, d = h2.shape
    tm = min(ROW_TILE, n)

    def full(a):
        return pl.BlockSpec(a.shape, lambda i: (0, 0), pipeline_mode=pl.Buffered(1))

    return pl.pallas_call(
        _mlp_kernel,
        grid=(n // tm,),
        in_specs=[pl.BlockSpec((tm, d), lambda i: (i, 0)), full(gain), full(w1), full(w2)],
        out_specs=pl.BlockSpec((tm, d), lambda i: (i, 0)),
        out_shape=jax.ShapeDtypeStruct((n, d), F32),
        compiler_params=pltpu.CompilerParams(
            dimension_semantics=("arbitrary",), vmem_limit_bytes=VMEM_LIMIT),
        name="mlp",
    )(h2, gain, w1, w2)


def _layer(h, norm1_gain, w_in, q_norm_gain, k_norm_gain, attn_sinks,
           mu_r, mu_k, mu_v, mu_w, mu_a, mu_g, decay_bias, decay_up, aaa_bias, aaa_up,
           gate_up, k_k, k_a, r_k, ln_x_gain, ln_x_bias, w_branch_attn, w_branch_rwkv,
           w_out, norm2_gain, w_ff_in, w_ff_out):
    b, t, d = h.shape
    x2 = h.reshape(b * t, d)
    row = lambda a: a.reshape(1, -1).astype(F32)
    w_in_b = w_in.astype(BF16)

    mu = jnp.concatenate([mu_r, mu_k, mu_v, mu_w, mu_a, mu_g]).reshape(1, -1).astype(F32)
    q, kv, rw = _inproj(x2, row(norm1_gain), w_in_b, mu, t)
    rwkv = _rwkv(rw.reshape(b, t, -1), row(decay_bias), decay_up.astype(BF16),
                 row(aaa_bias), aaa_up.astype(BF16), gate_up.astype(BF16),
                 row(k_k), row(k_a), row(r_k), row(ln_x_gain), row(ln_x_bias))
    h2 = _attn_merge(x2, q.reshape(b, t, -1), kv.reshape(b, t, -1), rwkv.reshape(b * t, -1),
                     row(norm1_gain), row(q_norm_gain), row(k_norm_gain), attn_sinks.astype(F32),
                     w_in_b, w_branch_attn.astype(BF16), w_branch_rwkv.astype(BF16),
                     w_out.astype(BF16))
    out = _mlp(h2, row(norm2_gain), w_ff_in.astype(BF16), w_ff_out.astype(BF16))
    return out.reshape(b, t, d)


def kernel(x, norm1_gain, w_in, q_norm_gain, k_norm_gain, attn_sinks, mu_r, mu_k, mu_v, mu_w, mu_a, mu_g, decay_bias, decay_up, aaa_bias, aaa_up, gate_up, k_k, k_a, r_k, ln_x_gain, ln_x_bias, w_branch_attn, w_branch_rwkv, w_out, norm2_gain, w_ff_in, w_ff_out):
    h = x.astype(F32)
    params = (norm1_gain, w_in, q_norm_gain, k_norm_gain, attn_sinks, mu_r, mu_k, mu_v, mu_w,
              mu_a, mu_g, decay_bias, decay_up, aaa_bias, aaa_up, gate_up, k_k, k_a, r_k,
              ln_x_gain, ln_x_bias, w_branch_attn, w_branch_rwkv, w_out, norm2_gain,
              w_ff_in, w_ff_out)
    for l in range(norm1_gain.shape[0]):
        h = _layer(h, *(p[l] for p in params))
    return h.astype(x.dtype)
```

```python
import functools
import math

import jax
import jax.numpy as jnp
from jax import lax
from jax.experimental import pallas as pl
from jax.experimental.pallas import tpu as pltpu

F32 = jnp.float32
BF16 = jnp.bfloat16

HEAD_DIM = 64
ATTN_Q_HEADS = 8
ATTN_KV_HEADS = 2
ATTN_GROUP = ATTN_Q_HEADS // ATTN_KV_HEADS
WINDOW = 128
BLOCK = 128
RWKV_HEADS = 8
RWKV_HEAD_SIZE = 64
LORA_DECAY = 64
LORA_AAA = 64
LORA_GATE = 128
ATTN_Q_WIDTH = ATTN_Q_HEADS * HEAD_DIM
ATTN_KV_WIDTH = ATTN_KV_HEADS * HEAD_DIM
RWKV_WIDTH = RWKV_HEADS * RWKV_HEAD_SIZE
RWKV_IN_WIDTH = 3 * RWKV_WIDTH + LORA_DECAY + LORA_AAA + LORA_GATE
N_PROJ = ATTN_Q_WIDTH + 2 * ATTN_KV_WIDTH + RWKV_IN_WIDTH
RMS_EPS = 1e-6
GN_EPS = 64e-5
L2_EPS = 1e-12

CHUNK = 64
ROW_TILE = 512
ATTN_TILE = 1024
RWKV_TILE = 1024
RWKV_GROUP = 256
VMEM_LIMIT = 56 * 1024 * 1024

_NT = (((1,), (1,)), ((), ()))
_TN = (((0,), (0,)), ((), ()))


def _bdot(a, b):
    return jnp.dot(a.astype(BF16), b.astype(BF16), preferred_element_type=F32)


def _rms(x, gain):
    return x * lax.rsqrt(jnp.mean(x * x, axis=-1, keepdims=True) + RMS_EPS) * gain


def _sigmoid(x):
    return 0.5 * jnp.tanh(0.5 * x) + 0.5


def _inproj_kernel(x_ref, g_ref, w_ref, mu_ref, q_ref, kv_ref, rw_ref, last_ref, *, tiles_per_seq):
    u = _rms(x_ref[...], g_ref[...])
    p = _bdot(u, w_ref[...])
    q_ref[...] = p[:, :ATTN_Q_WIDTH]
    kv_ref[...] = p[:, ATTN_Q_WIDTH:ATTN_Q_WIDTH + 2 * ATTN_KV_WIDTH]
    rw = p[:, ATTN_Q_WIDTH + 2 * ATTN_KV_WIDTH:]
    tm = rw.shape[0]
    seq_start = pl.program_id(0) % tiles_per_seq == 0
    before = jnp.where(seq_start, 0.0, last_ref[...])
    first = lax.broadcasted_iota(jnp.int32, (tm, 1), 0) == 0
    prev = jnp.where(first, before, pltpu.roll(rw, 1, 0))
    last_ref[...] = rw[tm - 1:tm, :]
    rw_ref[...] = rw + (prev - rw) * mu_ref[...]


def _inproj(x2, gain, w, mu, seq_len):
    n, d = x2.shape
    wid = N_PROJ
    tm = min(2 * ROW_TILE, seq_len)

    def full(shape):
        return pl.BlockSpec(shape, lambda i: (0, 0), pipeline_mode=pl.Buffered(1))

    return pl.pallas_call(
        functools.partial(_inproj_kernel, tiles_per_seq=seq_len // tm),
        grid=(n // tm,),
        in_specs=[
            pl.BlockSpec((tm, d), lambda i: (i, 0)),
            full((1, d)), full((d, wid)), full((1, RWKV_IN_WIDTH)),
        ],
        out_specs=[
            pl.BlockSpec((tm, ATTN_Q_WIDTH), lambda i: (i, 0)),
            pl.BlockSpec((tm, 2 * ATTN_KV_WIDTH), lambda i: (i, 0)),
            pl.BlockSpec((tm, RWKV_IN_WIDTH), lambda i: (i, 0)),
        ],
        out_shape=[
            jax.ShapeDtypeStruct((n, ATTN_Q_WIDTH), F32),
            jax.ShapeDtypeStruct((n, 2 * ATTN_KV_WIDTH), F32),
            jax.ShapeDtypeStruct((n, RWKV_IN_WIDTH), F32),
        ],
        scratch_shapes=[pltpu.VMEM((1, RWKV_IN_WIDTH), F32)],
        compiler_params=pltpu.CompilerParams(
            dimension_semantics=("arbitrary",), vmem_limit_bytes=VMEM_LIMIT),
        name="inproj",
    )(x2, gain, w, mu)


def _head_sumsq(x, ones_bd):
    x2 = x * x
    hi = x2.astype(BF16)
    lo = (x2 - hi.astype(F32)).astype(BF16)
    return (jnp.dot(hi, ones_bd, preferred_element_type=F32)
            + jnp.dot(lo, ones_bd, preferred_element_type=F32))


def _attn_prep_kv(kv_all, q_gain, k_gain):
    hp = 2 * HEAD_DIM
    even = lax.broadcasted_iota(jnp.int32, (1, hp), 1) < HEAD_DIM
    ri = lax.broadcasted_iota(jnp.int32, (hp, hp), 0)
    ci = lax.broadcasted_iota(jnp.int32, (hp, hp), 1)
    ones_bd = ((ri < HEAD_DIM) == (ci < HEAD_DIM)).astype(BF16)
    k = kv_all[:, :hp]
    kn = k * lax.rsqrt(_head_sumsq(k, ones_bd) * (1.0 / HEAD_DIM) + RMS_EPS) * k_gain * q_gain
    kn_sw = pltpu.roll(kn, HEAD_DIM, 1)
    kdup = [jnp.where(even, kn, kn_sw).astype(BF16), jnp.where(even, kn_sw, kn).astype(BF16)]
    v_t = kv_all[:, hp:].T.astype(BF16)
    return kdup, v_t


def _attn_prep_q(q):
    hp = 2 * HEAD_DIM
    si = lax.broadcasted_iota(jnp.int32, (16, hp), 0)
    sj = lax.broadcasted_iota(jnp.int32, (16, hp), 1)
    sel = (((si == 0) & (sj < HEAD_DIM)) | ((si == 1) & (sj >= HEAD_DIM))).astype(BF16)
    q2 = q * q
    hi = q2.astype(BF16)
    lo = (q2 - hi.astype(F32)).astype(BF16)
    inv_rms = []
    for m in range(ATTN_Q_WIDTH // hp):
        cs = slice(m * hp, (m + 1) * hp)
        ss = (lax.dot_general(sel, hi[:, cs], _NT, preferred_element_type=F32)
              + lax.dot_general(sel, lo[:, cs], _NT, preferred_element_type=F32))
        inv_rms.append(lax.rsqrt(ss * (1.0 / HEAD_DIM) + RMS_EPS))
    return q.astype(BF16), inv_rms


def _attn_scores(j, kh, qb, kdup, inv_rms, no_prev):
    hp = 2 * HEAD_DIM
    even = lax.broadcasted_iota(jnp.int32, (1, hp), 1) < HEAD_DIM
    zero_q = jnp.zeros((BLOCK, hp), BF16)
    parts, scale = [], []
    for g in range(ATTN_GROUP):
        h = kh * ATTN_GROUP + g
        col = qb[j * BLOCK:(j + 1) * BLOCK, (h // 2) * hp:(h // 2 + 1) * hp]
        parts.append(jnp.where(even, col, zero_q) if h % 2 == 0 else jnp.where(even, zero_q, col))
        scale.append(inv_rms[h // 2][h % 2:h % 2 + 1, j * BLOCK:(j + 1) * BLOCK])
    s_t = lax.dot_general(kdup[kh][j * BLOCK:(j + 2) * BLOCK], jnp.concatenate(parts, axis=0), _NT,
                          preferred_element_type=F32)
    key = lax.broadcasted_iota(jnp.int32, (BLOCK, ATTN_GROUP * BLOCK), 0)
    qry = lax.broadcasted_iota(jnp.int32, (BLOCK, ATTN_GROUP * BLOCK), 1) & (BLOCK - 1)
    use_prev = key > qry
    prev = s_t[:BLOCK]
    if no_prev is not None:
        prev = jnp.where(no_prev, -jnp.inf, prev)
    folded = jnp.where(use_prev, prev, s_t[BLOCK:]) * jnp.concatenate(scale, axis=1)
    return folded, use_prev


def _attn_softmax(kh, folded, use_prev, sink_ref):
    sink = jnp.concatenate(
        [jnp.full((1, BLOCK), sink_ref[kh * ATTN_GROUP + g], F32) for g in range(ATTN_GROUP)], axis=1)
    m = jnp.maximum(jnp.max(folded, axis=0, keepdims=True), sink)
    p = jnp.exp(folded - m)
    den = jnp.sum(p, axis=0, keepdims=True) + jnp.exp(sink - m)
    zero = jnp.zeros_like(p)
    p_t = jnp.concatenate([jnp.where(use_prev, p, zero), jnp.where(use_prev, zero, p)], axis=0)
    return p_t.astype(BF16), 1.0 / den


def _attn_values(j, kh, p_t, inv_den, v_t, o_ref):
    o_t = jnp.dot(v_t[kh * HEAD_DIM:(kh + 1) * HEAD_DIM, j * BLOCK:(j + 2) * BLOCK], p_t,
                  preferred_element_type=F32) * inv_den
    for g in range(ATTN_GROUP):
        h = kh * ATTN_GROUP + g
        o_ref[h * HEAD_DIM:(h + 1) * HEAD_DIM, j * BLOCK:(j + 1) * BLOCK] = (
            o_t[:, g * BLOCK:(g + 1) * BLOCK].astype(o_ref.dtype))


def _pair_blockdiag(x, even):
    zero = jnp.zeros_like(x)
    return jnp.concatenate([jnp.where(even, x, zero), jnp.where(even, zero, x)], axis=0)


def _pair_sum(x, even):
    s_e = jnp.sum(jnp.where(even, x, 0.0), axis=-1, keepdims=True)
    s_o = jnp.sum(jnp.where(even, 0.0, x), axis=-1, keepdims=True)
    return jnp.where(even, s_e, s_o)


def _rwkv_prep(xs, prm):
    c = CHUNK
    hp = 2 * RWKV_HEAD_SIZE
    w = RWKV_WIDTH
    rows = xs.shape[0]
    nc = rows // c
    cols = [slice(p * hp, (p + 1) * hp) for p in range(w // hp)]
    r = xs[:, 0:w]
    k = xs[:, w:2 * w]
    v = xs[:, 2 * w:3 * w]
    w_lora = xs[:, 3 * w:3 * w + LORA_DECAY]
    a_lora = xs[:, 3 * w + LORA_DECAY:3 * w + LORA_DECAY + LORA_AAA]
    g_lora = xs[:, 3 * w + LORA_DECAY + LORA_AAA:]

    d = prm["dbias"][...] + _bdot(jnp.tanh(w_lora), prm["dup"][...])
    lw = (-math.exp(-0.5)) * _sigmoid(d)
    a_sig = _sigmoid(prm["abias"][...] + _bdot(a_lora, prm["aup"][...]))
    gate = _bdot(_sigmoid(g_lora), prm["gup"][...])
    yield

    kkr = k * prm["kk"][...]
    ka = prm["ka"][...]
    k2 = k * (a_sig * ka + (1.0 - ka))
    rk = r * k2 * prm["rk"][...]
    even_t = lax.broadcasted_iota(jnp.int32, (rows, hp), 1) < RWKV_HEAD_SIZE
    den = jnp.concatenate(
        [jnp.maximum(jnp.sqrt(_pair_sum(kkr[:, cs] * kkr[:, cs], even_t)), L2_EPS) for cs in cols],
        axis=1)
    bonus = jnp.concatenate([_pair_sum(rk[:, cs], even_t) for cs in cols], axis=1) * v
    kk = kkr / den
    a_s = -kk
    b_s = kk * a_sig
    yield

    ti = lax.broadcasted_iota(jnp.int32, (c, 3 * c), 0)
    si = lax.broadcasted_iota(jnp.int32, (c, 3 * c), 1) & (c - 1)
    lower3 = (ti >= si).astype(BF16)
    lw_hi = lw.astype(BF16)
    lw_r = lw - lw_hi.astype(F32)
    lw_mid = lw_r.astype(BF16)
    lw_lo = (lw_r - lw_mid.astype(F32)).astype(BF16)
    cl = jnp.concatenate(
        [jnp.dot(lower3, jnp.concatenate([z[ci * c:(ci + 1) * c] for z in (lw_hi, lw_mid, lw_lo)], axis=0),
                 preferred_element_type=F32) for ci in range(nc)], axis=0)
    yield

    e_c = jnp.exp(cl)
    e_neg = 1.0 / e_c
    g_rows = [e_c[ci * c + c - 1:ci * c + c, :] for ci in range(nc)]
    g_full = jnp.concatenate([jnp.broadcast_to(g, (c, w)) for g in g_rows], axis=0)
    at = a_s * jnp.exp(cl - lw)
    rt = r * e_c
    yield
    bt = b_s * e_neg
    kt = k2 * e_neg
    ops = dict(rt=rt, g_rows=g_rows, gate=gate, bonus=bonus,
               bh=(bt * g_full).astype(BF16), khat=(kt * g_full).astype(BF16),
               at_b=at.astype(BF16), rt_b=rt.astype(BF16), bt_b=bt.astype(BF16),
               kt_b=kt.astype(BF16), v_b=v.astype(BF16))
    return ops


def _rwkv_solve(ops, prm):
    c = CHUNK
    hs = RWKV_HEAD_SIZE
    assert c == hs
    hp = 2 * hs
    w = RWKV_WIDTH
    npair = w // hp
    nc = ops["rt"].shape[0] // c
    cols = [slice(p * hp, (p + 1) * hp) for p in range(npair)]
    even = lax.broadcasted_iota(jnp.int32, (c, hp), 1) < RWKV_HEAD_SIZE
    gi = lax.broadcasted_iota(jnp.int32, (2 * c, 2 * hp), 0)
    g_t = gi & (c - 1)
    g_s = lax.broadcasted_iota(jnp.int32, (2 * c, 2 * hp), 1) & (c - 1)
    g_mask = (g_t > g_s) | ((gi >= c) & (g_t == g_s))
    eye2 = (lax.broadcasted_iota(jnp.int32, (c, hp), 0)
            == (lax.broadcasted_iota(jnp.int32, (c, hp), 1) & (c - 1)))
    pi = lax.broadcasted_iota(jnp.int32, (hp, 2 * hp), 0)
    pj = lax.broadcasted_iota(jnp.int32, (hp, 2 * hp), 1)
    same_head = (pi < RWKV_HEAD_SIZE) == ((pj & (hp - 1)) < RWKV_HEAD_SIZE)
    units = [(ci, p) for ci in range(nc) for p in range(npair)]

    def col(name, ci, p):
        return ops[name][ci * c:(ci + 1) * c, cols[p]]

    def bd(x):
        return _pair_blockdiag(x, even)

    def lanes(*xs):
        return jnp.concatenate(xs, axis=1)

    def rows(*xs):
        return jnp.concatenate(xs, axis=0)

    def mm(a, b):
        return jnp.dot(a, b, preferred_element_type=F32)

    gm = [jnp.where(g_mask,
                    lax.dot_general(rows(col("at_b", ci, p), col("rt_b", ci, p)),
                                    rows(bd(col("bt_b", ci, p)), bd(col("kt_b", ci, p))), _NT,
                                    preferred_element_type=F32), 0.0)
          for ci, p in units]
    yield
    a_pow = [g[:c, :hp].astype(BF16) for g in gm]
    t_inv = [jnp.where(eye2, 1.0, g[:c, :hp]) for g in gm]
    a_pow = [mm(a, bd(a)).astype(BF16) for a in a_pow]
    v_st = [col("v_b", ci, p) for ci, p in units]
    akv = [mm(g[:c, hp:].astype(BF16), bd(v)).astype(BF16) for g, v in zip(gm, v_st)]
    yield
    for _ in range(int(math.log2(c)) - 2):
        res = [mm(a, lanes(bd(t.astype(BF16)), bd(a))) for a, t in zip(a_pow, t_inv)]
        t_inv = [t + z[:, :hp] for t, z in zip(t_inv, res)]
        a_pow = [z[:, hp:].astype(BF16) for z in res]
        yield
    t_inv = [t + mm(a, bd(t.astype(BF16))) for a, t in zip(a_pow, t_inv)]
    yield
    wu = [mm(t.astype(BF16), lanes(bd(col("at_b", ci, p)), bd(x)))
          for (ci, p), t, x in zip(units, t_inv, akv)]
    w_b = [z[:, :hp].astype(BF16) for z in wu]
    u0_b = [z[:, hp:].astype(BF16) for z in wu]
    yield
    ab_y = [mm(g[c:, :].astype(BF16),
               rows(lanes(bd(wb), bd(ub)), lanes(jnp.zeros((hp, hp), BF16), bd(v))))
            for g, wb, ub, v in zip(gm, w_b, u0_b, v_st)]
    yield
    ab_h = [jnp.where(same_head,
                      lax.dot_general(rows(col("bh", ci, p), col("khat", ci, p)),
                                      rows(lanes(wb, ub), lanes(jnp.zeros((c, hp), BF16), v)), _TN,
                                      preferred_element_type=F32), 0.0)
            for (ci, p), wb, ub, v in zip(units, w_b, u0_b, v_st)]
    ab_h = [z[:hs] + z[hs:] for z in ab_h]
    yield

    def tail(state, out_rows):
        for ci in range(nc):
            y_cols = []
            for p in range(npair):
                idx = ci * npair + p
                qmat = col("rt", ci, p) + ab_y[idx][:, :hp]
                g_diag = jnp.where(eye2, jnp.broadcast_to(ops["g_rows"][ci][:, cols[p]], (hs, hp)), 0.0)
                mmat = g_diag + ab_h[idx][:, :hp]
                ys = mm(rows(qmat, mmat).astype(BF16), bd(state[p].astype(BF16)))
                y = ys[:c] + ab_y[idx][:, hp:]
                state[p] = ys[c:] + ab_h[idx][:, hp:]
                mean = _pair_sum(y, even) * (1.0 / RWKV_HEAD_SIZE)
                yc = y - mean
                var = _pair_sum(yc * yc, even) * (1.0 / RWKV_HEAD_SIZE)
                y_cols.append(yc * lax.rsqrt(var + GN_EPS))
            yn = jnp.concatenate(y_cols, axis=1)
            rs = slice(ci * c, (ci + 1) * c)
            out_rows(ci, (yn * prm["lng"][...] + prm["lnb"][...] + ops["bonus"][rs]) * ops["gate"][rs])
            yield

    return tail


def _round_robin(gens):
    values = [None] * len(gens)
    live = list(range(len(gens)))
    while live:
        for i in list(live):
            try:
                next(gens[i])
            except StopIteration as stop:
                values[i] = stop.value
                live.remove(i)
    return values


def _rwkv_kernel(xs_ref, dbias_ref, dup_ref, abias_ref, aup_ref, gup_ref,
                 kk_ref, ka_ref, rk_ref, lng_ref, lnb_ref, o_ref, state_ref):
    hp = 2 * RWKV_HEAD_SIZE
    tb = xs_ref.shape[1]
    gr = min(RWKV_GROUP, tb)
    ngroup = tb // gr
    prm = dict(dbias=dbias_ref, dup=dup_ref, abias=abias_ref, aup=aup_ref, gup=gup_ref,
               kk=kk_ref, ka=ka_ref, rk=rk_ref, lng=lng_ref, lnb=lnb_ref)

    @pl.when(pl.program_id(1) == 0)
    def _():
        state_ref[...] = jnp.zeros_like(state_ref)

    def prep(g):
        return _rwkv_prep(xs_ref[0, g * gr:(g + 1) * gr, :], prm)

    state = [state_ref[:, p * hp:(p + 1) * hp] for p in range(RWKV_WIDTH // hp)]
    (ops,) = _round_robin([prep(0)])
    done = _round_robin([_rwkv_solve(ops, prm)] + ([prep(1)] if ngroup > 1 else []))
    tail, ops = done[0], (done[1] if ngroup > 1 else None)
    for g in range(ngroup):
        def out_rows(ci, value, g=g):
            r0 = g * gr + ci * CHUNK
            o_ref[0, r0:r0 + CHUNK, :] = value.astype(o_ref.dtype)

        gens = [tail(state, out_rows)]
        if g + 1 < ngroup:
            gens.append(_rwkv_solve(ops, prm))
        if g + 2 < ngroup:
            gens.append(prep(g + 2))
        done = _round_robin(gens)
        tail = done[1] if g + 1 < ngroup else None
        ops = done[2] if g + 2 < ngroup else None
    for p in range(RWKV_WIDTH // hp):
        state_ref[:, p * hp:(p + 1) * hp] = state[p]


def _rwkv(rw, decay_bias, decay_up, aaa_bias, aaa_up, gate_up, k_k, k_a, r_k, ln_g, ln_b):
    b, t, _ = rw.shape
    tb = min(RWKV_TILE, t)
    w = RWKV_WIDTH

    def full(shape):
        return pl.BlockSpec(shape, lambda bi, ti: (0, 0))

    return pl.pallas_call(
        _rwkv_kernel,
        grid=(b, t // tb),
        in_specs=[
            pl.BlockSpec((1, tb, RWKV_IN_WIDTH), lambda bi, ti: (bi, ti, 0)),
            full((1, w)), full((LORA_DECAY, w)),
            full((1, w)), full((LORA_AAA, w)),
            full((LORA_GATE, w)),
            full((1, w)), full((1, w)), full((1, w)), full((1, w)), full((1, w)),
        ],
        out_specs=pl.BlockSpec((1, tb, w), lambda bi, ti: (bi, ti, 0)),
        out_shape=jax.ShapeDtypeStruct((b, t, w), BF16),
        scratch_shapes=[pltpu.VMEM((RWKV_HEAD_SIZE, w), F32)],
        compiler_params=pltpu.CompilerParams(
            dimension_semantics=("arbitrary", "arbitrary"), vmem_limit_bytes=VMEM_LIMIT),
        name="rwkv",
    )(rw, decay_bias, decay_up, aaa_bias, aaa_up, gate_up, k_k, k_a, r_k, ln_g, ln_b)


def _attn_merge_kernel(x_ref, q_ref, kv_ref, kvp_ref, rwkv_ref, g_ref, qg_ref, kg_ref, sink_ref,
                       wg_ref, wba_ref, wbr_ref, wo_ref, h_ref, attn_ref):
    x = x_ref[...]
    d = x.shape[1]
    nblk = q_ref.shape[1] // BLOCK
    units = [(j, kh) for j in range(nblk) for kh in range(ATTN_KV_HEADS)]
    n_u = len(units)
    gw = 2 * BLOCK
    n_g = (wg_ref.shape[1] - N_PROJ) // gw
    gate_parts = []

    def gate_slice():
        c0 = N_PROJ + len(gate_parts) * gw
        gate_parts.append(jnp.dot(ub, wg_ref[:, c0:c0 + gw], preferred_element_type=F32))

    y_rwkv = jnp.dot(rwkv_ref[...], wbr_ref[...], preferred_element_type=F32)
    ub = _rms(x, g_ref[...]).astype(BF16)
    gate_slice()
    kv_all = jnp.concatenate([kvp_ref[0], kv_ref[0]], axis=0)
    kdup, v_t = _attn_prep_kv(kv_all, qg_ref[...], kg_ref[...])
    gate_slice()
    qb, inv_rms = _attn_prep_q(q_ref[0])
    first_tile = pl.program_id(1) == 0
    scores = [_attn_scores(j, kh, qb, kdup, inv_rms, first_tile if j == 0 else None)
              for j, kh in units]
    soft = []
    for idx, (_, kh) in enumerate(units):
        while len(gate_parts) < 2 + (idx + 1) * (n_g - 2) // n_u:
            gate_slice()
        soft.append(_attn_softmax(kh, *scores[idx], sink_ref))
    while len(gate_parts) < n_g:
        gate_slice()
    gates = jnp.concatenate(gate_parts, axis=1)
    for unit, sm in zip(units, soft):
        _attn_values(*unit, *sm, v_t, attn_ref)
    y_attn = lax.dot_general(attn_ref[...], wba_ref[...], _TN, preferred_element_type=F32)
    mixed = _sigmoid(gates[:, :d]) * y_attn + _sigmoid(gates[:, d:]) * y_rwkv
    h_ref[...] = x + _bdot(mixed, wo_ref[...])


def _attn_merge(x2, q, kv, rwkv, gain, q_gain, k_gain, sinks, w_in, w_ba, w_br, w_out):
    n, d = x2.shape
    b, t, _ = q.shape
    tq = min(ATTN_TILE, t)
    nblk = tq // BLOCK
    nt = t // tq
    q_gain = jnp.tile(q_gain * (HEAD_DIM ** -0.5), (1, ATTN_KV_HEADS))
    k_gain = jnp.tile(k_gain, (1, ATTN_KV_HEADS))

    def full(a):
        return pl.BlockSpec(a.shape, lambda bi, i: (0, 0), pipeline_mode=pl.Buffered(1))

    def rows(width):
        return pl.BlockSpec((tq, width), lambda bi, i: (bi * nt + i, 0))

    return pl.pallas_call(
        _attn_merge_kernel,
        grid=(b, nt),
        in_specs=[
            rows(d),
            pl.BlockSpec((1, tq, ATTN_Q_WIDTH), lambda bi, i: (bi, i, 0)),
            pl.BlockSpec((1, tq, 2 * ATTN_KV_WIDTH), lambda bi, i: (bi, i, 0)),
            pl.BlockSpec((1, BLOCK, 2 * ATTN_KV_WIDTH),
                         lambda bi, i: (bi, jnp.maximum(i * nblk - 1, 0), 0)),
            rows(rwkv.shape[1]),
            full(gain), full(q_gain), full(k_gain),
            pl.BlockSpec(memory_space=pltpu.SMEM),
            full(w_in), full(w_ba), full(w_br), full(w_out),
        ],
        out_specs=rows(d),
        out_shape=jax.ShapeDtypeStruct((n, d), F32),
        scratch_shapes=[pltpu.VMEM((ATTN_Q_WIDTH, tq), BF16)],
        compiler_params=pltpu.CompilerParams(
            dimension_semantics=("arbitrary", "arbitrary"), vmem_limit_bytes=VMEM_LIMIT),
        name="attn_merge",
    )(x2, q, kv, kv, rwkv, gain, q_gain, k_gain, sinks, w_in, w_ba, w_br, w_out)


def _mlp_kernel(h_ref, g_ref, w1_ref, w2_ref, o_ref):
    h = h_ref[...]
    hidden = jnp.square(jnp.maximum(_bdot(_rms(h, g_ref[...]), w1_ref[...]), 0.0))
    o_ref[...] = h + _bdot(hidden, w2_ref[...])


def _mlp(h2, gain, w1, w2):
    n, d = h2.shape
    tm = min(ROW_TILE, n)

    def full(a):
        return pl.BlockSpec(a.shape, lambda i: (0, 0), pipeline_mode=pl.Buffered(1))

    return pl.pallas_call(
        _mlp_kernel,
        grid=(n // tm,),
        in_specs=[pl.BlockSpec((tm, d), lambda i: (i, 0)), full(gain), full(w1), full(w2)],
        out_specs=pl.BlockSpec((tm, d), lambda i: (i, 0)),
        out_shape=jax.ShapeDtypeStruct((n, d), F32),
        compiler_params=pltpu.CompilerParams(
            dimension_semantics=("arbitrary",), vmem_limit_bytes=VMEM_LIMIT),
        name="mlp",
    )(h2, gain, w1, w2)


def _layer(h, norm1_gain, w_in, q_norm_gain, k_norm_gain, attn_sinks,
           mu_r, mu_k, mu_v, mu_w, mu_a, mu_g, decay_bias, decay_up, aaa_bias, aaa_up,
           gate_up, k_k, k_a, r_k, ln_x_gain, ln_x_bias, w_branch_attn, w_branch_rwkv,
           w_out, norm2_gain, w_ff_in, w_ff_out):
    b, t, d = h.shape
    x2 = h.reshape(b * t, d)
    row = lambda a: a.reshape(1, -1).astype(F32)
    w_in_b = w_in.astype(BF16)

    mu = jnp.concatenate([mu_r, mu_k, mu_v, mu_w, mu_a, mu_g]).reshape(1, -1).astype(F32)
    q, kv, rw = _inproj(x2, row(norm1_gain), w_in_b, mu, t)
    rwkv = _rwkv(rw.reshape(b, t, -1), row(decay_bias), decay_up.astype(BF16),
                 row(aaa_bias), aaa_up.astype(BF16), gate_up.astype(BF16),
                 row(k_k), row(k_a), row(r_k), row(ln_x_gain), row(ln_x_bias))
    h2 = _attn_merge(x2, q.reshape(b, t, -1), kv.reshape(b, t, -1), rwkv.reshape(b * t, -1),
                     row(norm1_gain), row(q_norm_gain), row(k_norm_gain), attn_sinks.astype(F32),
                     w_in_b, w_branch_attn.astype(BF16), w_branch_rwkv.astype(BF16),
                     w_out.astype(BF16))
    out = _mlp(h2, row(norm2_gain), w_ff_in.astype(BF16), w_ff_out.astype(BF16))
    return out.reshape(b, t, d)


def kernel(x, norm1_gain, w_in, q_norm_gain, k_norm_gain, attn_sinks, mu_r, mu_k, mu_v, mu_w, mu_a, mu_g, decay_bias, decay_up, aaa_bias, aaa_up, gate_up, k_k, k_a, r_k, ln_x_gain, ln_x_bias, w_branch_attn, w_branch_rwkv, w_out, norm2_gain, w_ff_in, w_ff_out):
    h = x.astype(F32)
    params = (norm1_gain, w_in, q_norm_gain, k_norm_gain, attn_sinks, mu_r, mu_k, mu_v, mu_w,
              mu_a, mu_g, decay_bias, decay_up, aaa_bias, aaa_up, gate_up, k_k, k_a, r_k,
              ln_x_gain, ln_x_bias, w_branch_attn, w_branch_rwkv, w_out, norm2_gain,
              w_ff_in, w_ff_out)
    for l in range(norm1_gain.shape[0]):
        h = _layer(h, *(p[l] for p in params))
    return h.astype(x.dtype)
```

```python
import functools
import math

import jax
import jax.numpy as jnp
from jax import lax
from jax.experimental import pallas as pl
from jax.experimental.pallas import tpu as pltpu

F32 = jnp.float32
BF16 = jnp.bfloat16

HEAD_DIM = 64
ATTN_Q_HEADS = 8
ATTN_KV_HEADS = 2
ATTN_GROUP = ATTN_Q_HEADS // ATTN_KV_HEADS
WINDOW = 128
BLOCK = 128
RWKV_HEADS = 8
RWKV_HEAD_SIZE = 64
LORA_DECAY = 64
LORA_AAA = 64
LORA_GATE = 128
ATTN_Q_WIDTH = ATTN_Q_HEADS * HEAD_DIM
ATTN_KV_WIDTH = ATTN_KV_HEADS * HEAD_DIM
RWKV_WIDTH = RWKV_HEADS * RWKV_HEAD_SIZE
RWKV_IN_WIDTH = 3 * RWKV_WIDTH + LORA_DECAY + LORA_AAA + LORA_GATE
N_PROJ = ATTN_Q_WIDTH + 2 * ATTN_KV_WIDTH + RWKV_IN_WIDTH
RMS_EPS = 1e-6
GN_EPS = 64e-5
L2_EPS = 1e-12

CHUNK = 64
ROW_TILE = 512
ATTN_TILE = 1024
RWKV_TILE = 1024
RWKV_GROUP = 256
VMEM_LIMIT = 56 * 1024 * 1024

_NT = (((1,), (1,)), ((), ()))
_TN = (((0,), (0,)), ((), ()))


def _bdot(a, b):
    return jnp.dot(a.astype(BF16), b.astype(BF16), preferred_element_type=F32)


def _rms(x, gain):
    return x * lax.rsqrt(jnp.mean(x * x, axis=-1, keepdims=True) + RMS_EPS) * gain


def _sigmoid(x):
    return 0.5 * jnp.tanh(0.5 * x) + 0.5


def _inproj_kernel(x_ref, g_ref, w_ref, mu_ref, q_ref, kv_ref, rw_ref, last_ref, *, tiles_per_seq):
    u = _rms(x_ref[...], g_ref[...])
    p = _bdot(u, w_ref[...])
    q_ref[...] = p[:, :ATTN_Q_WIDTH]
    kv_ref[...] = p[:, ATTN_Q_WIDTH:ATTN_Q_WIDTH + 2 * ATTN_KV_WIDTH]
    rw = p[:, ATTN_Q_WIDTH + 2 * ATTN_KV_WIDTH:]
    tm = rw.shape[0]
    seq_start = pl.program_id(0) % tiles_per_seq == 0
    before = jnp.where(seq_start, 0.0, last_ref[...])
    first = lax.broadcasted_iota(jnp.int32, (tm, 1), 0) == 0
    prev = jnp.where(first, before, pltpu.roll(rw, 1, 0))
    last_ref[...] = rw[tm - 1:tm, :]
    rw_ref[...] = rw + (prev - rw) * mu_ref[...]


def _inproj(x2, gain, w, mu, seq_len):
    n, d = x2.shape
    wid = N_PROJ
    tm = min(2 * ROW_TILE, seq_len)

    def full(shape):
        return pl.BlockSpec(shape, lambda i: (0, 0), pipeline_mode=pl.Buffered(1))

    return pl.pallas_call(
        functools.partial(_inproj_kernel, tiles_per_seq=seq_len // tm),
        grid=(n // tm,),
        in_specs=[
            pl.BlockSpec((tm, d), lambda i: (i, 0)),
            full((1, d)), full((d, wid)), full((1, RWKV_IN_WIDTH)),
        ],
        out_specs=[
            pl.BlockSpec((tm, ATTN_Q_WIDTH), lambda i: (i, 0)),
            pl.BlockSpec((tm, 2 * ATTN_KV_WIDTH), lambda i: (i, 0)),
            pl.BlockSpec((tm, RWKV_IN_WIDTH), lambda i: (i, 0)),
        ],
        out_shape=[
            jax.ShapeDtypeStruct((n, ATTN_Q_WIDTH), F32),
            jax.ShapeDtypeStruct((n, 2 * ATTN_KV_WIDTH), F32),
            jax.ShapeDtypeStruct((n, RWKV_IN_WIDTH), F32),
        ],
        scratch_shapes=[pltpu.VMEM((1, RWKV_IN_WIDTH), F32)],
        compiler_params=pltpu.CompilerParams(
            dimension_semantics=("arbitrary",), vmem_limit_bytes=VMEM_LIMIT),
        name="inproj",
    )(x2, gain, w, mu)


def _head_sumsq(x, ones_bd):
    x2 = x * x
    hi = x2.astype(BF16)
    lo = (x2 - hi.astype(F32)).astype(BF16)
    return (jnp.dot(hi, ones_bd, preferred_element_type=F32)
            + jnp.dot(lo, ones_bd, preferred_element_type=F32))


def _attn_prep_kv(kv_all, q_gain, k_gain):
    hp = 2 * HEAD_DIM
    even = lax.broadcasted_iota(jnp.int32, (1, hp), 1) < HEAD_DIM
    ri = lax.broadcasted_iota(jnp.int32, (hp, hp), 0)
    ci = lax.broadcasted_iota(jnp.int32, (hp, hp), 1)
    ones_bd = ((ri < HEAD_DIM) == (ci < HEAD_DIM)).astype(BF16)
    k = kv_all[:, :hp]
    kn = k * lax.rsqrt(_head_sumsq(k, ones_bd) * (1.0 / HEAD_DIM) + RMS_EPS) * k_gain * q_gain
    kn_sw = pltpu.roll(kn, HEAD_DIM, 1)
    kdup = [jnp.where(even, kn, kn_sw).astype(BF16), jnp.where(even, kn_sw, kn).astype(BF16)]
    v_t = kv_all[:, hp:].T.astype(BF16)
    return kdup, v_t


def _attn_prep_q(q):
    hp = 2 * HEAD_DIM
    si = lax.broadcasted_iota(jnp.int32, (16, hp), 0)
    sj = lax.broadcasted_iota(jnp.int32, (16, hp), 1)
    sel = (((si == 0) & (sj < HEAD_DIM)) | ((si == 1) & (sj >= HEAD_DIM))).astype(BF16)
    q2 = q * q
    hi = q2.astype(BF16)
    lo = (q2 - hi.astype(F32)).astype(BF16)
    inv_rms = []
    for m in range(ATTN_Q_WIDTH // hp):
        cs = slice(m * hp, (m + 1) * hp)
        ss = (lax.dot_general(sel, hi[:, cs], _NT, preferred_element_type=F32)
              + lax.dot_general(sel, lo[:, cs], _NT, preferred_element_type=F32))
        inv_rms.append(lax.rsqrt(ss * (1.0 / HEAD_DIM) + RMS_EPS))
    return q.astype(BF16), inv_rms


def _attn_scores(j, kh, qb, kdup, inv_rms, no_prev):
    hp = 2 * HEAD_DIM
    even = lax.broadcasted_iota(jnp.int32, (1, hp), 1) < HEAD_DIM
    zero_q = jnp.zeros((BLOCK, hp), BF16)
    parts, scale = [], []
    for g in range(ATTN_GROUP):
        h = kh * ATTN_GROUP + g
        col = qb[j * BLOCK:(j + 1) * BLOCK, (h // 2) * hp:(h // 2 + 1) * hp]
        parts.append(jnp.where(even, col, zero_q) if h % 2 == 0 else jnp.where(even, zero_q, col))
        scale.append(inv_rms[h // 2][h % 2:h % 2 + 1, j * BLOCK:(j + 1) * BLOCK])
    s_t = lax.dot_general(kdup[kh][j * BLOCK:(j + 2) * BLOCK], jnp.concatenate(parts, axis=0), _NT,
                          preferred_element_type=F32)
    key = lax.broadcasted_iota(jnp.int32, (BLOCK, ATTN_GROUP * BLOCK), 0)
    qry = lax.broadcasted_iota(jnp.int32, (BLOCK, ATTN_GROUP * BLOCK), 1) & (BLOCK - 1)
    use_prev = key > qry
    prev = s_t[:BLOCK]
    if no_prev is not None:
        prev = jnp.where(no_prev, -jnp.inf, prev)
    folded = jnp.where(use_prev, prev, s_t[BLOCK:]) * jnp.concatenate(scale, axis=1)
    return folded, use_prev


def _attn_softmax(kh, folded, use_prev, sink_ref):
    sink = jnp.concatenate(
        [jnp.full((1, BLOCK), sink_ref[kh * ATTN_GROUP + g], F32) for g in range(ATTN_GROUP)], axis=1)
    m = jnp.maximum(jnp.max(folded, axis=0, keepdims=True), sink)
    p = jnp.exp(folded - m)
    den = jnp.sum(p, axis=0, keepdims=True) + jnp.exp(sink - m)
    zero = jnp.zeros_like(p)
    p_t = jnp.concatenate([jnp.where(use_prev, p, zero), jnp.where(use_prev, zero, p)], axis=0)
    return p_t.astype(BF16), 1.0 / den


def _attn_values(j, kh, p_t, inv_den, v_t, o_ref):
    o_t = jnp.dot(v_t[kh * HEAD_DIM:(kh + 1) * HEAD_DIM, j * BLOCK:(j + 2) * BLOCK], p_t,
                  preferred_element_type=F32) * inv_den
    for g in range(ATTN_GROUP):
        h = kh * ATTN_GROUP + g
        o_ref[h * HEAD_DIM:(h + 1) * HEAD_DIM, j * BLOCK:(j + 1) * BLOCK] = (
            o_t[:, g * BLOCK:(g + 1) * BLOCK].astype(o_ref.dtype))


def _pair_blockdiag(x, even):
    zero = jnp.zeros_like(x)
    return jnp.concatenate([jnp.where(even, x, zero), jnp.where(even, zero, x)], axis=0)


def _pair_sum(x, even):
    s_e = jnp.sum(jnp.where(even, x, 0.0), axis=-1, keepdims=True)
    s_o = jnp.sum(jnp.where(even, 0.0, x), axis=-1, keepdims=True)
    return jnp.where(even, s_e, s_o)


def _rwkv_prep(xs, prm):
    c = CHUNK
    hp = 2 * RWKV_HEAD_SIZE
    w = RWKV_WIDTH
    rows = xs.shape[0]
    nc = rows // c
    cols = [slice(p * hp, (p + 1) * hp) for p in range(w // hp)]
    r = xs[:, 0:w]
    k = xs[:, w:2 * w]
    v = xs[:, 2 * w:3 * w]
    w_lora = xs[:, 3 * w:3 * w + LORA_DECAY]
    a_lora = xs[:, 3 * w + LORA_DECAY:3 * w + LORA_DECAY + LORA_AAA]
    g_lora = xs[:, 3 * w + LORA_DECAY + LORA_AAA:]

    d = prm["dbias"][...] + _bdot(jnp.tanh(w_lora), prm["dup"][...])
    lw = (-math.exp(-0.5)) * _sigmoid(d)
    a_sig = _sigmoid(prm["abias"][...] + _bdot(a_lora, prm["aup"][...]))
    gate = _bdot(_sigmoid(g_lora), prm["gup"][...])
    yield

    kkr = k * prm["kk"][...]
    ka = prm["ka"][...]
    k2 = k * (a_sig * ka + (1.0 - ka))
    rk = r * k2 * prm["rk"][...]
    even_t = lax.broadcasted_iota(jnp.int32, (rows, hp), 1) < RWKV_HEAD_SIZE
    den = jnp.concatenate(
        [jnp.maximum(jnp.sqrt(_pair_sum(kkr[:, cs] * kkr[:, cs], even_t)), L2_EPS) for cs in cols],
        axis=1)
    bonus = jnp.concatenate([_pair_sum(rk[:, cs], even_t) for cs in cols], axis=1) * v
    kk = kkr / den
    a_s = -kk
    b_s = kk * a_sig
    yield

    ti = lax.broadcasted_iota(jnp.int32, (c, 3 * c), 0)
    si = lax.broadcasted_iota(jnp.int32, (c, 3 * c), 1) & (c - 1)
    lower3 = (ti >= si).astype(BF16)
    lw_hi = lw.astype(BF16)
    lw_r = lw - lw_hi.astype(F32)
    lw_mid = lw_r.astype(BF16)
    lw_lo = (lw_r - lw_mid.astype(F32)).astype(BF16)
    cl = jnp.concatenate(
        [jnp.dot(lower3, jnp.concatenate([z[ci * c:(ci + 1) * c] for z in (lw_hi, lw_mid, lw_lo)], axis=0),
                 preferred_element_type=F32) for ci in range(nc)], axis=0)
    yield

    e_c = jnp.exp(cl)
    e_neg = 1.0 / e_c
    g_rows = [e_c[ci * c + c - 1:ci * c + c, :] for ci in range(nc)]
    g_full = jnp.concatenate([jnp.broadcast_to(g, (c, w)) for g in g_rows], axis=0)
    at = a_s * jnp.exp(cl - lw)
    rt = r * e_c
    yield
    bt = b_s * e_neg
    kt = k2 * e_neg
    ops = dict(rt=rt, g_rows=g_rows, gate=gate, bonus=bonus,
               bh=bt * g_full, khat=kt * g_full,
               at_b=at.astype(BF16), rt_b=rt.astype(BF16), bt_b=bt.astype(BF16),
               kt_b=kt.astype(BF16), v_b=v.astype(BF16))
    return ops


def _rwkv_solve(ops, prm):
    c = CHUNK
    hs = RWKV_HEAD_SIZE
    assert c == hs
    hp = 2 * hs
    w = RWKV_WIDTH
    npair = w // hp
    nc = ops["rt"].shape[0] // c
    cols = [slice(p * hp, (p + 1) * hp) for p in range(npair)]
    even = lax.broadcasted_iota(jnp.int32, (c, hp), 1) < RWKV_HEAD_SIZE
    gi = lax.broadcasted_iota(jnp.int32, (2 * c, 2 * hp), 0)
    g_t = gi & (c - 1)
    g_s = lax.broadcasted_iota(jnp.int32, (2 * c, 2 * hp), 1) & (c - 1)
    g_mask = (g_t > g_s) | ((gi >= c) & (g_t == g_s))
    eye2 = (lax.broadcasted_iota(jnp.int32, (c, hp), 0)
            == (lax.broadcasted_iota(jnp.int32, (c, hp), 1) & (c - 1)))
    units = [(ci, p) for ci in range(nc) for p in range(npair)]

    def col(name, ci, p):
        return ops[name][ci * c:(ci + 1) * c, cols[p]]

    def bd(x):
        return _pair_blockdiag(x, even)

    def lanes(*xs):
        return jnp.concatenate(xs, axis=1)

    def rows(*xs):
        return jnp.concatenate(xs, axis=0)

    def mm(a, b):
        return jnp.dot(a, b, preferred_element_type=F32)

    gm = [jnp.where(g_mask,
                    lax.dot_general(rows(col("at_b", ci, p), col("rt_b", ci, p)),
                                    rows(bd(col("bt_b", ci, p)), bd(col("kt_b", ci, p))), _NT,
                                    preferred_element_type=F32), 0.0)
          for ci, p in units]
    yield
    a_pow = [g[:c, :hp].astype(BF16) for g in gm]
    t_inv = [jnp.where(eye2, 1.0, g[:c, :hp]) for g in gm]
    a_pow = [mm(a, bd(a)).astype(BF16) for a in a_pow]
    v_st = [col("v_b", ci, p) for ci, p in units]
    akv = [mm(g[:c, hp:].astype(BF16), bd(v)).astype(BF16) for g, v in zip(gm, v_st)]
    yield
    for _ in range(int(math.log2(c)) - 2):
        res = [mm(a, lanes(bd(t.astype(BF16)), bd(a))) for a, t in zip(a_pow, t_inv)]
        t_inv = [t + z[:, :hp] for t, z in zip(t_inv, res)]
        a_pow = [z[:, hp:].astype(BF16) for z in res]
        yield
    t_inv = [t + mm(a, bd(t.astype(BF16))) for a, t in zip(a_pow, t_inv)]
    yield
    wu = [mm(t.astype(BF16), lanes(bd(col("at_b", ci, p)), bd(x)))
          for (ci, p), t, x in zip(units, t_inv, akv)]
    w_b = [z[:, :hp].astype(BF16) for z in wu]
    u0_b = [z[:, hp:].astype(BF16) for z in wu]
    yield
    def head_t(x):
        xt = x.T
        return lanes(xt[:hs], xt[hs:]).astype(BF16)

    ab = [mm(rows(g[c:, :].astype(BF16), lanes(head_t(col("bh", ci, p)), head_t(col("khat", ci, p)))),
             rows(lanes(bd(wb), bd(ub)), lanes(jnp.zeros((hp, hp), BF16), bd(v))))
          for (ci, p), g, wb, ub, v in zip(units, gm, w_b, u0_b, v_st)]
    ab_y = [z[:c] for z in ab]
    ab_h = [z[c:] for z in ab]
    yield

    def tail(state, out_rows):
        for ci in range(nc):
            y_cols = []
            for p in range(npair):
                idx = ci * npair + p
                qmat = col("rt", ci, p) + ab_y[idx][:, :hp]
                g_diag = jnp.where(eye2, jnp.broadcast_to(ops["g_rows"][ci][:, cols[p]], (hs, hp)), 0.0)
                mmat = g_diag + ab_h[idx][:, :hp]
                ys = mm(rows(qmat, mmat).astype(BF16), bd(state[p].astype(BF16)))
                y = ys[:c] + ab_y[idx][:, hp:]
                state[p] = ys[c:] + ab_h[idx][:, hp:]
                mean = _pair_sum(y, even) * (1.0 / RWKV_HEAD_SIZE)
                yc = y - mean
                var = _pair_sum(yc * yc, even) * (1.0 / RWKV_HEAD_SIZE)
                y_cols.append(yc * lax.rsqrt(var + GN_EPS))
            yn = jnp.concatenate(y_cols, axis=1)
            rs = slice(ci * c, (ci + 1) * c)
            out_rows(ci, (yn * prm["lng"][...] + prm["lnb"][...] + ops["bonus"][rs]) * ops["gate"][rs])
            yield

    return tail


def _round_robin(gens):
    values = [None] * len(gens)
    live = list(range(len(gens)))
    while live:
        for i in list(live):
            try:
                next(gens[i])
            except StopIteration as stop:
                values[i] = stop.value
                live.remove(i)
    return values


def _rwkv_kernel(xs_ref, dbias_ref, dup_ref, abias_ref, aup_ref, gup_ref,
                 kk_ref, ka_ref, rk_ref, lng_ref, lnb_ref, o_ref, state_ref):
    hp = 2 * RWKV_HEAD_SIZE
    tb = xs_ref.shape[1]
    gr = min(RWKV_GROUP, tb)
    ngroup = tb // gr
    prm = dict(dbias=dbias_ref, dup=dup_ref, abias=abias_ref, aup=aup_ref, gup=gup_ref,
               kk=kk_ref, ka=ka_ref, rk=rk_ref, lng=lng_ref, lnb=lnb_ref)

    @pl.when(pl.program_id(1) == 0)
    def _():
        state_ref[...] = jnp.zeros_like(state_ref)

    def prep(g):
        return _rwkv_prep(xs_ref[0, g * gr:(g + 1) * gr, :], prm)

    state = [state_ref[:, p * hp:(p + 1) * hp] for p in range(RWKV_WIDTH // hp)]
    (ops,) = _round_robin([prep(0)])
    done = _round_robin([_rwkv_solve(ops, prm)] + ([prep(1)] if ngroup > 1 else []))
    tail, ops = done[0], (done[1] if ngroup > 1 else None)
    for g in range(ngroup):
        def out_rows(ci, value, g=g):
            r0 = g * gr + ci * CHUNK
            o_ref[0, r0:r0 + CHUNK, :] = value.astype(o_ref.dtype)

        gens = [tail(state, out_rows)]
        if g + 1 < ngroup:
            gens.append(_rwkv_solve(ops, prm))
        if g + 2 < ngroup:
            gens.append(prep(g + 2))
        done = _round_robin(gens)
        tail = done[1] if g + 1 < ngroup else None
        ops = done[2] if g + 2 < ngroup else None
    for p in range(RWKV_WIDTH // hp):
        state_ref[:, p * hp:(p + 1) * hp] = state[p]


def _rwkv(rw, decay_bias, decay_up, aaa_bias, aaa_up, gate_up, k_k, k_a, r_k, ln_g, ln_b):
    b, t, _ = rw.shape
    tb = min(RWKV_TILE, t)
    w = RWKV_WIDTH

    def full(shape):
        return pl.BlockSpec(shape, lambda bi, ti: (0, 0))

    return pl.pallas_call(
        _rwkv_kernel,
        grid=(b, t // tb),
        in_specs=[
            pl.BlockSpec((1, tb, RWKV_IN_WIDTH), lambda bi, ti: (bi, ti, 0)),
            full((1, w)), full((LORA_DECAY, w)),
            full((1, w)), full((LORA_AAA, w)),
            full((LORA_GATE, w)),
            full((1, w)), full((1, w)), full((1, w)), full((1, w)), full((1, w)),
        ],
        out_specs=pl.BlockSpec((1, tb, w), lambda bi, ti: (bi, ti, 0)),
        out_shape=jax.ShapeDtypeStruct((b, t, w), BF16),
        scratch_shapes=[pltpu.VMEM((RWKV_HEAD_SIZE, w), F32)],
        compiler_params=pltpu.CompilerParams(
            dimension_semantics=("arbitrary", "arbitrary"), vmem_limit_bytes=VMEM_LIMIT),
        name="rwkv",
    )(rw, decay_bias, decay_up, aaa_bias, aaa_up, gate_up, k_k, k_a, r_k, ln_g, ln_b)


def _attn_merge_kernel(x_ref, q_ref, kv_ref, kvp_ref, rwkv_ref, g_ref, qg_ref, kg_ref, sink_ref,
                       wg_ref, wba_ref, wbr_ref, wo_ref, h_ref, attn_ref):
    x = x_ref[...]
    d = x.shape[1]
    nblk = q_ref.shape[1] // BLOCK
    units = [(j, kh) for j in range(nblk) for kh in range(ATTN_KV_HEADS)]
    n_u = len(units)
    gw = 2 * BLOCK
    n_g = (wg_ref.shape[1] - N_PROJ) // gw
    gate_parts = []

    def gate_slice():
        c0 = N_PROJ + len(gate_parts) * gw
        gate_parts.append(jnp.dot(ub, wg_ref[:, c0:c0 + gw], preferred_element_type=F32))

    y_rwkv = jnp.dot(rwkv_ref[...], wbr_ref[...], preferred_element_type=F32)
    ub = _rms(x, g_ref[...]).astype(BF16)
    gate_slice()
    kv_all = jnp.concatenate([kvp_ref[0], kv_ref[0]], axis=0)
    kdup, v_t = _attn_prep_kv(kv_all, qg_ref[...], kg_ref[...])
    gate_slice()
    qb, inv_rms = _attn_prep_q(q_ref[0])
    first_tile = pl.program_id(1) == 0
    scores = [_attn_scores(j, kh, qb, kdup, inv_rms, first_tile if j == 0 else None)
              for j, kh in units]
    soft = []
    for idx, (_, kh) in enumerate(units):
        while len(gate_parts) < 2 + (idx + 1) * (n_g - 2) // n_u:
            gate_slice()
        soft.append(_attn_softmax(kh, *scores[idx], sink_ref))
    while len(gate_parts) < n_g:
        gate_slice()
    gates = jnp.concatenate(gate_parts, axis=1)
    for unit, sm in zip(units, soft):
        _attn_values(*unit, *sm, v_t, attn_ref)
    y_attn = lax.dot_general(attn_ref[...], wba_ref[...], _TN, preferred_element_type=F32)
    mixed = _sigmoid(gates[:, :d]) * y_attn + _sigmoid(gates[:, d:]) * y_rwkv
    h_ref[...] = x + _bdot(mixed, wo_ref[...])


def _attn_merge(x2, q, kv, rwkv, gain, q_gain, k_gain, sinks, w_in, w_ba, w_br, w_out):
    n, d = x2.shape
    b, t, _ = q.shape
    tq = min(ATTN_TILE, t)
    nblk = tq // BLOCK
    nt = t // tq
    q_gain = jnp.tile(q_gain * (HEAD_DIM ** -0.5), (1, ATTN_KV_HEADS))
    k_gain = jnp.tile(k_gain, (1, ATTN_KV_HEADS))

    def full(a):
        return pl.BlockSpec(a.shape, lambda bi, i: (0, 0), pipeline_mode=pl.Buffered(1))

    def rows(width):
        return pl.BlockSpec((tq, width), lambda bi, i: (bi * nt + i, 0))

    return pl.pallas_call(
        _attn_merge_kernel,
        grid=(b, nt),
        in_specs=[
            rows(d),
            pl.BlockSpec((1, tq, ATTN_Q_WIDTH), lambda bi, i: (bi, i, 0)),
            pl.BlockSpec((1, tq, 2 * ATTN_KV_WIDTH), lambda bi, i: (bi, i, 0)),
            pl.BlockSpec((1, BLOCK, 2 * ATTN_KV_WIDTH),
                         lambda bi, i: (bi, jnp.maximum(i * nblk - 1, 0), 0)),
            rows(rwkv.shape[1]),
            full(gain), full(q_gain), full(k_gain),
            pl.BlockSpec(memory_space=pltpu.SMEM),
            full(w_in), full(w_ba), full(w_br), full(w_out),
        ],
        out_specs=rows(d),
        out_shape=jax.ShapeDtypeStruct((n, d), F32),
        scratch_shapes=[pltpu.VMEM((ATTN_Q_WIDTH, tq), BF16)],
        compiler_params=pltpu.CompilerParams(
            dimension_semantics=("arbitrary", "arbitrary"), vmem_limit_bytes=VMEM_LIMIT),
        name="attn_merge",
    )(x2, q, kv, kv, rwkv, gain, q_gain, k_gain, sinks, w_in, w_ba, w_br, w_out)


def _mlp_kernel(h_ref, g_ref, w1_ref, w2_ref, o_ref):
    h = h_ref[...]
    hidden = jnp.square(jnp.maximum(_bdot(_rms(h, g_ref[...]), w1_ref[...]), 0.0))
    o_ref[...] = h + _bdot(hidden, w2_ref[...])


def _mlp(h2, gain, w1, w2):
    n, d = h2.shape
    tm = min(ROW_TILE, n)

    def full(a):
        return pl.BlockSpec(a.shape, lambda i: (0, 0), pipeline_mode=pl.Buffered(1))

    return pl.pallas_call(
        _mlp_kernel,
        grid=(n // tm,),
        in_specs=[pl.BlockSpec((tm, d), lambda i: (i, 0)), full(gain), full(w1), full(w2)],
        out_specs=pl.BlockSpec((tm, d), lambda i: (i, 0)),
        out_shape=jax.ShapeDtypeStruct((n, d), F32),
        compiler_params=pltpu.CompilerParams(
            dimension_semantics=("arbitrary",), vmem_limit_bytes=VMEM_LIMIT),
        name="mlp",
    )(h2, gain, w1, w2)


def _layer(h, norm1_gain, w_in, q_norm_gain, k_norm_gain, attn_sinks,
           mu_r, mu_k, mu_v, mu_w, mu_a, mu_g, decay_bias, decay_up, aaa_bias, aaa_up,
           gate_up, k_k, k_a, r_k, ln_x_gain, ln_x_bias, w_branch_attn, w_branch_rwkv,
           w_out, norm2_gain, w_ff_in, w_ff_out):
    b, t, d = h.shape
    x2 = h.reshape(b * t, d)
    row = lambda a: a.reshape(1, -1).astype(F32)
    w_in_b = w_in.astype(BF16)

    mu = jnp.concatenate([mu_r, mu_k, mu_v, mu_w, mu_a, mu_g]).reshape(1, -1).astype(F32)
    q, kv, rw = _inproj(x2, row(norm1_gain), w_in_b, mu, t)
    rwkv = _rwkv(rw.reshape(b, t, -1), row(decay_bias), decay_up.astype(BF16),
                 row(aaa_bias), aaa_up.astype(BF16), gate_up.astype(BF16),
                 row(k_k), row(k_a), row(r_k), row(ln_x_gain), row(ln_x_bias))
    h2 = _attn_merge(x2, q.reshape(b, t, -1), kv.reshape(b, t, -1), rwkv.reshape(b * t, -1),
                     row(norm1_gain), row(q_norm_gain), row(k_norm_gain), attn_sinks.astype(F32),
                     w_in_b, w_branch_attn.astype(BF16), w_branch_rwkv.astype(BF16),
                     w_out.astype(BF16))
    out = _mlp(h2, row(norm2_gain), w_ff_in.astype(BF16), w_ff_out.astype(BF16))
    return out.reshape(b, t, d)


def kernel(x, norm1_gain, w_in, q_norm_gain, k_norm_gain, attn_sinks, mu_r, mu_k, mu_v, mu_w, mu_a, mu_g, decay_bias, decay_up, aaa_bias, aaa_up, gate_up, k_k, k_a, r_k, ln_x_gain, ln_x_bias, w_branch_attn, w_branch_rwkv, w_out, norm2_gain, w_ff_in, w_ff_out):
    h = x.astype(F32)
    params = (norm1_gain, w_in, q_norm_gain, k_norm_gain, attn_sinks, mu_r, mu_k, mu_v, mu_w,
              mu_a, mu_g, decay_bias, decay_up, aaa_bias, aaa_up, gate_up, k_k, k_a, r_k,
              ln_x_gain, ln_x_bias, w_branch_attn, w_branch_rwkv, w_out, norm2_gain,
              w_ff_in, w_ff_out)
    for l in range(norm1_gain.shape[0]):
        h = _layer(h, *(p[l] for p in params))
    return h.astype(x.dtype)
```

```python
import functools
import math

import jax
import jax.numpy as jnp
from jax import lax
from jax.experimental import pallas as pl
from jax.experimental.pallas import tpu as pltpu

F32 = jnp.float32
BF16 = jnp.bfloat16

HEAD_DIM = 64
ATTN_Q_HEADS = 8
ATTN_KV_HEADS = 2
ATTN_GROUP = ATTN_Q_HEADS // ATTN_KV_HEADS
WINDOW = 128
BLOCK = 128
RWKV_HEADS = 8
RWKV_HEAD_SIZE = 64
LORA_DECAY = 64
LORA_AAA = 64
LORA_GATE = 128
ATTN_Q_WIDTH = ATTN_Q_HEADS * HEAD_DIM
ATTN_KV_WIDTH = ATTN_KV_HEADS * HEAD_DIM
RWKV_WIDTH = RWKV_HEADS * RWKV_HEAD_SIZE
RWKV_IN_WIDTH = 3 * RWKV_WIDTH + LORA_DECAY + LORA_AAA + LORA_GATE
N_PROJ = ATTN_Q_WIDTH + 2 * ATTN_KV_WIDTH + RWKV_IN_WIDTH
RMS_EPS = 1e-6
GN_EPS = 64e-5
L2_EPS = 1e-12

CHUNK = 64
ROW_TILE = 512
ATTN_TILE = 1024
RWKV_TILE = 1024
RWKV_GROUP = 256
VMEM_LIMIT = 56 * 1024 * 1024

_NT = (((1,), (1,)), ((), ()))
_TN = (((0,), (0,)), ((), ()))


def _bdot(a, b):
    return jnp.dot(a.astype(BF16), b.astype(BF16), preferred_element_type=F32)


def _rms(x, gain):
    return x * lax.rsqrt(jnp.mean(x * x, axis=-1, keepdims=True) + RMS_EPS) * gain


def _sigmoid(x):
    return 0.5 * jnp.tanh(0.5 * x) + 0.5


def _inproj_kernel(x_ref, g_ref, w_ref, mu_ref, q_ref, kv_ref, rw_ref, last_ref, *, tiles_per_seq):
    u = _rms(x_ref[...], g_ref[...])
    p = _bdot(u, w_ref[...])
    q_ref[...] = p[:, :ATTN_Q_WIDTH]
    kv_ref[...] = p[:, ATTN_Q_WIDTH:ATTN_Q_WIDTH + 2 * ATTN_KV_WIDTH]
    rw = p[:, ATTN_Q_WIDTH + 2 * ATTN_KV_WIDTH:]
    tm = rw.shape[0]
    seq_start = pl.program_id(0) % tiles_per_seq == 0
    before = jnp.where(seq_start, 0.0, last_ref[...])
    first = lax.broadcasted_iota(jnp.int32, (tm, 1), 0) == 0
    prev = jnp.where(first, before, pltpu.roll(rw, 1, 0))
    last_ref[...] = rw[tm - 1:tm, :]
    rw_ref[...] = rw + (prev - rw) * mu_ref[...]


def _inproj(x2, gain, w, mu, seq_len):
    n, d = x2.shape
    wid = N_PROJ
    tm = min(2 * ROW_TILE, seq_len)

    def full(shape):
        return pl.BlockSpec(shape, lambda i: (0, 0), pipeline_mode=pl.Buffered(1))

    return pl.pallas_call(
        functools.partial(_inproj_kernel, tiles_per_seq=seq_len // tm),
        grid=(n // tm,),
        in_specs=[
            pl.BlockSpec((tm, d), lambda i: (i, 0)),
            full((1, d)), full((d, wid)), full((1, RWKV_IN_WIDTH)),
        ],
        out_specs=[
            pl.BlockSpec((tm, ATTN_Q_WIDTH), lambda i: (i, 0)),
            pl.BlockSpec((tm, 2 * ATTN_KV_WIDTH), lambda i: (i, 0)),
            pl.BlockSpec((tm, RWKV_IN_WIDTH), lambda i: (i, 0)),
        ],
        out_shape=[
            jax.ShapeDtypeStruct((n, ATTN_Q_WIDTH), F32),
            jax.ShapeDtypeStruct((n, 2 * ATTN_KV_WIDTH), F32),
            jax.ShapeDtypeStruct((n, RWKV_IN_WIDTH), F32),
        ],
        scratch_shapes=[pltpu.VMEM((1, RWKV_IN_WIDTH), F32)],
        compiler_params=pltpu.CompilerParams(
            dimension_semantics=("arbitrary",), vmem_limit_bytes=VMEM_LIMIT),
        name="inproj",
    )(x2, gain, w, mu)


def _head_sumsq(x, ones_bd):
    x2 = x * x
    hi = x2.astype(BF16)
    lo = (x2 - hi.astype(F32)).astype(BF16)
    return (jnp.dot(hi, ones_bd, preferred_element_type=F32)
            + jnp.dot(lo, ones_bd, preferred_element_type=F32))


def _attn_prep_kv(kv_all, q_gain, k_gain):
    hp = 2 * HEAD_DIM
    even = lax.broadcasted_iota(jnp.int32, (1, hp), 1) < HEAD_DIM
    ri = lax.broadcasted_iota(jnp.int32, (hp, hp), 0)
    ci = lax.broadcasted_iota(jnp.int32, (hp, hp), 1)
    ones_bd = ((ri < HEAD_DIM) == (ci < HEAD_DIM)).astype(BF16)
    k = kv_all[:, :hp]
    kn = k * lax.rsqrt(_head_sumsq(k, ones_bd) * (1.0 / HEAD_DIM) + RMS_EPS) * k_gain * q_gain
    kn_sw = pltpu.roll(kn, HEAD_DIM, 1)
    kdup = [jnp.where(even, kn, kn_sw).astype(BF16), jnp.where(even, kn_sw, kn).astype(BF16)]
    v_t = kv_all[:, hp:].T.astype(BF16)
    return kdup, v_t


def _attn_prep_q(q):
    hp = 2 * HEAD_DIM
    si = lax.broadcasted_iota(jnp.int32, (16, hp), 0)
    sj = lax.broadcasted_iota(jnp.int32, (16, hp), 1)
    sel = (((si == 0) & (sj < HEAD_DIM)) | ((si == 1) & (sj >= HEAD_DIM))).astype(BF16)
    q2 = q * q
    hi = q2.astype(BF16)
    lo = (q2 - hi.astype(F32)).astype(BF16)
    inv_rms = []
    for m in range(ATTN_Q_WIDTH // hp):
        cs = slice(m * hp, (m + 1) * hp)
        ss = (lax.dot_general(sel, hi[:, cs], _NT, preferred_element_type=F32)
              + lax.dot_general(sel, lo[:, cs], _NT, preferred_element_type=F32))
        inv_rms.append(lax.rsqrt(ss * (1.0 / HEAD_DIM) + RMS_EPS))
    return q.astype(BF16), inv_rms


def _attn_scores(j, kh, qb, kdup, inv_rms, no_prev):
    hp = 2 * HEAD_DIM
    even = lax.broadcasted_iota(jnp.int32, (1, hp), 1) < HEAD_DIM
    zero_q = jnp.zeros((BLOCK, hp), BF16)
    parts, scale = [], []
    for g in range(ATTN_GROUP):
        h = kh * ATTN_GROUP + g
        col = qb[j * BLOCK:(j + 1) * BLOCK, (h // 2) * hp:(h // 2 + 1) * hp]
        parts.append(jnp.where(even, col, zero_q) if h % 2 == 0 else jnp.where(even, zero_q, col))
        scale.append(inv_rms[h // 2][h % 2:h % 2 + 1, j * BLOCK:(j + 1) * BLOCK])
    s_t = lax.dot_general(kdup[kh][j * BLOCK:(j + 2) * BLOCK], jnp.concatenate(parts, axis=0), _NT,
                          preferred_element_type=F32)
    key = lax.broadcasted_iota(jnp.int32, (BLOCK, ATTN_GROUP * BLOCK), 0)
    qry = lax.broadcasted_iota(jnp.int32, (BLOCK, ATTN_GROUP * BLOCK), 1) & (BLOCK - 1)
    use_prev = key > qry
    prev = s_t[:BLOCK]
    if no_prev is not None:
        prev = jnp.where(no_prev, -jnp.inf, prev)
    folded = jnp.where(use_prev, prev, s_t[BLOCK:]) * jnp.concatenate(scale, axis=1)
    return folded, use_prev


def _attn_softmax(kh, folded, use_prev, sink_ref):
    sink = jnp.concatenate(
        [jnp.full((1, BLOCK), sink_ref[kh * ATTN_GROUP + g], F32) for g in range(ATTN_GROUP)], axis=1)
    m = jnp.maximum(jnp.max(folded, axis=0, keepdims=True), sink)
    p = jnp.exp(folded - m)
    den = jnp.sum(p, axis=0, keepdims=True) + jnp.exp(sink - m)
    zero = jnp.zeros_like(p)
    p_t = jnp.concatenate([jnp.where(use_prev, p, zero), jnp.where(use_prev, zero, p)], axis=0)
    return p_t.astype(BF16), 1.0 / den


def _attn_values(j, kh, p_t, inv_den, v_t, o_ref):
    o_t = jnp.dot(v_t[kh * HEAD_DIM:(kh + 1) * HEAD_DIM, j * BLOCK:(j + 2) * BLOCK], p_t,
                  preferred_element_type=F32) * inv_den
    for g in range(ATTN_GROUP):
        h = kh * ATTN_GROUP + g
        o_ref[h * HEAD_DIM:(h + 1) * HEAD_DIM, j * BLOCK:(j + 1) * BLOCK] = (
            o_t[:, g * BLOCK:(g + 1) * BLOCK].astype(o_ref.dtype))


def _pair_blockdiag(x, even):
    zero = jnp.zeros_like(x)
    return jnp.concatenate([jnp.where(even, x, zero), jnp.where(even, zero, x)], axis=0)


def _pair_sum(x, even):
    s_e = jnp.sum(jnp.where(even, x, 0.0), axis=-1, keepdims=True)
    s_o = jnp.sum(jnp.where(even, 0.0, x), axis=-1, keepdims=True)
    return jnp.where(even, s_e, s_o)


def _rwkv_prep(xs, prm):
    c = CHUNK
    hp = 2 * RWKV_HEAD_SIZE
    w = RWKV_WIDTH
    rows = xs.shape[0]
    nc = rows // c
    cols = [slice(p * hp, (p + 1) * hp) for p in range(w // hp)]
    r = xs[:, 0:w]
    k = xs[:, w:2 * w]
    v = xs[:, 2 * w:3 * w]
    w_lora = xs[:, 3 * w:3 * w + LORA_DECAY]
    a_lora = xs[:, 3 * w + LORA_DECAY:3 * w + LORA_DECAY + LORA_AAA]
    g_lora = xs[:, 3 * w + LORA_DECAY + LORA_AAA:]

    d = prm["dbias"][...] + _bdot(jnp.tanh(w_lora), prm["dup"][...])
    lw = (-math.exp(-0.5)) * _sigmoid(d)
    a_sig = _sigmoid(prm["abias"][...] + _bdot(a_lora, prm["aup"][...]))
    gate = _bdot(_sigmoid(g_lora), prm["gup"][...])
    yield

    kkr = k * prm["kk"][...]
    ka = prm["ka"][...]
    k2 = k * (a_sig * ka + (1.0 - ka))
    rk = r * k2 * prm["rk"][...]
    even_t = lax.broadcasted_iota(jnp.int32, (rows, hp), 1) < RWKV_HEAD_SIZE
    den = jnp.concatenate(
        [jnp.maximum(jnp.sqrt(_pair_sum(kkr[:, cs] * kkr[:, cs], even_t)), L2_EPS) for cs in cols],
        axis=1)
    bonus = jnp.concatenate([_pair_sum(rk[:, cs], even_t) for cs in cols], axis=1) * v
    kk = kkr / den
    a_s = -kk
    b_s = kk * a_sig
    yield

    ti = lax.broadcasted_iota(jnp.int32, (c, 3 * c), 0)
    si = lax.broadcasted_iota(jnp.int32, (c, 3 * c), 1) & (c - 1)
    lower3 = (ti >= si).astype(BF16)
    lw_hi = lw.astype(BF16)
    lw_r = lw - lw_hi.astype(F32)
    lw_mid = lw_r.astype(BF16)
    lw_lo = (lw_r - lw_mid.astype(F32)).astype(BF16)
    cl = jnp.concatenate(
        [jnp.dot(lower3, jnp.concatenate([z[ci * c:(ci + 1) * c] for z in (lw_hi, lw_mid, lw_lo)], axis=0),
                 preferred_element_type=F32) for ci in range(nc)], axis=0)
    yield

    e_c = jnp.exp(cl)
    e_neg = 1.0 / e_c
    g_rows = [e_c[ci * c + c - 1:ci * c + c, :] for ci in range(nc)]
    g_full = jnp.concatenate([jnp.broadcast_to(g, (c, w)) for g in g_rows], axis=0)
    at = a_s * jnp.exp(cl - lw)
    rt = r * e_c
    yield
    bt = b_s * e_neg
    kt = k2 * e_neg
    ops = dict(rt=rt, g_rows=g_rows, gate=gate, bonus=bonus,
               bh=(bt * g_full).astype(BF16), khat=(kt * g_full).astype(BF16),
               at_b=at.astype(BF16), rt_b=rt.astype(BF16), bt_b=bt.astype(BF16),
               kt_b=kt.astype(BF16), v_b=v.astype(BF16))
    return ops


def _rwkv_solve(ops, prm):
    c = CHUNK
    hs = RWKV_HEAD_SIZE
    assert c == hs
    hp = 2 * hs
    w = RWKV_WIDTH
    npair = w // hp
    nc = ops["rt"].shape[0] // c
    cols = [slice(p * hp, (p + 1) * hp) for p in range(npair)]
    even = lax.broadcasted_iota(jnp.int32, (c, hp), 1) < RWKV_HEAD_SIZE
    gi = lax.broadcasted_iota(jnp.int32, (2 * c, 2 * hp), 0)
    g_t = gi & (c - 1)
    g_s = lax.broadcasted_iota(jnp.int32, (2 * c, 2 * hp), 1) & (c - 1)
    g_mask = (g_t > g_s) | ((gi >= c) & (g_t == g_s))
    eye2 = (lax.broadcasted_iota(jnp.int32, (c, hp), 0)
            == (lax.broadcasted_iota(jnp.int32, (c, hp), 1) & (c - 1)))
    units = [(ci, p) for ci in range(nc) for p in range(npair)]

    def col(name, ci, p):
        return ops[name][ci * c:(ci + 1) * c, cols[p]]

    def bd(x):
        return _pair_blockdiag(x, even)

    def lanes(*xs):
        return jnp.concatenate(xs, axis=1)

    def rows(*xs):
        return jnp.concatenate(xs, axis=0)

    def mm(a, b):
        return jnp.dot(a, b, preferred_element_type=F32)

    gm = [jnp.where(g_mask,
                    lax.dot_general(rows(col("at_b", ci, p), col("rt_b", ci, p)),
                                    rows(bd(col("bt_b", ci, p)), bd(col("kt_b", ci, p))), _NT,
                                    preferred_element_type=F32), 0.0)
          for ci, p in units]
    yield
    a_pow = [g[:c, :hp].astype(BF16) for g in gm]
    t_inv = [jnp.where(eye2, 1.0, g[:c, :hp]) for g in gm]
    a_pow = [mm(a, bd(a)).astype(BF16) for a in a_pow]
    v_st = [col("v_b", ci, p) for ci, p in units]
    akv = [mm(g[:c, hp:].astype(BF16), bd(v)).astype(BF16) for g, v in zip(gm, v_st)]
    yield
    for _ in range(int(math.log2(c)) - 2):
        res = [mm(a, lanes(bd(t.astype(BF16)), bd(a))) for a, t in zip(a_pow, t_inv)]
        t_inv = [t + z[:, :hp] for t, z in zip(t_inv, res)]
        a_pow = [z[:, hp:].astype(BF16) for z in res]
        yield
    t_inv = [t + mm(a, bd(t.astype(BF16))) for a, t in zip(a_pow, t_inv)]
    yield
    wu = [mm(t.astype(BF16), lanes(bd(col("at_b", ci, p)), bd(x)))
          for (ci, p), t, x in zip(units, t_inv, akv)]
    w_b = [z[:, :hp].astype(BF16) for z in wu]
    u0_b = [z[:, hp:].astype(BF16) for z in wu]
    yield
    def head_t(x):
        xt = x.T
        return lanes(xt[:hs], xt[hs:]).astype(BF16)

    ab = [mm(rows(g[c:, :].astype(BF16), lanes(head_t(col("bh", ci, p)), head_t(col("khat", ci, p)))),
             rows(lanes(bd(wb), bd(ub)), lanes(jnp.zeros((hp, hp), BF16), bd(v))))
          for (ci, p), g, wb, ub, v in zip(units, gm, w_b, u0_b, v_st)]
    ab_y = [z[:c] for z in ab]
    ab_h = [z[c:] for z in ab]
    yield

    def tail(state, out_rows):
        for ci in range(nc):
            y_cols = []
            for p in range(npair):
                idx = ci * npair + p
                qmat = col("rt", ci, p) + ab_y[idx][:, :hp]
                g_diag = jnp.where(eye2, jnp.broadcast_to(ops["g_rows"][ci][:, cols[p]], (hs, hp)), 0.0)
                mmat = g_diag + ab_h[idx][:, :hp]
                ys = mm(rows(qmat, mmat).astype(BF16), bd(state[p].astype(BF16)))
                y = ys[:c] + ab_y[idx][:, hp:]
                state[p] = ys[c:] + ab_h[idx][:, hp:]
                mean = _pair_sum(y, even) * (1.0 / RWKV_HEAD_SIZE)
                yc = y - mean
                var = _pair_sum(yc * yc, even) * (1.0 / RWKV_HEAD_SIZE)
                y_cols.append(yc * lax.rsqrt(var + GN_EPS))
            yn = jnp.concatenate(y_cols, axis=1)
            rs = slice(ci * c, (ci + 1) * c)
            out_rows(ci, (yn * prm["lng"][...] + prm["lnb"][...] + ops["bonus"][rs]) * ops["gate"][rs])
            yield

    return tail


def _round_robin(gens):
    values = [None] * len(gens)
    live = list(range(len(gens)))
    while live:
        for i in list(live):
            try:
                next(gens[i])
            except StopIteration as stop:
                values[i] = stop.value
                live.remove(i)
    return values


def _rwkv_kernel(xs_ref, dbias_ref, dup_ref, abias_ref, aup_ref, gup_ref,
                 kk_ref, ka_ref, rk_ref, lng_ref, lnb_ref, o_ref, state_ref):
    hp = 2 * RWKV_HEAD_SIZE
    tb = xs_ref.shape[1]
    gr = min(RWKV_GROUP, tb)
    ngroup = tb // gr
    prm = dict(dbias=dbias_ref, dup=dup_ref, abias=abias_ref, aup=aup_ref, gup=gup_ref,
               kk=kk_ref, ka=ka_ref, rk=rk_ref, lng=lng_ref, lnb=lnb_ref)

    @pl.when(pl.program_id(1) == 0)
    def _():
        state_ref[...] = jnp.zeros_like(state_ref)

    def prep(g):
        return _rwkv_prep(xs_ref[0, g * gr:(g + 1) * gr, :], prm)

    state = [state_ref[:, p * hp:(p + 1) * hp] for p in range(RWKV_WIDTH // hp)]
    (ops,) = _round_robin([prep(0)])
    done = _round_robin([_rwkv_solve(ops, prm)] + ([prep(1)] if ngroup > 1 else []))
    tail, ops = done[0], (done[1] if ngroup > 1 else None)
    for g in range(ngroup):
        def out_rows(ci, value, g=g):
            r0 = g * gr + ci * CHUNK
            o_ref[0, r0:r0 + CHUNK, :] = value.astype(o_ref.dtype)

        gens = [tail(state, out_rows)]
        if g + 1 < ngroup:
            gens.append(_rwkv_solve(ops, prm))
        if g + 2 < ngroup:
            gens.append(prep(g + 2))
        done = _round_robin(gens)
        tail = done[1] if g + 1 < ngroup else None
        ops = done[2] if g + 2 < ngroup else None
    for p in range(RWKV_WIDTH // hp):
        state_ref[:, p * hp:(p + 1) * hp] = state[p]


def _rwkv(rw, decay_bias, decay_up, aaa_bias, aaa_up, gate_up, k_k, k_a, r_k, ln_g, ln_b):
    b, t, _ = rw.shape
    tb = min(RWKV_TILE, t)
    w = RWKV_WIDTH

    def full(shape):
        return pl.BlockSpec(shape, lambda bi, ti: (0, 0))

    return pl.pallas_call(
        _rwkv_kernel,
        grid=(b, t // tb),
        in_specs=[
            pl.BlockSpec((1, tb, RWKV_IN_WIDTH), lambda bi, ti: (bi, ti, 0)),
            full((1, w)), full((LORA_DECAY, w)),
            full((1, w)), full((LORA_AAA, w)),
            full((LORA_GATE, w)),
            full((1, w)), full((1, w)), full((1, w)), full((1, w)), full((1, w)),
        ],
        out_specs=pl.BlockSpec((1, tb, w), lambda bi, ti: (bi, ti, 0)),
        out_shape=jax.ShapeDtypeStruct((b, t, w), BF16),
        scratch_shapes=[pltpu.VMEM((RWKV_HEAD_SIZE, w), F32)],
        compiler_params=pltpu.CompilerParams(
            dimension_semantics=("arbitrary", "arbitrary"), vmem_limit_bytes=VMEM_LIMIT),
        name="rwkv",
    )(rw, decay_bias, decay_up, aaa_bias, aaa_up, gate_up, k_k, k_a, r_k, ln_g, ln_b)


def _attn_merge_kernel(x_ref, q_ref, kv_ref, kvp_ref, rwkv_ref, g_ref, qg_ref, kg_ref, sink_ref,
                       wg_ref, wba_ref, wbr_ref, wo_ref, h_ref, attn_ref):
    x = x_ref[...]
    d = x.shape[1]
    nblk = q_ref.shape[1] // BLOCK
    units = [(j, kh) for j in range(nblk) for kh in range(ATTN_KV_HEADS)]
    n_u = len(units)
    gw = 2 * BLOCK
    n_g = (wg_ref.shape[1] - N_PROJ) // gw
    gate_parts = []

    def gate_slice():
        c0 = N_PROJ + len(gate_parts) * gw
        gate_parts.append(jnp.dot(ub, wg_ref[:, c0:c0 + gw], preferred_element_type=F32))

    y_rwkv = jnp.dot(rwkv_ref[...], wbr_ref[...], preferred_element_type=F32)
    ub = _rms(x, g_ref[...]).astype(BF16)
    gate_slice()
    kv_all = jnp.concatenate([kvp_ref[0], kv_ref[0]], axis=0)
    kdup, v_t = _attn_prep_kv(kv_all, qg_ref[...], kg_ref[...])
    gate_slice()
    qb, inv_rms = _attn_prep_q(q_ref[0])
    first_tile = pl.program_id(1) == 0
    scores = [_attn_scores(j, kh, qb, kdup, inv_rms, first_tile if j == 0 else None)
              for j, kh in units]
    soft = []
    for idx, (_, kh) in enumerate(units):
        while len(gate_parts) < 2 + (idx + 1) * (n_g - 2) // n_u:
            gate_slice()
        soft.append(_attn_softmax(kh, *scores[idx], sink_ref))
    while len(gate_parts) < n_g:
        gate_slice()
    gates = jnp.concatenate(gate_parts, axis=1)
    for unit, sm in zip(units, soft):
        _attn_values(*unit, *sm, v_t, attn_ref)
    y_attn = lax.dot_general(attn_ref[...], wba_ref[...], _TN, preferred_element_type=F32)
    mixed = _sigmoid(gates[:, :d]) * y_attn + _sigmoid(gates[:, d:]) * y_rwkv
    h_ref[...] = x + _bdot(mixed, wo_ref[...])


def _attn_merge(x2, q, kv, rwkv, gain, q_gain, k_gain, sinks, w_in, w_ba, w_br, w_out):
    n, d = x2.shape
    b, t, _ = q.shape
    tq = min(ATTN_TILE, t)
    nblk = tq // BLOCK
    nt = t // tq
    q_gain = jnp.tile(q_gain * (HEAD_DIM ** -0.5), (1, ATTN_KV_HEADS))
    k_gain = jnp.tile(k_gain, (1, ATTN_KV_HEADS))

    def full(a):
        return pl.BlockSpec(a.shape, lambda bi, i: (0, 0), pipeline_mode=pl.Buffered(1))

    def rows(width):
        return pl.BlockSpec((tq, width), lambda bi, i: (bi * nt + i, 0))

    return pl.pallas_call(
        _attn_merge_kernel,
        grid=(b, nt),
        in_specs=[
            rows(d),
            pl.BlockSpec((1, tq, ATTN_Q_WIDTH), lambda bi, i: (bi, i, 0)),
            pl.BlockSpec((1, tq, 2 * ATTN_KV_WIDTH), lambda bi, i: (bi, i, 0)),
            pl.BlockSpec((1, BLOCK, 2 * ATTN_KV_WIDTH),
                         lambda bi, i: (bi, jnp.maximum(i * nblk - 1, 0), 0)),
            rows(rwkv.shape[1]),
            full(gain), full(q_gain), full(k_gain),
            pl.BlockSpec(memory_space=pltpu.SMEM),
            full(w_in), full(w_ba), full(w_br), full(w_out),
        ],
        out_specs=rows(d),
        out_shape=jax.ShapeDtypeStruct((n, d), F32),
        scratch_shapes=[pltpu.VMEM((ATTN_Q_WIDTH, tq), BF16)],
        compiler_params=pltpu.CompilerParams(
            dimension_semantics=("arbitrary", "arbitrary"), vmem_limit_bytes=VMEM_LIMIT),
        name="attn_merge",
    )(x2, q, kv, kv, rwkv, gain, q_gain, k_gain, sinks, w_in, w_ba, w_br, w_out)


def _mlp_kernel(h_ref, g_ref, w1_ref, w2_ref, o_ref):
    h = h_ref[...]
    hidden = jnp.square(jnp.maximum(_bdot(_rms(h, g_ref[...]), w1_ref[...]), 0.0))
    o_ref[...] = h + _bdot(hidden, w2_ref[...])


def _mlp(h2, gain, w1, w2):
    n, d = h2.shape
    tm = min(ROW_TILE, n)

    def full(a):
        return pl.BlockSpec(a.shape, lambda i: (0, 0), pipeline_mode=pl.Buffered(1))

    return pl.pallas_call(
        _mlp_kernel,
        grid=(n // tm,),
        in_specs=[pl.BlockSpec((tm, d), lambda i: (i, 0)), full(gain), full(w1), full(w2)],
        out_specs=pl.BlockSpec((tm, d), lambda i: (i, 0)),
        out_shape=jax.ShapeDtypeStruct((n, d), F32),
        compiler_params=pltpu.CompilerParams(
            dimension_semantics=("arbitrary",), vmem_limit_bytes=VMEM_LIMIT),
        name="mlp",
    )(h2, gain, w1, w2)


def _layer(h, norm1_gain, w_in, q_norm_gain, k_norm_gain, attn_sinks,
           mu_r, mu_k, mu_v, mu_w, mu_a, mu_g, decay_bias, decay_up, aaa_bias, aaa_up,
           gate_up, k_k, k_a, r_k, ln_x_gain, ln_x_bias, w_branch_attn, w_branch_rwkv,
           w_out, norm2_gain, w_ff_in, w_ff_out):
    b, t, d = h.shape
    x2 = h.reshape(b * t, d)
    row = lambda a: a.reshape(1, -1).astype(F32)
    w_in_b = w_in.astype(BF16)

    mu = jnp.concatenate([mu_r, mu_k, mu_v, mu_w, mu_a, mu_g]).reshape(1, -1).astype(F32)
    q, kv, rw = _inproj(x2, row(norm1_gain), w_in_b, mu, t)
    rwkv = _rwkv(rw.reshape(b, t, -1), row(decay_bias), decay_up.astype(BF16),
                 row(aaa_bias), aaa_up.astype(BF16), gate_up.astype(BF16),
                 row(k_k), row(k_a), row(r_k), row(ln_x_gain), row(ln_x_bias))
    h2 = _attn_merge(x2, q.reshape(b, t, -1), kv.reshape(b, t, -1), rwkv.reshape(b * t, -1),
                     row(norm1_gain), row(q_norm_gain), row(k_norm_gain), attn_sinks.astype(F32),
                     w_in_b, w_branch_attn.astype(BF16), w_branch_rwkv.astype(BF16),
                     w_out.astype(BF16))
    out = _mlp(h2, row(norm2_gain), w_ff_in.astype(BF16), w_ff_out.astype(BF16))
    return out.reshape(b, t, d)


def kernel(x, norm1_gain, w_in, q_norm_gain, k_norm_gain, attn_sinks, mu_r, mu_k, mu_v, mu_w, mu_a, mu_g, decay_bias, decay_up, aaa_bias, aaa_up, gate_up, k_k, k_a, r_k, ln_x_gain, ln_x_bias, w_branch_attn, w_branch_rwkv, w_out, norm2_gain, w_ff_in, w_ff_out):
    h = x.astype(F32)
    params = (norm1_gain, w_in, q_norm_gain, k_norm_gain, attn_sinks, mu_r, mu_k, mu_v, mu_w,
              mu_a, mu_g, decay_bias, decay_up, aaa_bias, aaa_up, gate_up, k_k, k_a, r_k,
              ln_x_gain, ln_x_bias, w_branch_attn, w_branch_rwkv, w_out, norm2_gain,
              w_ff_in, w_ff_out)
    for l in range(norm1_gain.shape[0]):
        h = _layer(h, *(p[l] for p in params))
    return h.astype(x.dtype)
```

```python
import functools
import math

import jax
import jax.numpy as jnp
from jax import lax
from jax.experimental import pallas as pl
from jax.experimental.pallas import tpu as pltpu

F32 = jnp.float32
BF16 = jnp.bfloat16

HEAD_DIM = 64
ATTN_Q_HEADS = 8
ATTN_KV_HEADS = 2
ATTN_GROUP = ATTN_Q_HEADS // ATTN_KV_HEADS
WINDOW = 128
BLOCK = 128
RWKV_HEADS = 8
RWKV_HEAD_SIZE = 64
LORA_DECAY = 64
LORA_AAA = 64
LORA_GATE = 128
ATTN_Q_WIDTH = ATTN_Q_HEADS * HEAD_DIM
ATTN_KV_WIDTH = ATTN_KV_HEADS * HEAD_DIM
RWKV_WIDTH = RWKV_HEADS * RWKV_HEAD_SIZE
RWKV_IN_WIDTH = 3 * RWKV_WIDTH + LORA_DECAY + LORA_AAA + LORA_GATE
N_PROJ = ATTN_Q_WIDTH + 2 * ATTN_KV_WIDTH + RWKV_IN_WIDTH
RMS_EPS = 1e-6
GN_EPS = 64e-5
L2_EPS = 1e-12

CHUNK = 64
ROW_TILE = 512
ATTN_TILE = 1024
RWKV_TILE = 1024
RWKV_GROUP = 256
MLP_CHUNK = 1024
VMEM_LIMIT = 56 * 1024 * 1024

_NT = (((1,), (1,)), ((), ()))
_TN = (((0,), (0,)), ((), ()))


def _bdot(a, b):
    return jnp.dot(a.astype(BF16), b.astype(BF16), preferred_element_type=F32)


def _rms(x, gain):
    return x * lax.rsqrt(jnp.mean(x * x, axis=-1, keepdims=True) + RMS_EPS) * gain


def _sigmoid(x):
    return 0.5 * jnp.tanh(0.5 * x) + 0.5


def _inproj_kernel(x_ref, g_ref, w_ref, mu_ref, q_ref, kv_ref, rw_ref, last_ref, *, tiles_per_seq):
    u = _rms(x_ref[...], g_ref[...])
    p = _bdot(u, w_ref[...])
    q_ref[...] = p[:, :ATTN_Q_WIDTH]
    kv_ref[...] = p[:, ATTN_Q_WIDTH:ATTN_Q_WIDTH + 2 * ATTN_KV_WIDTH]
    rw = p[:, ATTN_Q_WIDTH + 2 * ATTN_KV_WIDTH:]
    tm = rw.shape[0]
    seq_start = pl.program_id(0) % tiles_per_seq == 0
    before = jnp.where(seq_start, 0.0, last_ref[...])
    first = lax.broadcasted_iota(jnp.int32, (tm, 1), 0) == 0
    prev = jnp.where(first, before, pltpu.roll(rw, 1, 0))
    last_ref[...] = rw[tm - 1:tm, :]
    rw_ref[...] = rw + (prev - rw) * mu_ref[...]


def _inproj(x2, gain, w, mu, seq_len):
    n, d = x2.shape
    wid = N_PROJ
    tm = min(2 * ROW_TILE, seq_len)

    def full(shape):
        return pl.BlockSpec(shape, lambda i: (0, 0), pipeline_mode=pl.Buffered(1))

    return pl.pallas_call(
        functools.partial(_inproj_kernel, tiles_per_seq=seq_len // tm),
        grid=(n // tm,),
        in_specs=[
            pl.BlockSpec((tm, d), lambda i: (i, 0)),
            full((1, d)), full((d, wid)), full((1, RWKV_IN_WIDTH)),
        ],
        out_specs=[
            pl.BlockSpec((tm, ATTN_Q_WIDTH), lambda i: (i, 0)),
            pl.BlockSpec((tm, 2 * ATTN_KV_WIDTH), lambda i: (i, 0)),
            pl.BlockSpec((tm, RWKV_IN_WIDTH), lambda i: (i, 0)),
        ],
        out_shape=[
            jax.ShapeDtypeStruct((n, ATTN_Q_WIDTH), F32),
            jax.ShapeDtypeStruct((n, 2 * ATTN_KV_WIDTH), F32),
            jax.ShapeDtypeStruct((n, RWKV_IN_WIDTH), F32),
        ],
        scratch_shapes=[pltpu.VMEM((1, RWKV_IN_WIDTH), F32)],
        compiler_params=pltpu.CompilerParams(
            dimension_semantics=("arbitrary",), vmem_limit_bytes=VMEM_LIMIT),
        name="inproj",
    )(x2, gain, w, mu)


def _head_sumsq(x, ones_bd):
    x2 = x * x
    hi = x2.astype(BF16)
    lo = (x2 - hi.astype(F32)).astype(BF16)
    return (jnp.dot(hi, ones_bd, preferred_element_type=F32)
            + jnp.dot(lo, ones_bd, preferred_element_type=F32))


def _attn_prep_kv(kv_all, q_gain, k_gain):
    hp = 2 * HEAD_DIM
    even = lax.broadcasted_iota(jnp.int32, (1, hp), 1) < HEAD_DIM
    ri = lax.broadcasted_iota(jnp.int32, (hp, hp), 0)
    ci = lax.broadcasted_iota(jnp.int32, (hp, hp), 1)
    ones_bd = ((ri < HEAD_DIM) == (ci < HEAD_DIM)).astype(BF16)
    k = kv_all[:, :hp]
    kn = k * lax.rsqrt(_head_sumsq(k, ones_bd) * (1.0 / HEAD_DIM) + RMS_EPS) * k_gain * q_gain
    kn_sw = pltpu.roll(kn, HEAD_DIM, 1)
    kdup = [jnp.where(even, kn, kn_sw).astype(BF16), jnp.where(even, kn_sw, kn).astype(BF16)]
    v_t = kv_all[:, hp:].T.astype(BF16)
    return kdup, v_t


def _attn_prep_q(q):
    hp = 2 * HEAD_DIM
    si = lax.broadcasted_iota(jnp.int32, (16, hp), 0)
    sj = lax.broadcasted_iota(jnp.int32, (16, hp), 1)
    sel = (((si == 0) & (sj < HEAD_DIM)) | ((si == 1) & (sj >= HEAD_DIM))).astype(BF16)
    q2 = q * q
    hi = q2.astype(BF16)
    lo = (q2 - hi.astype(F32)).astype(BF16)
    inv_rms = []
    for m in range(ATTN_Q_WIDTH // hp):
        cs = slice(m * hp, (m + 1) * hp)
        ss = (lax.dot_general(sel, hi[:, cs], _NT, preferred_element_type=F32)
              + lax.dot_general(sel, lo[:, cs], _NT, preferred_element_type=F32))
        inv_rms.append(lax.rsqrt(ss * (1.0 / HEAD_DIM) + RMS_EPS))
    return q.astype(BF16), inv_rms


def _attn_scores(j, kh, qb, kdup, inv_rms, no_prev):
    hp = 2 * HEAD_DIM
    even = lax.broadcasted_iota(jnp.int32, (1, hp), 1) < HEAD_DIM
    zero_q = jnp.zeros((BLOCK, hp), BF16)
    parts, scale = [], []
    for g in range(ATTN_GROUP):
        h = kh * ATTN_GROUP + g
        col = qb[j * BLOCK:(j + 1) * BLOCK, (h // 2) * hp:(h // 2 + 1) * hp]
        parts.append(jnp.where(even, col, zero_q) if h % 2 == 0 else jnp.where(even, zero_q, col))
        scale.append(inv_rms[h // 2][h % 2:h % 2 + 1, j * BLOCK:(j + 1) * BLOCK])
    s_t = lax.dot_general(kdup[kh][j * BLOCK:(j + 2) * BLOCK], jnp.concatenate(parts, axis=0), _NT,
                          preferred_element_type=F32)
    key = lax.broadcasted_iota(jnp.int32, (BLOCK, ATTN_GROUP * BLOCK), 0)
    qry = lax.broadcasted_iota(jnp.int32, (BLOCK, ATTN_GROUP * BLOCK), 1) & (BLOCK - 1)
    use_prev = key > qry
    prev = s_t[:BLOCK]
    if no_prev is not None:
        prev = jnp.where(no_prev, -jnp.inf, prev)
    folded = jnp.where(use_prev, prev, s_t[BLOCK:]) * jnp.concatenate(scale, axis=1)
    return folded, use_prev


def _attn_softmax(kh, folded, use_prev, sink_ref):
    sink = jnp.concatenate(
        [jnp.full((1, BLOCK), sink_ref[kh * ATTN_GROUP + g], F32) for g in range(ATTN_GROUP)], axis=1)
    m = jnp.maximum(jnp.max(folded, axis=0, keepdims=True), sink)
    p = jnp.exp(folded - m)
    den = jnp.sum(p, axis=0, keepdims=True) + jnp.exp(sink - m)
    zero = jnp.zeros_like(p)
    p_t = jnp.concatenate([jnp.where(use_prev, p, zero), jnp.where(use_prev, zero, p)], axis=0)
    return p_t.astype(BF16), 1.0 / den


def _attn_values(j, kh, p_t, inv_den, v_t, o_ref):
    o_t = jnp.dot(v_t[kh * HEAD_DIM:(kh + 1) * HEAD_DIM, j * BLOCK:(j + 2) * BLOCK], p_t,
                  preferred_element_type=F32) * inv_den
    for g in range(ATTN_GROUP):
        h = kh * ATTN_GROUP + g
        o_ref[h * HEAD_DIM:(h + 1) * HEAD_DIM, j * BLOCK:(j + 1) * BLOCK] = (
            o_t[:, g * BLOCK:(g + 1) * BLOCK].astype(o_ref.dtype))


def _pair_blockdiag(x, even):
    zero = jnp.zeros_like(x)
    return jnp.concatenate([jnp.where(even, x, zero), jnp.where(even, zero, x)], axis=0)


def _pair_sum(x, even):
    s_e = jnp.sum(jnp.where(even, x, 0.0), axis=-1, keepdims=True)
    s_o = jnp.sum(jnp.where(even, 0.0, x), axis=-1, keepdims=True)
    return jnp.where(even, s_e, s_o)


def _rwkv_prep(xs, prm):
    c = CHUNK
    hp = 2 * RWKV_HEAD_SIZE
    w = RWKV_WIDTH
    rows = xs.shape[0]
    nc = rows // c
    cols = [slice(p * hp, (p + 1) * hp) for p in range(w // hp)]
    r = xs[:, 0:w]
    k = xs[:, w:2 * w]
    v = xs[:, 2 * w:3 * w]
    w_lora = xs[:, 3 * w:3 * w + LORA_DECAY]
    a_lora = xs[:, 3 * w + LORA_DECAY:3 * w + LORA_DECAY + LORA_AAA]
    g_lora = xs[:, 3 * w + LORA_DECAY + LORA_AAA:]

    d = prm["dbias"][...] + _bdot(jnp.tanh(w_lora), prm["dup"][...])
    lw = (-math.exp(-0.5)) * _sigmoid(d)
    a_sig = _sigmoid(prm["abias"][...] + _bdot(a_lora, prm["aup"][...]))
    gate = _bdot(_sigmoid(g_lora), prm["gup"][...])
    yield

    kkr = k * prm["kk"][...]
    ka = prm["ka"][...]
    k2 = k * (a_sig * ka + (1.0 - ka))
    rk = r * k2 * prm["rk"][...]
    even_t = lax.broadcasted_iota(jnp.int32, (rows, hp), 1) < RWKV_HEAD_SIZE
    den = jnp.concatenate(
        [jnp.maximum(jnp.sqrt(_pair_sum(kkr[:, cs] * kkr[:, cs], even_t)), L2_EPS) for cs in cols],
        axis=1)
    bonus = jnp.concatenate([_pair_sum(rk[:, cs], even_t) for cs in cols], axis=1) * v
    kk = kkr / den
    a_s = -kk
    b_s = kk * a_sig
    yield

    ti = lax.broadcasted_iota(jnp.int32, (c, 3 * c), 0)
    si = lax.broadcasted_iota(jnp.int32, (c, 3 * c), 1) & (c - 1)
    lower3 = (ti >= si).astype(BF16)
    lw_hi = lw.astype(BF16)
    lw_r = lw - lw_hi.astype(F32)
    lw_mid = lw_r.astype(BF16)
    lw_lo = (lw_r - lw_mid.astype(F32)).astype(BF16)
    cl = jnp.concatenate(
        [jnp.dot(lower3, jnp.concatenate([z[ci * c:(ci + 1) * c] for z in (lw_hi, lw_mid, lw_lo)], axis=0),
                 preferred_element_type=F32) for ci in range(nc)], axis=0)
    yield

    e_c = jnp.exp(cl)
    e_neg = 1.0 / e_c
    g_rows = [e_c[ci * c + c - 1:ci * c + c, :] for ci in range(nc)]
    g_full = jnp.concatenate([jnp.broadcast_to(g, (c, w)) for g in g_rows], axis=0)
    at = a_s * jnp.exp(cl - lw)
    rt = r * e_c
    yield
    bt = b_s * e_neg
    kt = k2 * e_neg
    ops = dict(rt=rt, g_rows=g_rows, gate=gate, bonus=bonus,
               bh=(bt * g_full).astype(BF16), khat=(kt * g_full).astype(BF16),
               at_b=at.astype(BF16), rt_b=rt.astype(BF16), bt_b=bt.astype(BF16),
               kt_b=kt.astype(BF16), v_b=v.astype(BF16))
    return ops


def _rwkv_solve(ops, prm):
    c = CHUNK
    hs = RWKV_HEAD_SIZE
    assert c == hs
    hp = 2 * hs
    w = RWKV_WIDTH
    npair = w // hp
    nc = ops["rt"].shape[0] // c
    cols = [slice(p * hp, (p + 1) * hp) for p in range(npair)]
    even = lax.broadcasted_iota(jnp.int32, (c, hp), 1) < RWKV_HEAD_SIZE
    gi = lax.broadcasted_iota(jnp.int32, (2 * c, 2 * hp), 0)
    g_t = gi & (c - 1)
    g_s = lax.broadcasted_iota(jnp.int32, (2 * c, 2 * hp), 1) & (c - 1)
    g_mask = (g_t > g_s) | ((gi >= c) & (g_t == g_s))
    eye2 = (lax.broadcasted_iota(jnp.int32, (c, hp), 0)
            == (lax.broadcasted_iota(jnp.int32, (c, hp), 1) & (c - 1)))
    units = [(ci, p) for ci in range(nc) for p in range(npair)]

    def col(name, ci, p):
        return ops[name][ci * c:(ci + 1) * c, cols[p]]

    def bd(x):
        return _pair_blockdiag(x, even)

    def lanes(*xs):
        return jnp.concatenate(xs, axis=1)

    def rows(*xs):
        return jnp.concatenate(xs, axis=0)

    def mm(a, b):
        return jnp.dot(a, b, preferred_element_type=F32)

    gm = [jnp.where(g_mask,
                    lax.dot_general(rows(col("at_b", ci, p), col("rt_b", ci, p)),
                                    rows(bd(col("bt_b", ci, p)), bd(col("kt_b", ci, p))), _NT,
                                    preferred_element_type=F32), 0.0)
          for ci, p in units]
    yield
    a_pow = [g[:c, :hp].astype(BF16) for g in gm]
    t_inv = [jnp.where(eye2, 1.0, g[:c, :hp]) for g in gm]
    a_pow = [mm(a, bd(a)).astype(BF16) for a in a_pow]
    v_st = [col("v_b", ci, p) for ci, p in units]
    akv = [mm(g[:c, hp:].astype(BF16), bd(v)).astype(BF16) for g, v in zip(gm, v_st)]
    yield
    for _ in range(int(math.log2(c)) - 2):
        res = [mm(a, lanes(bd(t.astype(BF16)), bd(a))) for a, t in zip(a_pow, t_inv)]
        t_inv = [t + z[:, :hp] for t, z in zip(t_inv, res)]
        a_pow = [z[:, hp:].astype(BF16) for z in res]
        yield
    t_inv = [t + mm(a, bd(t.astype(BF16))) for a, t in zip(a_pow, t_inv)]
    yield
    wu = [mm(t.astype(BF16), lanes(bd(col("at_b", ci, p)), bd(x)))
          for (ci, p), t, x in zip(units, t_inv, akv)]
    w_b = [z[:, :hp].astype(BF16) for z in wu]
    u0_b = [z[:, hp:].astype(BF16) for z in wu]
    yield
    def head_t(x):
        xt = x.T
        return lanes(xt[:hs], xt[hs:]).astype(BF16)

    ab = [mm(rows(g[c:, :].astype(BF16), lanes(head_t(col("bh", ci, p)), head_t(col("khat", ci, p)))),
             rows(lanes(bd(wb), bd(ub)), lanes(jnp.zeros((hp, hp), BF16), bd(v))))
          for (ci, p), g, wb, ub, v in zip(units, gm, w_b, u0_b, v_st)]
    ab_y = [z[:c] for z in ab]
    ab_h = [z[c:] for z in ab]
    yield

    def tail(state, out_rows):
        for ci in range(nc):
            y_cols = []
            for p in range(npair):
                idx = ci * npair + p
                qmat = col("rt", ci, p) + ab_y[idx][:, :hp]
                g_diag = jnp.where(eye2, jnp.broadcast_to(ops["g_rows"][ci][:, cols[p]], (hs, hp)), 0.0)
                mmat = g_diag + ab_h[idx][:, :hp]
                ys = mm(rows(qmat, mmat).astype(BF16), bd(state[p].astype(BF16)))
                y = ys[:c] + ab_y[idx][:, hp:]
                state[p] = ys[c:] + ab_h[idx][:, hp:]
                mean = _pair_sum(y, even) * (1.0 / RWKV_HEAD_SIZE)
                yc = y - mean
                var = _pair_sum(yc * yc, even) * (1.0 / RWKV_HEAD_SIZE)
                y_cols.append(yc * lax.rsqrt(var + GN_EPS))
            yn = jnp.concatenate(y_cols, axis=1)
            rs = slice(ci * c, (ci + 1) * c)
            out_rows(ci, (yn * prm["lng"][...] + prm["lnb"][...] + ops["bonus"][rs]) * ops["gate"][rs])
            yield

    return tail


def _round_robin(gens):
    values = [None] * len(gens)
    live = list(range(len(gens)))
    while live:
        for i in list(live):
            try:
                next(gens[i])
            except StopIteration as stop:
                values[i] = stop.value
                live.remove(i)
    return values


def _rwkv_kernel(xs_ref, dbias_ref, dup_ref, abias_ref, aup_ref, gup_ref,
                 kk_ref, ka_ref, rk_ref, lng_ref, lnb_ref, o_ref, state_ref):
    hp = 2 * RWKV_HEAD_SIZE
    tb = xs_ref.shape[1]
    gr = min(RWKV_GROUP, tb)
    ngroup = tb // gr
    prm = dict(dbias=dbias_ref, dup=dup_ref, abias=abias_ref, aup=aup_ref, gup=gup_ref,
               kk=kk_ref, ka=ka_ref, rk=rk_ref, lng=lng_ref, lnb=lnb_ref)

    @pl.when(pl.program_id(1) == 0)
    def _():
        state_ref[...] = jnp.zeros_like(state_ref)

    def prep(g):
        return _rwkv_prep(xs_ref[0, g * gr:(g + 1) * gr, :], prm)

    state = [state_ref[:, p * hp:(p + 1) * hp] for p in range(RWKV_WIDTH // hp)]
    (ops,) = _round_robin([prep(0)])
    done = _round_robin([_rwkv_solve(ops, prm)] + ([prep(1)] if ngroup > 1 else []))
    tail, ops = done[0], (done[1] if ngroup > 1 else None)
    for g in range(ngroup):
        def out_rows(ci, value, g=g):
            r0 = g * gr + ci * CHUNK
            o_ref[0, r0:r0 + CHUNK, :] = value.astype(o_ref.dtype)

        gens = [tail(state, out_rows)]
        if g + 1 < ngroup:
            gens.append(_rwkv_solve(ops, prm))
        if g + 2 < ngroup:
            gens.append(prep(g + 2))
        done = _round_robin(gens)
        tail = done[1] if g + 1 < ngroup else None
        ops = done[2] if g + 2 < ngroup else None
    for p in range(RWKV_WIDTH // hp):
        state_ref[:, p * hp:(p + 1) * hp] = state[p]


def _rwkv(rw, decay_bias, decay_up, aaa_bias, aaa_up, gate_up, k_k, k_a, r_k, ln_g, ln_b):
    b, t, _ = rw.shape
    tb = min(RWKV_TILE, t)
    w = RWKV_WIDTH

    def full(shape):
        return pl.BlockSpec(shape, lambda bi, ti: (0, 0))

    return pl.pallas_call(
        _rwkv_kernel,
        grid=(b, t // tb),
        in_specs=[
            pl.BlockSpec((1, tb, RWKV_IN_WIDTH), lambda bi, ti: (bi, ti, 0)),
            full((1, w)), full((LORA_DECAY, w)),
            full((1, w)), full((LORA_AAA, w)),
            full((LORA_GATE, w)),
            full((1, w)), full((1, w)), full((1, w)), full((1, w)), full((1, w)),
        ],
        out_specs=pl.BlockSpec((1, tb, w), lambda bi, ti: (bi, ti, 0)),
        out_shape=jax.ShapeDtypeStruct((b, t, w), BF16),
        scratch_shapes=[pltpu.VMEM((RWKV_HEAD_SIZE, w), F32)],
        compiler_params=pltpu.CompilerParams(
            dimension_semantics=("arbitrary", "arbitrary"), vmem_limit_bytes=VMEM_LIMIT),
        name="rwkv",
    )(rw, decay_bias, decay_up, aaa_bias, aaa_up, gate_up, k_k, k_a, r_k, ln_g, ln_b)


def _attn_merge_kernel(x_ref, q_ref, kv_ref, kvp_ref, rwkv_ref, g_ref, qg_ref, kg_ref, sink_ref,
                       wg_ref, wba_ref, wbr_ref, wo_ref, h_ref, attn_ref):
    x = x_ref[...]
    d = x.shape[1]
    nblk = q_ref.shape[1] // BLOCK
    units = [(j, kh) for j in range(nblk) for kh in range(ATTN_KV_HEADS)]
    n_u = len(units)
    gw = 2 * BLOCK
    n_g = (wg_ref.shape[1] - N_PROJ) // gw
    gate_parts = []

    def gate_slice():
        c0 = N_PROJ + len(gate_parts) * gw
        gate_parts.append(jnp.dot(ub, wg_ref[:, c0:c0 + gw], preferred_element_type=F32))

    y_rwkv = jnp.dot(rwkv_ref[...], wbr_ref[...], preferred_element_type=F32)
    ub = _rms(x, g_ref[...]).astype(BF16)
    gate_slice()
    kv_all = jnp.concatenate([kvp_ref[0], kv_ref[0]], axis=0)
    kdup, v_t = _attn_prep_kv(kv_all, qg_ref[...], kg_ref[...])
    gate_slice()
    qb, inv_rms = _attn_prep_q(q_ref[0])
    first_tile = pl.program_id(1) == 0
    scores = [_attn_scores(j, kh, qb, kdup, inv_rms, first_tile if j == 0 else None)
              for j, kh in units]
    soft = []
    for idx, (_, kh) in enumerate(units):
        while len(gate_parts) < 2 + (idx + 1) * (n_g - 2) // n_u:
            gate_slice()
        soft.append(_attn_softmax(kh, *scores[idx], sink_ref))
    while len(gate_parts) < n_g:
        gate_slice()
    gates = jnp.concatenate(gate_parts, axis=1)
    for unit, sm in zip(units, soft):
        _attn_values(*unit, *sm, v_t, attn_ref)
    y_attn = lax.dot_general(attn_ref[...], wba_ref[...], _TN, preferred_element_type=F32)
    mixed = _sigmoid(gates[:, :d]) * y_attn + _sigmoid(gates[:, d:]) * y_rwkv
    h_ref[...] = x + _bdot(mixed, wo_ref[...])


def _attn_merge(x2, q, kv, rwkv, gain, q_gain, k_gain, sinks, w_in, w_ba, w_br, w_out):
    n, d = x2.shape
    b, t, _ = q.shape
    tq = min(ATTN_TILE, t)
    nblk = tq // BLOCK
    nt = t // tq
    q_gain = jnp.tile(q_gain * (HEAD_DIM ** -0.5), (1, ATTN_KV_HEADS))
    k_gain = jnp.tile(k_gain, (1, ATTN_KV_HEADS))

    def full(a):
        return pl.BlockSpec(a.shape, lambda bi, i: (0, 0), pipeline_mode=pl.Buffered(1))

    def rows(width):
        return pl.BlockSpec((tq, width), lambda bi, i: (bi * nt + i, 0))

    return pl.pallas_call(
        _attn_merge_kernel,
        grid=(b, nt),
        in_specs=[
            rows(d),
            pl.BlockSpec((1, tq, ATTN_Q_WIDTH), lambda bi, i: (bi, i, 0)),
            pl.BlockSpec((1, tq, 2 * ATTN_KV_WIDTH), lambda bi, i: (bi, i, 0)),
            pl.BlockSpec((1, BLOCK, 2 * ATTN_KV_WIDTH),
                         lambda bi, i: (bi, jnp.maximum(i * nblk - 1, 0), 0)),
            rows(rwkv.shape[1]),
            full(gain), full(q_gain), full(k_gain),
            pl.BlockSpec(memory_space=pltpu.SMEM),
            full(w_in), full(w_ba), full(w_br), full(w_out),
        ],
        out_specs=rows(d),
        out_shape=jax.ShapeDtypeStruct((n, d), F32),
        scratch_shapes=[pltpu.VMEM((ATTN_Q_WIDTH, tq), BF16)],
        compiler_params=pltpu.CompilerParams(
            dimension_semantics=("arbitrary", "arbitrary"), vmem_limit_bytes=VMEM_LIMIT),
        name="attn_merge",
    )(x2, q, kv, kv, rwkv, gain, q_gain, k_gain, sinks, w_in, w_ba, w_br, w_out)


def _mlp_kernel(h_ref, g_ref, w1_ref, w2_ref, o_ref):
    h = h_ref[...]
    u = _rms(h, g_ref[...]).astype(BF16)
    d_ff = w1_ref.shape[1]
    acc = h
    for c0 in range(0, d_ff, MLP_CHUNK):
        hidden = jnp.square(jnp.maximum(
            jnp.dot(u, w1_ref[:, c0:c0 + MLP_CHUNK], preferred_element_type=F32), 0.0))
        acc = acc + jnp.dot(hidden.astype(BF16), w2_ref[c0:c0 + MLP_CHUNK, :],
                            preferred_element_type=F32)
    o_ref[...] = acc


def _mlp(h2, gain, w1, w2):
    n, d = h2.shape
    tm = min(2 * ROW_TILE, n)

    def full(a):
        return pl.BlockSpec(a.shape, lambda i: (0, 0), pipeline_mode=pl.Buffered(1))

    return pl.pallas_call(
        _mlp_kernel,
        grid=(n // tm,),
        in_specs=[pl.BlockSpec((tm, d), lambda i: (i, 0)), full(gain), full(w1), full(w2)],
        out_specs=pl.BlockSpec((tm, d), lambda i: (i, 0)),
        out_shape=jax.ShapeDtypeStruct((n, d), F32),
        compiler_params=pltpu.CompilerParams(
            dimension_semantics=("arbitrary",), vmem_limit_bytes=VMEM_LIMIT),
        name="mlp",
    )(h2, gain, w1, w2)


def _layer(h, norm1_gain, w_in, q_norm_gain, k_norm_gain, attn_sinks,
           mu_r, mu_k, mu_v, mu_w, mu_a, mu_g, decay_bias, decay_up, aaa_bias, aaa_up,
           gate_up, k_k, k_a, r_k, ln_x_gain, ln_x_bias, w_branch_attn, w_branch_rwkv,
           w_out, norm2_gain, w_ff_in, w_ff_out):
    b, t, d = h.shape
    x2 = h.reshape(b * t, d)
    row = lambda a: a.reshape(1, -1).astype(F32)
    w_in_b = w_in.astype(BF16)

    mu = jnp.concatenate([mu_r, mu_k, mu_v, mu_w, mu_a, mu_g]).reshape(1, -1).astype(F32)
    q, kv, rw = _inproj(x2, row(norm1_gain), w_in_b, mu, t)
    rwkv = _rwkv(rw.reshape(b, t, -1), row(decay_bias), decay_up.astype(BF16),
                 row(aaa_bias), aaa_up.astype(BF16), gate_up.astype(BF16),
                 row(k_k), row(k_a), row(r_k), row(ln_x_gain), row(ln_x_bias))
    h2 = _attn_merge(x2, q.reshape(b, t, -1), kv.reshape(b, t, -1), rwkv.reshape(b * t, -1),
                     row(norm1_gain), row(q_norm_gain), row(k_norm_gain), attn_sinks.astype(F32),
                     w_in_b, w_branch_attn.astype(BF16), w_branch_rwkv.astype(BF16),
                     w_out.astype(BF16))
    out = _mlp(h2, row(norm2_gain), w_ff_in.astype(BF16), w_ff_out.astype(BF16))
    return out.reshape(b, t, d)


def kernel(x, norm1_gain, w_in, q_norm_gain, k_norm_gain, attn_sinks, mu_r, mu_k, mu_v, mu_w, mu_a, mu_g, decay_bias, decay_up, aaa_bias, aaa_up, gate_up, k_k, k_a, r_k, ln_x_gain, ln_x_bias, w_branch_attn, w_branch_rwkv, w_out, norm2_gain, w_ff_in, w_ff_out):
    h = x.astype(F32)
    params = (norm1_gain, w_in, q_norm_gain, k_norm_gain, attn_sinks, mu_r, mu_k, mu_v, mu_w,
              mu_a, mu_g, decay_bias, decay_up, aaa_bias, aaa_up, gate_up, k_k, k_a, r_k,
              ln_x_gain, ln_x_bias, w_branch_attn, w_branch_rwkv, w_out, norm2_gain,
              w_ff_in, w_ff_out)
    for l in range(norm1_gain.shape[0]):
        h = _layer(h, *(p[l] for p in params))
    return h.astype(x.dtype)
```

```python
import functools
import math

import jax
import jax.numpy as jnp
from jax import lax
from jax.experimental import pallas as pl
from jax.experimental.pallas import tpu as pltpu

F32 = jnp.float32
BF16 = jnp.bfloat16

HEAD_DIM = 64
ATTN_Q_HEADS = 8
ATTN_KV_HEADS = 2
ATTN_GROUP = ATTN_Q_HEADS // ATTN_KV_HEADS
WINDOW = 128
BLOCK = 128
RWKV_HEADS = 8
RWKV_HEAD_SIZE = 64
LORA_DECAY = 64
LORA_AAA = 64
LORA_GATE = 128
ATTN_Q_WIDTH = ATTN_Q_HEADS * HEAD_DIM
ATTN_KV_WIDTH = ATTN_KV_HEADS * HEAD_DIM
RWKV_WIDTH = RWKV_HEADS * RWKV_HEAD_SIZE
RWKV_IN_WIDTH = 3 * RWKV_WIDTH + LORA_DECAY + LORA_AAA + LORA_GATE
N_PROJ = ATTN_Q_WIDTH + 2 * ATTN_KV_WIDTH + RWKV_IN_WIDTH
RMS_EPS = 1e-6
GN_EPS = 64e-5
L2_EPS = 1e-12

CHUNK = 64
ROW_TILE = 512
ATTN_TILE = 1024
RWKV_TILE = 1024
RWKV_GROUP = 256
MLP_CHUNK = 1024
VMEM_LIMIT = 56 * 1024 * 1024

_NT = (((1,), (1,)), ((), ()))
_TN = (((0,), (0,)), ((), ()))


def _bdot(a, b):
    return jnp.dot(a.astype(BF16), b.astype(BF16), preferred_element_type=F32)


def _rms(x, gain):
    return x * lax.rsqrt(jnp.mean(x * x, axis=-1, keepdims=True) + RMS_EPS) * gain


def _sigmoid(x):
    return 0.5 * jnp.tanh(0.5 * x) + 0.5


def _inproj_kernel(x_ref, g_ref, w_ref, mu_ref, q_ref, kv_ref, rw_ref, last_ref, *, tiles_per_seq):
    u = _rms(x_ref[...], g_ref[...])
    p = _bdot(u, w_ref[...])
    q_ref[...] = p[:, :ATTN_Q_WIDTH]
    kv_ref[...] = p[:, ATTN_Q_WIDTH:ATTN_Q_WIDTH + 2 * ATTN_KV_WIDTH]
    rw = p[:, ATTN_Q_WIDTH + 2 * ATTN_KV_WIDTH:]
    tm = rw.shape[0]
    seq_start = pl.program_id(0) % tiles_per_seq == 0
    before = jnp.where(seq_start, 0.0, last_ref[...])
    first = lax.broadcasted_iota(jnp.int32, (tm, 1), 0) == 0
    prev = jnp.where(first, before, pltpu.roll(rw, 1, 0))
    last_ref[...] = rw[tm - 1:tm, :]
    rw_ref[...] = rw + (prev - rw) * mu_ref[...]


def _inproj(x2, gain, w, mu, seq_len):
    n, d = x2.shape
    wid = N_PROJ
    tm = min(2 * ROW_TILE, seq_len)

    def full(shape):
        return pl.BlockSpec(shape, lambda i: (0, 0), pipeline_mode=pl.Buffered(1))

    return pl.pallas_call(
        functools.partial(_inproj_kernel, tiles_per_seq=seq_len // tm),
        grid=(n // tm,),
        in_specs=[
            pl.BlockSpec((tm, d), lambda i: (i, 0)),
            full((1, d)), full((d, wid)), full((1, RWKV_IN_WIDTH)),
        ],
        out_specs=[
            pl.BlockSpec((tm, ATTN_Q_WIDTH), lambda i: (i, 0)),
            pl.BlockSpec((tm, 2 * ATTN_KV_WIDTH), lambda i: (i, 0)),
            pl.BlockSpec((tm, RWKV_IN_WIDTH), lambda i: (i, 0)),
        ],
        out_shape=[
            jax.ShapeDtypeStruct((n, ATTN_Q_WIDTH), F32),
            jax.ShapeDtypeStruct((n, 2 * ATTN_KV_WIDTH), F32),
            jax.ShapeDtypeStruct((n, RWKV_IN_WIDTH), F32),
        ],
        scratch_shapes=[pltpu.VMEM((1, RWKV_IN_WIDTH), F32)],
        compiler_params=pltpu.CompilerParams(
            dimension_semantics=("arbitrary",), vmem_limit_bytes=VMEM_LIMIT),
        name="inproj",
    )(x2, gain, w, mu)


def _head_sumsq(x, ones_bd):
    x2 = x * x
    hi = x2.astype(BF16)
    lo = (x2 - hi.astype(F32)).astype(BF16)
    return (jnp.dot(hi, ones_bd, preferred_element_type=F32)
            + jnp.dot(lo, ones_bd, preferred_element_type=F32))


def _attn_prep_kv(kv_all, q_gain, k_gain):
    hp = 2 * HEAD_DIM
    even = lax.broadcasted_iota(jnp.int32, (1, hp), 1) < HEAD_DIM
    ri = lax.broadcasted_iota(jnp.int32, (hp, hp), 0)
    ci = lax.broadcasted_iota(jnp.int32, (hp, hp), 1)
    ones_bd = ((ri < HEAD_DIM) == (ci < HEAD_DIM)).astype(BF16)
    k = kv_all[:, :hp]
    kn = k * lax.rsqrt(_head_sumsq(k, ones_bd) * (1.0 / HEAD_DIM) + RMS_EPS) * k_gain * q_gain
    kn_sw = pltpu.roll(kn, HEAD_DIM, 1)
    kdup = [jnp.where(even, kn, kn_sw).astype(BF16), jnp.where(even, kn_sw, kn).astype(BF16)]
    v_t = kv_all[:, hp:].T.astype(BF16)
    return kdup, v_t


def _attn_prep_q(q):
    hp = 2 * HEAD_DIM
    si = lax.broadcasted_iota(jnp.int32, (16, hp), 0)
    sj = lax.broadcasted_iota(jnp.int32, (16, hp), 1)
    sel = (((si == 0) & (sj < HEAD_DIM)) | ((si == 1) & (sj >= HEAD_DIM))).astype(BF16)
    q2 = q * q
    hi = q2.astype(BF16)
    lo = (q2 - hi.astype(F32)).astype(BF16)
    inv_rms = []
    for m in range(ATTN_Q_WIDTH // hp):
        cs = slice(m * hp, (m + 1) * hp)
        ss = (lax.dot_general(sel, hi[:, cs], _NT, preferred_element_type=F32)
              + lax.dot_general(sel, lo[:, cs], _NT, preferred_element_type=F32))
        inv_rms.append(lax.rsqrt(ss * (1.0 / HEAD_DIM) + RMS_EPS))
    return q.astype(BF16), inv_rms


def _attn_scores(j, kh, qb, kdup, inv_rms, no_prev):
    assert WINDOW == BLOCK
    hp = 2 * HEAD_DIM
    even = lax.broadcasted_iota(jnp.int32, (1, hp), 1) < HEAD_DIM
    zero_q = jnp.zeros((BLOCK, hp), BF16)
    parts, scale = [], []
    for g in range(ATTN_GROUP):
        h = kh * ATTN_GROUP + g
        col = qb[j * BLOCK:(j + 1) * BLOCK, (h // 2) * hp:(h // 2 + 1) * hp]
        parts.append(jnp.where(even, col, zero_q) if h % 2 == 0 else jnp.where(even, zero_q, col))
        scale.append(inv_rms[h // 2][h % 2:h % 2 + 1, j * BLOCK:(j + 1) * BLOCK])
    s_t = lax.dot_general(kdup[kh][j * BLOCK:(j + 2) * BLOCK], jnp.concatenate(parts, axis=0), _NT,
                          preferred_element_type=F32)
    key = lax.broadcasted_iota(jnp.int32, (BLOCK, ATTN_GROUP * BLOCK), 0)
    qry = lax.broadcasted_iota(jnp.int32, (BLOCK, ATTN_GROUP * BLOCK), 1) & (BLOCK - 1)
    use_prev = key > qry
    prev = s_t[:BLOCK]
    if no_prev is not None:
        prev = jnp.where(no_prev, -jnp.inf, prev)
    folded = jnp.where(use_prev, prev, s_t[BLOCK:]) * jnp.concatenate(scale, axis=1)
    return folded, use_prev


def _attn_softmax(kh, folded, use_prev, sink_ref):
    sink = jnp.concatenate(
        [jnp.full((1, BLOCK), sink_ref[kh * ATTN_GROUP + g], F32) for g in range(ATTN_GROUP)], axis=1)
    m = jnp.maximum(jnp.max(folded, axis=0, keepdims=True), sink)
    p = jnp.exp(folded - m)
    den = jnp.sum(p, axis=0, keepdims=True) + jnp.exp(sink - m)
    zero = jnp.zeros_like(p)
    p_t = jnp.concatenate([jnp.where(use_prev, p, zero), jnp.where(use_prev, zero, p)], axis=0)
    return p_t.astype(BF16), 1.0 / den


def _attn_values(j, kh, p_t, inv_den, v_t, o_ref):
    o_t = jnp.dot(v_t[kh * HEAD_DIM:(kh + 1) * HEAD_DIM, j * BLOCK:(j + 2) * BLOCK], p_t,
                  preferred_element_type=F32) * inv_den
    for g in range(ATTN_GROUP):
        h = kh * ATTN_GROUP + g
        o_ref[h * HEAD_DIM:(h + 1) * HEAD_DIM, j * BLOCK:(j + 1) * BLOCK] = (
            o_t[:, g * BLOCK:(g + 1) * BLOCK].astype(o_ref.dtype))


def _pair_blockdiag(x, even):
    zero = jnp.zeros_like(x)
    return jnp.concatenate([jnp.where(even, x, zero), jnp.where(even, zero, x)], axis=0)


def _pair_sum(x, even):
    s_e = jnp.sum(jnp.where(even, x, 0.0), axis=-1, keepdims=True)
    s_o = jnp.sum(jnp.where(even, 0.0, x), axis=-1, keepdims=True)
    return jnp.where(even, s_e, s_o)


def _rwkv_prep(xs, prm):
    c = CHUNK
    hp = 2 * RWKV_HEAD_SIZE
    w = RWKV_WIDTH
    rows = xs.shape[0]
    nc = rows // c
    cols = [slice(p * hp, (p + 1) * hp) for p in range(w // hp)]
    r = xs[:, 0:w]
    k = xs[:, w:2 * w]
    v = xs[:, 2 * w:3 * w]
    w_lora = xs[:, 3 * w:3 * w + LORA_DECAY]
    a_lora = xs[:, 3 * w + LORA_DECAY:3 * w + LORA_DECAY + LORA_AAA]
    g_lora = xs[:, 3 * w + LORA_DECAY + LORA_AAA:]

    d = prm["dbias"][...] + _bdot(jnp.tanh(w_lora), prm["dup"][...])
    lw = (-math.exp(-0.5)) * _sigmoid(d)
    a_sig = _sigmoid(prm["abias"][...] + _bdot(a_lora, prm["aup"][...]))
    gate = _bdot(_sigmoid(g_lora), prm["gup"][...])
    yield

    kkr = k * prm["kk"][...]
    ka = prm["ka"][...]
    k2 = k * (a_sig * ka + (1.0 - ka))
    rk = r * k2 * prm["rk"][...]
    even_t = lax.broadcasted_iota(jnp.int32, (rows, hp), 1) < RWKV_HEAD_SIZE
    den = jnp.concatenate(
        [jnp.maximum(jnp.sqrt(_pair_sum(kkr[:, cs] * kkr[:, cs], even_t)), L2_EPS) for cs in cols],
        axis=1)
    bonus = jnp.concatenate([_pair_sum(rk[:, cs], even_t) for cs in cols], axis=1) * v
    kk = kkr / den
    a_s = -kk
    b_s = kk * a_sig
    yield

    ti = lax.broadcasted_iota(jnp.int32, (c, 3 * c), 0)
    si = lax.broadcasted_iota(jnp.int32, (c, 3 * c), 1) & (c - 1)
    lower3 = (ti >= si).astype(BF16)
    lw_hi = lw.astype(BF16)
    lw_r = lw - lw_hi.astype(F32)
    lw_mid = lw_r.astype(BF16)
    lw_lo = (lw_r - lw_mid.astype(F32)).astype(BF16)
    cl = jnp.concatenate(
        [jnp.dot(lower3, jnp.concatenate([z[ci * c:(ci + 1) * c] for z in (lw_hi, lw_mid, lw_lo)], axis=0),
                 preferred_element_type=F32) for ci in range(nc)], axis=0)
    yield

    e_c = jnp.exp(cl)
    e_neg = 1.0 / e_c
    g_rows = [e_c[ci * c + c - 1:ci * c + c, :] for ci in range(nc)]
    g_full = jnp.concatenate([jnp.broadcast_to(g, (c, w)) for g in g_rows], axis=0)
    at = a_s * jnp.exp(cl - lw)
    rt = r * e_c
    yield
    bt = b_s * e_neg
    kt = k2 * e_neg
    ops = dict(rt=rt, g_rows=g_rows, gate=gate, bonus=bonus,
               bh=(bt * g_full).astype(BF16), khat=(kt * g_full).astype(BF16),
               at_b=at.astype(BF16), rt_b=rt.astype(BF16), bt_b=bt.astype(BF16),
               kt_b=kt.astype(BF16), v_b=v.astype(BF16))
    return ops


def _rwkv_solve(ops, prm):
    c = CHUNK
    hs = RWKV_HEAD_SIZE
    assert c == hs
    hp = 2 * hs
    w = RWKV_WIDTH
    npair = w // hp
    nc = ops["rt"].shape[0] // c
    cols = [slice(p * hp, (p + 1) * hp) for p in range(npair)]
    even = lax.broadcasted_iota(jnp.int32, (c, hp), 1) < RWKV_HEAD_SIZE
    gi = lax.broadcasted_iota(jnp.int32, (2 * c, 2 * hp), 0)
    g_t = gi & (c - 1)
    g_s = lax.broadcasted_iota(jnp.int32, (2 * c, 2 * hp), 1) & (c - 1)
    g_mask = (g_t > g_s) | ((gi >= c) & (g_t == g_s))
    eye2 = (lax.broadcasted_iota(jnp.int32, (c, hp), 0)
            == (lax.broadcasted_iota(jnp.int32, (c, hp), 1) & (c - 1)))
    units = [(ci, p) for ci in range(nc) for p in range(npair)]

    def col(name, ci, p):
        return ops[name][ci * c:(ci + 1) * c, cols[p]]

    def bd(x):
        return _pair_blockdiag(x, even)

    def lanes(*xs):
        return jnp.concatenate(xs, axis=1)

    def rows(*xs):
        return jnp.concatenate(xs, axis=0)

    def mm(a, b):
        return jnp.dot(a, b, preferred_element_type=F32)

    gm = [jnp.where(g_mask,
                    lax.dot_general(rows(col("at_b", ci, p), col("rt_b", ci, p)),
                                    rows(bd(col("bt_b", ci, p)), bd(col("kt_b", ci, p))), _NT,
                                    preferred_element_type=F32), 0.0)
          for ci, p in units]
    yield
    a_pow = [g[:c, :hp].astype(BF16) for g in gm]
    t_inv = [jnp.where(eye2, 1.0, g[:c, :hp]) for g in gm]
    a_pow = [mm(a, bd(a)).astype(BF16) for a in a_pow]
    v_st = [col("v_b", ci, p) for ci, p in units]
    akv = [mm(g[:c, hp:].astype(BF16), bd(v)).astype(BF16) for g, v in zip(gm, v_st)]
    yield
    for _ in range(int(math.log2(c)) - 2):
        res = [mm(a, lanes(bd(t.astype(BF16)), bd(a))) for a, t in zip(a_pow, t_inv)]
        t_inv = [t + z[:, :hp] for t, z in zip(t_inv, res)]
        a_pow = [z[:, hp:].astype(BF16) for z in res]
        yield
    t_inv = [t + mm(a, bd(t.astype(BF16))) for a, t in zip(a_pow, t_inv)]
    yield
    wu = [mm(t.astype(BF16), lanes(bd(col("at_b", ci, p)), bd(x)))
          for (ci, p), t, x in zip(units, t_inv, akv)]
    w_b = [z[:, :hp].astype(BF16) for z in wu]
    u0_b = [z[:, hp:].astype(BF16) for z in wu]
    yield
    def head_t(x):
        xt = x.T
        return lanes(xt[:hs], xt[hs:]).astype(BF16)

    ab = [mm(rows(g[c:, :].astype(BF16), lanes(head_t(col("bh", ci, p)), head_t(col("khat", ci, p)))),
             rows(lanes(bd(wb), bd(ub)), lanes(jnp.zeros((hp, hp), BF16), bd(v))))
          for (ci, p), g, wb, ub, v in zip(units, gm, w_b, u0_b, v_st)]
    ab_y = [z[:c] for z in ab]
    ab_h = [z[c:] for z in ab]
    yield

    def tail(state, out_rows):
        for ci in range(nc):
            y_cols = []
            for p in range(npair):
                idx = ci * npair + p
                qmat = col("rt", ci, p) + ab_y[idx][:, :hp]
                g_diag = jnp.where(eye2, jnp.broadcast_to(ops["g_rows"][ci][:, cols[p]], (hs, hp)), 0.0)
                mmat = g_diag + ab_h[idx][:, :hp]
                ys = mm(rows(qmat, mmat).astype(BF16), bd(state[p].astype(BF16)))
                y = ys[:c] + ab_y[idx][:, hp:]
                state[p] = ys[c:] + ab_h[idx][:, hp:]
                mean = _pair_sum(y, even) * (1.0 / RWKV_HEAD_SIZE)
                yc = y - mean
                var = _pair_sum(yc * yc, even) * (1.0 / RWKV_HEAD_SIZE)
                y_cols.append(yc * lax.rsqrt(var + GN_EPS))
            yn = jnp.concatenate(y_cols, axis=1)
            rs = slice(ci * c, (ci + 1) * c)
            out_rows(ci, (yn * prm["lng"][...] + prm["lnb"][...] + ops["bonus"][rs]) * ops["gate"][rs])
            yield

    return tail


def _round_robin(gens):
    values = [None] * len(gens)
    live = list(range(len(gens)))
    while live:
        for i in list(live):
            try:
                next(gens[i])
            except StopIteration as stop:
                values[i] = stop.value
                live.remove(i)
    return values


def _rwkv_kernel(xs_ref, dbias_ref, dup_ref, abias_ref, aup_ref, gup_ref,
                 kk_ref, ka_ref, rk_ref, lng_ref, lnb_ref, o_ref, state_ref):
    hp = 2 * RWKV_HEAD_SIZE
    tb = xs_ref.shape[1]
    gr = min(RWKV_GROUP, tb)
    ngroup = tb // gr
    prm = dict(dbias=dbias_ref, dup=dup_ref, abias=abias_ref, aup=aup_ref, gup=gup_ref,
               kk=kk_ref, ka=ka_ref, rk=rk_ref, lng=lng_ref, lnb=lnb_ref)

    @pl.when(pl.program_id(1) == 0)
    def _():
        state_ref[...] = jnp.zeros_like(state_ref)

    def prep(g):
        return _rwkv_prep(xs_ref[0, g * gr:(g + 1) * gr, :], prm)

    state = [state_ref[:, p * hp:(p + 1) * hp] for p in range(RWKV_WIDTH // hp)]
    (ops,) = _round_robin([prep(0)])
    done = _round_robin([_rwkv_solve(ops, prm)] + ([prep(1)] if ngroup > 1 else []))
    tail, ops = done[0], (done[1] if ngroup > 1 else None)
    for g in range(ngroup):
        def out_rows(ci, value, g=g):
            r0 = g * gr + ci * CHUNK
            o_ref[0, r0:r0 + CHUNK, :] = value.astype(o_ref.dtype)

        gens = [tail(state, out_rows)]
        if g + 1 < ngroup:
            gens.append(_rwkv_solve(ops, prm))
        if g + 2 < ngroup:
            gens.append(prep(g + 2))
        done = _round_robin(gens)
        tail = done[1] if g + 1 < ngroup else None
        ops = done[2] if g + 2 < ngroup else None
    for p in range(RWKV_WIDTH // hp):
        state_ref[:, p * hp:(p + 1) * hp] = state[p]


def _rwkv(rw, decay_bias, decay_up, aaa_bias, aaa_up, gate_up, k_k, k_a, r_k, ln_g, ln_b):
    b, t, _ = rw.shape
    tb = min(RWKV_TILE, t)
    w = RWKV_WIDTH

    def full(shape):
        return pl.BlockSpec(shape, lambda bi, ti: (0, 0))

    return pl.pallas_call(
        _rwkv_kernel,
        grid=(b, t // tb),
        in_specs=[
            pl.BlockSpec((1, tb, RWKV_IN_WIDTH), lambda bi, ti: (bi, ti, 0)),
            full((1, w)), full((LORA_DECAY, w)),
            full((1, w)), full((LORA_AAA, w)),
            full((LORA_GATE, w)),
            full((1, w)), full((1, w)), full((1, w)), full((1, w)), full((1, w)),
        ],
        out_specs=pl.BlockSpec((1, tb, w), lambda bi, ti: (bi, ti, 0)),
        out_shape=jax.ShapeDtypeStruct((b, t, w), BF16),
        scratch_shapes=[pltpu.VMEM((RWKV_HEAD_SIZE, w), F32)],
        compiler_params=pltpu.CompilerParams(
            dimension_semantics=("arbitrary", "arbitrary"), vmem_limit_bytes=VMEM_LIMIT),
        name="rwkv",
    )(rw, decay_bias, decay_up, aaa_bias, aaa_up, gate_up, k_k, k_a, r_k, ln_g, ln_b)


def _attn_merge_kernel(x_ref, q_ref, kv_ref, kvp_ref, rwkv_ref, g_ref, qg_ref, kg_ref, sink_ref,
                       wg_ref, wba_ref, wbr_ref, wo_ref, h_ref, attn_ref):
    x = x_ref[...]
    d = x.shape[1]
    nblk = q_ref.shape[1] // BLOCK
    units = [(j, kh) for j in range(nblk) for kh in range(ATTN_KV_HEADS)]
    n_u = len(units)
    gw = 2 * BLOCK
    n_g = (wg_ref.shape[1] - N_PROJ) // gw
    gate_parts = []

    def gate_slice():
        c0 = N_PROJ + len(gate_parts) * gw
        gate_parts.append(jnp.dot(ub, wg_ref[:, c0:c0 + gw], preferred_element_type=F32))

    y_rwkv = jnp.dot(rwkv_ref[...], wbr_ref[...], preferred_element_type=F32)
    ub = _rms(x, g_ref[...]).astype(BF16)
    gate_slice()
    kv_all = jnp.concatenate([kvp_ref[0], kv_ref[0]], axis=0)
    kdup, v_t = _attn_prep_kv(kv_all, qg_ref[...], kg_ref[...])
    gate_slice()
    qb, inv_rms = _attn_prep_q(q_ref[0])
    first_tile = pl.program_id(1) == 0
    scores = [_attn_scores(j, kh, qb, kdup, inv_rms, first_tile if j == 0 else None)
              for j, kh in units]
    soft = []
    for idx, (_, kh) in enumerate(units):
        while len(gate_parts) < 2 + (idx + 1) * (n_g - 2) // n_u:
            gate_slice()
        soft.append(_attn_softmax(kh, *scores[idx], sink_ref))
    while len(gate_parts) < n_g:
        gate_slice()
    gates = jnp.concatenate(gate_parts, axis=1)
    for unit, sm in zip(units, soft):
        _attn_values(*unit, *sm, v_t, attn_ref)
    y_attn = lax.dot_general(attn_ref[...], wba_ref[...], _TN, preferred_element_type=F32)
    mixed = _sigmoid(gates[:, :d]) * y_attn + _sigmoid(gates[:, d:]) * y_rwkv
    h_ref[...] = x + _bdot(mixed, wo_ref[...])


def _attn_merge(x2, q, kv, rwkv, gain, q_gain, k_gain, sinks, w_in, w_ba, w_br, w_out):
    n, d = x2.shape
    b, t, _ = q.shape
    tq = min(ATTN_TILE, t)
    nblk = tq // BLOCK
    nt = t // tq
    q_gain = jnp.tile(q_gain * (HEAD_DIM ** -0.5), (1, ATTN_KV_HEADS))
    k_gain = jnp.tile(k_gain, (1, ATTN_KV_HEADS))

    def full(a):
        return pl.BlockSpec(a.shape, lambda bi, i: (0, 0), pipeline_mode=pl.Buffered(1))

    def rows(width):
        return pl.BlockSpec((tq, width), lambda bi, i: (bi * nt + i, 0))

    return pl.pallas_call(
        _attn_merge_kernel,
        grid=(b, nt),
        in_specs=[
            rows(d),
            pl.BlockSpec((1, tq, ATTN_Q_WIDTH), lambda bi, i: (bi, i, 0)),
            pl.BlockSpec((1, tq, 2 * ATTN_KV_WIDTH), lambda bi, i: (bi, i, 0)),
            pl.BlockSpec((1, BLOCK, 2 * ATTN_KV_WIDTH),
                         lambda bi, i: (bi, jnp.maximum(i * nblk - 1, 0), 0)),
            rows(rwkv.shape[1]),
            full(gain), full(q_gain), full(k_gain),
            pl.BlockSpec(memory_space=pltpu.SMEM),
            full(w_in), full(w_ba), full(w_br), full(w_out),
        ],
        out_specs=rows(d),
        out_shape=jax.ShapeDtypeStruct((n, d), F32),
        scratch_shapes=[pltpu.VMEM((ATTN_Q_WIDTH, tq), BF16)],
        compiler_params=pltpu.CompilerParams(
            dimension_semantics=("arbitrary", "arbitrary"), vmem_limit_bytes=VMEM_LIMIT),
        name="attn_merge",
    )(x2, q, kv, kv, rwkv, gain, q_gain, k_gain, sinks, w_in, w_ba, w_br, w_out)


def _mlp_kernel(h_ref, g_ref, w1_ref, w2_ref, o_ref):
    h = h_ref[...]
    u = _rms(h, g_ref[...]).astype(BF16)
    d_ff = w1_ref.shape[1]
    acc = h
    for c0 in range(0, d_ff, MLP_CHUNK):
        hidden = jnp.square(jnp.maximum(
            jnp.dot(u, w1_ref[:, c0:c0 + MLP_CHUNK], preferred_element_type=F32), 0.0))
        acc = acc + jnp.dot(hidden.astype(BF16), w2_ref[c0:c0 + MLP_CHUNK, :],
                            preferred_element_type=F32)
    o_ref[...] = acc


def _mlp(h2, gain, w1, w2):
    n, d = h2.shape
    tm = min(2 * ROW_TILE, n)

    def full(a):
        return pl.BlockSpec(a.shape, lambda i: (0, 0), pipeline_mode=pl.Buffered(1))

    return pl.pallas_call(
        _mlp_kernel,
        grid=(n // tm,),
        in_specs=[pl.BlockSpec((tm, d), lambda i: (i, 0)), full(gain), full(w1), full(w2)],
        out_specs=pl.BlockSpec((tm, d), lambda i: (i, 0)),
        out_shape=jax.ShapeDtypeStruct((n, d), F32),
        compiler_params=pltpu.CompilerParams(
            dimension_semantics=("arbitrary",), vmem_limit_bytes=VMEM_LIMIT),
        name="mlp",
    )(h2, gain, w1, w2)


def _layer(h, norm1_gain, w_in, q_norm_gain, k_norm_gain, attn_sinks,
           mu_r, mu_k, mu_v, mu_w, mu_a, mu_g, decay_bias, decay_up, aaa_bias, aaa_up,
           gate_up, k_k, k_a, r_k, ln_x_gain, ln_x_bias, w_branch_attn, w_branch_rwkv,
           w_out, norm2_gain, w_ff_in, w_ff_out):
    b, t, d = h.shape
    x2 = h.reshape(b * t, d)
    row = lambda a: a.reshape(1, -1).astype(F32)
    w_in_b = w_in.astype(BF16)

    mu = jnp.concatenate([mu_r, mu_k, mu_v, mu_w, mu_a, mu_g]).reshape(1, -1).astype(F32)
    q, kv, rw = _inproj(x2, row(norm1_gain), w_in_b, mu, t)
    rwkv = _rwkv(rw.reshape(b, t, -1), row(decay_bias), decay_up.astype(BF16),
                 row(aaa_bias), aaa_up.astype(BF16), gate_up.astype(BF16),
                 row(k_k), row(k_a), row(r_k), row(ln_x_gain), row(ln_x_bias))
    h2 = _attn_merge(x2, q.reshape(b, t, -1), kv.reshape(b, t, -1), rwkv.reshape(b * t, -1),
                     row(norm1_gain), row(q_norm_gain), row(k_norm_gain), attn_sinks.astype(F32),
                     w_in_b, w_branch_attn.astype(BF16), w_branch_rwkv.astype(BF16),
                     w_out.astype(BF16))
    out = _mlp(h2, row(norm2_gain), w_ff_in.astype(BF16), w_ff_out.astype(BF16))
    return out.reshape(b, t, d)


def kernel(x, norm1_gain, w_in, q_norm_gain, k_norm_gain, attn_sinks, mu_r, mu_k, mu_v, mu_w, mu_a, mu_g, decay_bias, decay_up, aaa_bias, aaa_up, gate_up, k_k, k_a, r_k, ln_x_gain, ln_x_bias, w_branch_attn, w_branch_rwkv, w_out, norm2_gain, w_ff_in, w_ff_out):
    h = x.astype(F32)
    params = (norm1_gain, w_in, q_norm_gain, k_norm_gain, attn_sinks, mu_r, mu_k, mu_v, mu_w,
              mu_a, mu_g, decay_bias, decay_up, aaa_bias, aaa_up, gate_up, k_k, k_a, r_k,
              ln_x_gain, ln_x_bias, w_branch_attn, w_branch_rwkv, w_out, norm2_gain,
              w_ff_in, w_ff_out)
    for l in range(norm1_gain.shape[0]):
        h = _layer(h, *(p[l] for p in params))
    return h.astype(x.dtype)
```

```python
import functools
import math

import jax
import jax.numpy as jnp
from jax import lax
from jax.experimental import pallas as pl
from jax.experimental.pallas import tpu as pltpu

F32 = jnp.float32
BF16 = jnp.bfloat16

HEAD_DIM = 64
ATTN_Q_HEADS = 8
ATTN_KV_HEADS = 2
ATTN_GROUP = ATTN_Q_HEADS // ATTN_KV_HEADS
WINDOW = 128
BLOCK = 128
RWKV_HEADS = 8
RWKV_HEAD_SIZE = 64
LORA_DECAY = 64
LORA_AAA = 64
LORA_GATE = 128
ATTN_Q_WIDTH = ATTN_Q_HEADS * HEAD_DIM
ATTN_KV_WIDTH = ATTN_KV_HEADS * HEAD_DIM
RWKV_WIDTH = RWKV_HEADS * RWKV_HEAD_SIZE
RWKV_IN_WIDTH = 3 * RWKV_WIDTH + LORA_DECAY + LORA_AAA + LORA_GATE
N_PROJ = ATTN_Q_WIDTH + 2 * ATTN_KV_WIDTH + RWKV_IN_WIDTH
RMS_EPS = 1e-6
GN_EPS = 64e-5
L2_EPS = 1e-12

CHUNK = 64
ROW_TILE = 512
ATTN_TILE = 1024
RWKV_TILE = 1024
RWKV_GROUP = 256
MLP_CHUNK = 1024
VMEM_LIMIT = 56 * 1024 * 1024

_NT = (((1,), (1,)), ((), ()))
_TN = (((0,), (0,)), ((), ()))


def _bdot(a, b):
    return jnp.dot(a.astype(BF16), b.astype(BF16), preferred_element_type=F32)


def _rms(x, gain):
    return x * lax.rsqrt(jnp.mean(x * x, axis=-1, keepdims=True) + RMS_EPS) * gain


def _sigmoid(x):
    return 0.5 * jnp.tanh(0.5 * x) + 0.5


def _inproj_kernel(x_ref, g_ref, w_ref, mu_ref, q_ref, kv_ref, rw_ref, last_ref, *, tiles_per_seq):
    u = _rms(x_ref[...], g_ref[...]).astype(BF16)
    n_attn = ATTN_Q_WIDTH + 2 * ATTN_KV_WIDTH
    p = jnp.dot(u, w_ref[:, :n_attn], preferred_element_type=F32)
    q_ref[...] = p[:, :ATTN_Q_WIDTH]
    kv_ref[...] = p[:, ATTN_Q_WIDTH:]
    rw = jnp.dot(u, w_ref[:, n_attn:], preferred_element_type=F32)
    tm = rw.shape[0]
    seq_start = pl.program_id(0) % tiles_per_seq == 0
    before = jnp.where(seq_start, 0.0, last_ref[...])
    first = lax.broadcasted_iota(jnp.int32, (tm, 1), 0) == 0
    prev = jnp.where(first, before, pltpu.roll(rw, 1, 0))
    last_ref[...] = rw[tm - 1:tm, :]
    rw_ref[...] = rw + (prev - rw) * mu_ref[...]


def _inproj(x2, gain, w, mu, seq_len):
    n, d = x2.shape
    wid = N_PROJ
    tm = min(2 * ROW_TILE, seq_len)

    def full(shape):
        return pl.BlockSpec(shape, lambda i: (0, 0), pipeline_mode=pl.Buffered(1))

    return pl.pallas_call(
        functools.partial(_inproj_kernel, tiles_per_seq=seq_len // tm),
        grid=(n // tm,),
        in_specs=[
            pl.BlockSpec((tm, d), lambda i: (i, 0)),
            full((1, d)), full((d, wid)), full((1, RWKV_IN_WIDTH)),
        ],
        out_specs=[
            pl.BlockSpec((tm, ATTN_Q_WIDTH), lambda i: (i, 0)),
            pl.BlockSpec((tm, 2 * ATTN_KV_WIDTH), lambda i: (i, 0)),
            pl.BlockSpec((tm, RWKV_IN_WIDTH), lambda i: (i, 0)),
        ],
        out_shape=[
            jax.ShapeDtypeStruct((n, ATTN_Q_WIDTH), F32),
            jax.ShapeDtypeStruct((n, 2 * ATTN_KV_WIDTH), F32),
            jax.ShapeDtypeStruct((n, RWKV_IN_WIDTH), F32),
        ],
        scratch_shapes=[pltpu.VMEM((1, RWKV_IN_WIDTH), F32)],
        compiler_params=pltpu.CompilerParams(
            dimension_semantics=("arbitrary",), vmem_limit_bytes=VMEM_LIMIT),
        name="inproj",
    )(x2, gain, w, mu)


def _head_sumsq(x, ones_bd):
    x2 = x * x
    hi = x2.astype(BF16)
    lo = (x2 - hi.astype(F32)).astype(BF16)
    return (jnp.dot(hi, ones_bd, preferred_element_type=F32)
            + jnp.dot(lo, ones_bd, preferred_element_type=F32))


def _attn_prep_kv(kv_all, q_gain, k_gain):
    hp = 2 * HEAD_DIM
    even = lax.broadcasted_iota(jnp.int32, (1, hp), 1) < HEAD_DIM
    ri = lax.broadcasted_iota(jnp.int32, (hp, hp), 0)
    ci = lax.broadcasted_iota(jnp.int32, (hp, hp), 1)
    ones_bd = ((ri < HEAD_DIM) == (ci < HEAD_DIM)).astype(BF16)
    k = kv_all[:, :hp]
    kn = k * lax.rsqrt(_head_sumsq(k, ones_bd) * (1.0 / HEAD_DIM) + RMS_EPS) * k_gain * q_gain
    kn_sw = pltpu.roll(kn, HEAD_DIM, 1)
    kdup = [jnp.where(even, kn, kn_sw).astype(BF16), jnp.where(even, kn_sw, kn).astype(BF16)]
    v_t = kv_all[:, hp:].T.astype(BF16)
    return kdup, v_t


def _attn_prep_q(q):
    hp = 2 * HEAD_DIM
    si = lax.broadcasted_iota(jnp.int32, (16, hp), 0)
    sj = lax.broadcasted_iota(jnp.int32, (16, hp), 1)
    sel = (((si == 0) & (sj < HEAD_DIM)) | ((si == 1) & (sj >= HEAD_DIM))).astype(BF16)
    q2 = q * q
    hi = q2.astype(BF16)
    lo = (q2 - hi.astype(F32)).astype(BF16)
    inv_rms = []
    for m in range(ATTN_Q_WIDTH // hp):
        cs = slice(m * hp, (m + 1) * hp)
        ss = (lax.dot_general(sel, hi[:, cs], _NT, preferred_element_type=F32)
              + lax.dot_general(sel, lo[:, cs], _NT, preferred_element_type=F32))
        inv_rms.append(lax.rsqrt(ss * (1.0 / HEAD_DIM) + RMS_EPS))
    return q.astype(BF16), inv_rms


def _attn_scores(j, kh, qb, kdup, inv_rms, no_prev):
    assert WINDOW == BLOCK
    hp = 2 * HEAD_DIM
    even = lax.broadcasted_iota(jnp.int32, (1, hp), 1) < HEAD_DIM
    zero_q = jnp.zeros((BLOCK, hp), BF16)
    parts, scale = [], []
    for g in range(ATTN_GROUP):
        h = kh * ATTN_GROUP + g
        col = qb[j * BLOCK:(j + 1) * BLOCK, (h // 2) * hp:(h // 2 + 1) * hp]
        parts.append(jnp.where(even, col, zero_q) if h % 2 == 0 else jnp.where(even, zero_q, col))
        scale.append(inv_rms[h // 2][h % 2:h % 2 + 1, j * BLOCK:(j + 1) * BLOCK])
    s_t = lax.dot_general(kdup[kh][j * BLOCK:(j + 2) * BLOCK], jnp.concatenate(parts, axis=0), _NT,
                          preferred_element_type=F32)
    key = lax.broadcasted_iota(jnp.int32, (BLOCK, ATTN_GROUP * BLOCK), 0)
    qry = lax.broadcasted_iota(jnp.int32, (BLOCK, ATTN_GROUP * BLOCK), 1) & (BLOCK - 1)
    use_prev = key > qry
    prev = s_t[:BLOCK]
    if no_prev is not None:
        prev = jnp.where(no_prev, -jnp.inf, prev)
    folded = jnp.where(use_prev, prev, s_t[BLOCK:]) * jnp.concatenate(scale, axis=1)
    return folded, use_prev


def _attn_softmax(kh, folded, use_prev, sink_ref):
    sink = jnp.concatenate(
        [jnp.full((1, BLOCK), sink_ref[kh * ATTN_GROUP + g], F32) for g in range(ATTN_GROUP)], axis=1)
    m = jnp.maximum(jnp.max(folded, axis=0, keepdims=True), sink)
    p = jnp.exp(folded - m)
    den = jnp.sum(p, axis=0, keepdims=True) + jnp.exp(sink - m)
    zero = jnp.zeros_like(p)
    p_t = jnp.concatenate([jnp.where(use_prev, p, zero), jnp.where(use_prev, zero, p)], axis=0)
    return p_t.astype(BF16), 1.0 / den


def _attn_values(j, kh, p_t, inv_den, v_t, o_ref):
    o_t = jnp.dot(v_t[kh * HEAD_DIM:(kh + 1) * HEAD_DIM, j * BLOCK:(j + 2) * BLOCK], p_t,
                  preferred_element_type=F32) * inv_den
    for g in range(ATTN_GROUP):
        h = kh * ATTN_GROUP + g
        o_ref[h * HEAD_DIM:(h + 1) * HEAD_DIM, j * BLOCK:(j + 1) * BLOCK] = (
            o_t[:, g * BLOCK:(g + 1) * BLOCK].astype(o_ref.dtype))


def _pair_blockdiag(x, even):
    zero = jnp.zeros_like(x)
    return jnp.concatenate([jnp.where(even, x, zero), jnp.where(even, zero, x)], axis=0)


def _pair_sum(x, even):
    s_e = jnp.sum(jnp.where(even, x, 0.0), axis=-1, keepdims=True)
    s_o = jnp.sum(jnp.where(even, 0.0, x), axis=-1, keepdims=True)
    return jnp.where(even, s_e, s_o)


def _rwkv_prep(xs, prm):
    c = CHUNK
    hp = 2 * RWKV_HEAD_SIZE
    w = RWKV_WIDTH
    rows = xs.shape[0]
    nc = rows // c
    cols = [slice(p * hp, (p + 1) * hp) for p in range(w // hp)]
    r = xs[:, 0:w]
    k = xs[:, w:2 * w]
    v = xs[:, 2 * w:3 * w]
    w_lora = xs[:, 3 * w:3 * w + LORA_DECAY]
    a_lora = xs[:, 3 * w + LORA_DECAY:3 * w + LORA_DECAY + LORA_AAA]
    g_lora = xs[:, 3 * w + LORA_DECAY + LORA_AAA:]

    d = prm["dbias"][...] + _bdot(jnp.tanh(w_lora), prm["dup"][...])
    half = 0.5 * math.exp(-0.5)
    lw = -half * jnp.tanh(0.5 * d) - half
    a_sig = _sigmoid(prm["abias"][...] + _bdot(a_lora, prm["aup"][...]))
    gate = _bdot(_sigmoid(g_lora), prm["gup"][...])
    yield

    kkr = k * prm["kk"][...]
    ka = prm["ka"][...]
    k2 = k * (a_sig * ka + (1.0 - ka))
    rk = r * k2 * prm["rk"][...]
    even_t = lax.broadcasted_iota(jnp.int32, (rows, hp), 1) < RWKV_HEAD_SIZE
    den = jnp.concatenate(
        [jnp.maximum(jnp.sqrt(_pair_sum(kkr[:, cs] * kkr[:, cs], even_t)), L2_EPS) for cs in cols],
        axis=1)
    bonus = jnp.concatenate([_pair_sum(rk[:, cs], even_t) for cs in cols], axis=1) * v
    kk = kkr / den
    a_s = -kk
    b_s = kk * a_sig
    yield

    ti = lax.broadcasted_iota(jnp.int32, (c, 3 * c), 0)
    si = lax.broadcasted_iota(jnp.int32, (c, 3 * c), 1) & (c - 1)
    lower3 = (ti >= si).astype(BF16)
    lw_hi = lw.astype(BF16)
    lw_r = lw - lw_hi.astype(F32)
    lw_mid = lw_r.astype(BF16)
    lw_lo = (lw_r - lw_mid.astype(F32)).astype(BF16)
    cl = jnp.concatenate(
        [jnp.dot(lower3, jnp.concatenate([z[ci * c:(ci + 1) * c] for z in (lw_hi, lw_mid, lw_lo)], axis=0),
                 preferred_element_type=F32) for ci in range(nc)], axis=0)
    yield

    e_c = jnp.exp(cl)
    e_neg = 1.0 / e_c
    g_rows = [e_c[ci * c + c - 1:ci * c + c, :] for ci in range(nc)]
    g_full = jnp.concatenate([jnp.broadcast_to(g, (c, w)) for g in g_rows], axis=0)
    at = a_s * jnp.exp(cl - lw)
    rt = r * e_c
    yield
    bt = b_s * e_neg
    kt = k2 * e_neg
    ops = dict(rt=rt, g_rows=g_rows, gate=gate, bonus=bonus,
               bh=(bt * g_full).astype(BF16), khat=(kt * g_full).astype(BF16),
               at_b=at.astype(BF16), rt_b=rt.astype(BF16), bt_b=bt.astype(BF16),
               kt_b=kt.astype(BF16), v_b=v.astype(BF16))
    return ops


def _rwkv_solve(ops, prm):
    c = CHUNK
    hs = RWKV_HEAD_SIZE
    assert c == hs
    hp = 2 * hs
    w = RWKV_WIDTH
    npair = w // hp
    nc = ops["rt"].shape[0] // c
    cols = [slice(p * hp, (p + 1) * hp) for p in range(npair)]
    even = lax.broadcasted_iota(jnp.int32, (c, hp), 1) < RWKV_HEAD_SIZE
    gi = lax.broadcasted_iota(jnp.int32, (2 * c, 2 * hp), 0)
    g_t = gi & (c - 1)
    g_s = lax.broadcasted_iota(jnp.int32, (2 * c, 2 * hp), 1) & (c - 1)
    g_mask = (g_t > g_s) | ((gi >= c) & (g_t == g_s))
    eye2 = (lax.broadcasted_iota(jnp.int32, (c, hp), 0)
            == (lax.broadcasted_iota(jnp.int32, (c, hp), 1) & (c - 1)))
    units = [(ci, p) for ci in range(nc) for p in range(npair)]

    def col(name, ci, p):
        return ops[name][ci * c:(ci + 1) * c, cols[p]]

    def bd(x):
        return _pair_blockdiag(x, even)

    def lanes(*xs):
        return jnp.concatenate(xs, axis=1)

    def rows(*xs):
        return jnp.concatenate(xs, axis=0)

    def mm(a, b):
        return jnp.dot(a, b, preferred_element_type=F32)

    gm = [jnp.where(g_mask,
                    lax.dot_general(rows(col("at_b", ci, p), col("rt_b", ci, p)),
                                    rows(bd(col("bt_b", ci, p)), bd(col("kt_b", ci, p))), _NT,
                                    preferred_element_type=F32), 0.0)
          for ci, p in units]
    yield
    a_pow = [g[:c, :hp].astype(BF16) for g in gm]
    t_inv = [jnp.where(eye2, 1.0, g[:c, :hp]) for g in gm]
    a_pow = [mm(a, bd(a)).astype(BF16) for a in a_pow]
    v_st = [col("v_b", ci, p) for ci, p in units]
    akv = [mm(g[:c, hp:].astype(BF16), bd(v)).astype(BF16) for g, v in zip(gm, v_st)]
    yield
    for _ in range(int(math.log2(c)) - 2):
        res = [mm(a, lanes(bd(t.astype(BF16)), bd(a))) for a, t in zip(a_pow, t_inv)]
        t_inv = [t + z[:, :hp] for t, z in zip(t_inv, res)]
        a_pow = [z[:, hp:].astype(BF16) for z in res]
        yield
    t_inv = [t + mm(a, bd(t.astype(BF16))) for a, t in zip(a_pow, t_inv)]
    yield
    wu = [mm(t.astype(BF16), lanes(bd(col("at_b", ci, p)), bd(x)))
          for (ci, p), t, x in zip(units, t_inv, akv)]
    w_b = [z[:, :hp].astype(BF16) for z in wu]
    u0_b = [z[:, hp:].astype(BF16) for z in wu]
    yield
    def head_t(x):
        xt = x.T
        return lanes(xt[:hs], xt[hs:]).astype(BF16)

    ab = [mm(rows(g[c:, :].astype(BF16), lanes(head_t(col("bh", ci, p)), head_t(col("khat", ci, p)))),
             rows(lanes(bd(wb), bd(ub)), lanes(jnp.zeros((hp, hp), BF16), bd(v))))
          for (ci, p), g, wb, ub, v in zip(units, gm, w_b, u0_b, v_st)]
    ab_y = [z[:c] for z in ab]
    ab_h = [z[c:] for z in ab]
    yield

    def tail(state, out_rows):
        for ci in range(nc):
            y_cols = []
            for p in range(npair):
                idx = ci * npair + p
                qmat = col("rt", ci, p) + ab_y[idx][:, :hp]
                g_diag = jnp.where(eye2, jnp.broadcast_to(ops["g_rows"][ci][:, cols[p]], (hs, hp)), 0.0)
                mmat = g_diag + ab_h[idx][:, :hp]
                ys = mm(rows(qmat, mmat).astype(BF16), bd(state[p].astype(BF16)))
                y = ys[:c] + ab_y[idx][:, hp:]
                state[p] = ys[c:] + ab_h[idx][:, hp:]
                mean = _pair_sum(y, even) * (1.0 / RWKV_HEAD_SIZE)
                yc = y - mean
                var = _pair_sum(yc * yc, even) * (1.0 / RWKV_HEAD_SIZE)
                y_cols.append(yc * lax.rsqrt(var + GN_EPS))
            yn = jnp.concatenate(y_cols, axis=1)
            rs = slice(ci * c, (ci + 1) * c)
            out_rows(ci, (yn * prm["lng"][...] + prm["lnb"][...] + ops["bonus"][rs]) * ops["gate"][rs])
            yield

    return tail


def _round_robin(gens):
    values = [None] * len(gens)
    live = list(range(len(gens)))
    while live:
        for i in list(live):
            try:
                next(gens[i])
            except StopIteration as stop:
                values[i] = stop.value
                live.remove(i)
    return values


def _rwkv_kernel(xs_ref, dbias_ref, dup_ref, abias_ref, aup_ref, gup_ref,
                 kk_ref, ka_ref, rk_ref, lng_ref, lnb_ref, o_ref, state_ref):
    hp = 2 * RWKV_HEAD_SIZE
    tb = xs_ref.shape[1]
    gr = min(RWKV_GROUP, tb)
    ngroup = tb // gr
    prm = dict(dbias=dbias_ref, dup=dup_ref, abias=abias_ref, aup=aup_ref, gup=gup_ref,
               kk=kk_ref, ka=ka_ref, rk=rk_ref, lng=lng_ref, lnb=lnb_ref)

    @pl.when(pl.program_id(1) == 0)
    def _():
        state_ref[...] = jnp.zeros_like(state_ref)

    def prep(g):
        return _rwkv_prep(xs_ref[0, g * gr:(g + 1) * gr, :], prm)

    state = [state_ref[:, p * hp:(p + 1) * hp] for p in range(RWKV_WIDTH // hp)]
    (ops,) = _round_robin([prep(0)])
    done = _round_robin([_rwkv_solve(ops, prm)] + ([prep(1)] if ngroup > 1 else []))
    tail, ops = done[0], (done[1] if ngroup > 1 else None)
    for g in range(ngroup):
        def out_rows(ci, value, g=g):
            r0 = g * gr + ci * CHUNK
            o_ref[0, r0:r0 + CHUNK, :] = value.astype(o_ref.dtype)

        gens = []
        if g + 1 < ngroup:
            gens.append(_rwkv_solve(ops, prm))
        if g + 2 < ngroup:
            gens.append(prep(g + 2))
        gens.append(tail(state, out_rows))
        done = _round_robin(gens)
        tail = done[0] if g + 1 < ngroup else None
        ops = done[1] if g + 2 < ngroup else None
    for p in range(RWKV_WIDTH // hp):
        state_ref[:, p * hp:(p + 1) * hp] = state[p]


def _rwkv(rw, decay_bias, decay_up, aaa_bias, aaa_up, gate_up, k_k, k_a, r_k, ln_g, ln_b):
    b, t, _ = rw.shape
    tb = min(RWKV_TILE, t)
    w = RWKV_WIDTH

    def full(shape):
        return pl.BlockSpec(shape, lambda bi, ti: (0, 0))

    return pl.pallas_call(
        _rwkv_kernel,
        grid=(b, t // tb),
        in_specs=[
            pl.BlockSpec((1, tb, RWKV_IN_WIDTH), lambda bi, ti: (bi, ti, 0)),
            full((1, w)), full((LORA_DECAY, w)),
            full((1, w)), full((LORA_AAA, w)),
            full((LORA_GATE, w)),
            full((1, w)), full((1, w)), full((1, w)), full((1, w)), full((1, w)),
        ],
        out_specs=pl.BlockSpec((1, tb, w), lambda bi, ti: (bi, ti, 0)),
        out_shape=jax.ShapeDtypeStruct((b, t, w), BF16),
        scratch_shapes=[pltpu.VMEM((RWKV_HEAD_SIZE, w), F32)],
        compiler_params=pltpu.CompilerParams(
            dimension_semantics=("arbitrary", "arbitrary"), vmem_limit_bytes=VMEM_LIMIT),
        name="rwkv",
    )(rw, decay_bias, decay_up, aaa_bias, aaa_up, gate_up, k_k, k_a, r_k, ln_g, ln_b)


def _attn_merge_kernel(x_ref, q_ref, kv_ref, kvp_ref, rwkv_ref, g_ref, qg_ref, kg_ref, sink_ref,
                       wg_ref, wba_ref, wbr_ref, wo_ref, h_ref, attn_ref):
    x = x_ref[...]
    d = x.shape[1]
    nblk = q_ref.shape[1] // BLOCK
    units = [(j, kh) for j in range(nblk) for kh in range(ATTN_KV_HEADS)]
    n_u = len(units)
    gw = 2 * BLOCK
    n_g = (wg_ref.shape[1] - N_PROJ) // gw
    gate_parts = []

    def gate_slice():
        c0 = N_PROJ + len(gate_parts) * gw
        gate_parts.append(jnp.dot(ub, wg_ref[:, c0:c0 + gw], preferred_element_type=F32))

    y_rwkv = jnp.dot(rwkv_ref[...], wbr_ref[...], preferred_element_type=F32)
    ub = _rms(x, g_ref[...]).astype(BF16)
    gate_slice()
    kv_all = jnp.concatenate([kvp_ref[0], kv_ref[0]], axis=0)
    kdup, v_t = _attn_prep_kv(kv_all, qg_ref[...], kg_ref[...])
    gate_slice()
    qb, inv_rms = _attn_prep_q(q_ref[0])
    first_tile = pl.program_id(1) == 0
    scores = [_attn_scores(j, kh, qb, kdup, inv_rms, first_tile if j == 0 else None)
              for j, kh in units]
    soft = []
    for idx, (_, kh) in enumerate(units):
        while len(gate_parts) < 2 + (idx + 1) * (n_g - 2) // n_u:
            gate_slice()
        soft.append(_attn_softmax(kh, *scores[idx], sink_ref))
    while len(gate_parts) < n_g:
        gate_slice()
    gates = jnp.concatenate(gate_parts, axis=1)
    for unit, sm in zip(units, soft):
        _attn_values(*unit, *sm, v_t, attn_ref)
    y_attn = lax.dot_general(attn_ref[...], wba_ref[...], _TN, preferred_element_type=F32)
    mixed = _sigmoid(gates[:, :d]) * y_attn + _sigmoid(gates[:, d:]) * y_rwkv
    h_ref[...] = x + _bdot(mixed, wo_ref[...])


def _attn_merge(x2, q, kv, rwkv, gain, q_gain, k_gain, sinks, w_in, w_ba, w_br, w_out):
    n, d = x2.shape
    b, t, _ = q.shape
    tq = min(ATTN_TILE, t)
    nblk = tq // BLOCK
    nt = t // tq
    q_gain = jnp.tile(q_gain * (HEAD_DIM ** -0.5), (1, ATTN_KV_HEADS))
    k_gain = jnp.tile(k_gain, (1, ATTN_KV_HEADS))

    def full(a):
        return pl.BlockSpec(a.shape, lambda bi, i: (0, 0), pipeline_mode=pl.Buffered(1))

    def rows(width):
        return pl.BlockSpec((tq, width), lambda bi, i: (bi * nt + i, 0))

    return pl.pallas_call(
        _attn_merge_kernel,
        grid=(b, nt),
        in_specs=[
            rows(d),
            pl.BlockSpec((1, tq, ATTN_Q_WIDTH), lambda bi, i: (bi, i, 0)),
            pl.BlockSpec((1, tq, 2 * ATTN_KV_WIDTH), lambda bi, i: (bi, i, 0)),
            pl.BlockSpec((1, BLOCK, 2 * ATTN_KV_WIDTH),
                         lambda bi, i: (bi, jnp.maximum(i * nblk - 1, 0), 0)),
            rows(rwkv.shape[1]),
            full(gain), full(q_gain), full(k_gain),
            pl.BlockSpec(memory_space=pltpu.SMEM),
            full(w_in), full(w_ba), full(w_br), full(w_out),
        ],
        out_specs=rows(d),
        out_shape=jax.ShapeDtypeStruct((n, d), F32),
        scratch_shapes=[pltpu.VMEM((ATTN_Q_WIDTH, tq), BF16)],
        compiler_params=pltpu.CompilerParams(
            dimension_semantics=("arbitrary", "arbitrary"), vmem_limit_bytes=VMEM_LIMIT),
        name="attn_merge",
    )(x2, q, kv, kv, rwkv, gain, q_gain, k_gain, sinks, w_in, w_ba, w_br, w_out)


def _mlp_kernel(h_ref, g_ref, w1_ref, w2_ref, o_ref):
    h = h_ref[...]
    u = _rms(h, g_ref[...]).astype(BF16)
    d_ff = w1_ref.shape[1]
    acc = h
    for c0 in range(0, d_ff, MLP_CHUNK):
        hidden = jnp.square(jnp.maximum(
            jnp.dot(u, w1_ref[:, c0:c0 + MLP_CHUNK], preferred_element_type=F32), 0.0))
        acc = acc + jnp.dot(hidden.astype(BF16), w2_ref[c0:c0 + MLP_CHUNK, :],
                            preferred_element_type=F32)
    o_ref[...] = acc


def _mlp(h2, gain, w1, w2):
    n, d = h2.shape
    tm = min(2 * ROW_TILE, n)

    def full(a):
        return pl.BlockSpec(a.shape, lambda i: (0, 0), pipeline_mode=pl.Buffered(1))

    return pl.pallas_call(
        _mlp_kernel,
        grid=(n // tm,),
        in_specs=[pl.BlockSpec((tm, d), lambda i: (i, 0)), full(gain), full(w1), full(w2)],
        out_specs=pl.BlockSpec((tm, d), lambda i: (i, 0)),
        out_shape=jax.ShapeDtypeStruct((n, d), F32),
        compiler_params=pltpu.CompilerParams(
            dimension_semantics=("arbitrary",), vmem_limit_bytes=VMEM_LIMIT),
        name="mlp",
    )(h2, gain, w1, w2)


def _layer(h, norm1_gain, w_in, q_norm_gain, k_norm_gain, attn_sinks,
           mu_r, mu_k, mu_v, mu_w, mu_a, mu_g, decay_bias, decay_up, aaa_bias, aaa_up,
           gate_up, k_k, k_a, r_k, ln_x_gain, ln_x_bias, w_branch_attn, w_branch_rwkv,
           w_out, norm2_gain, w_ff_in, w_ff_out):
    b, t, d = h.shape
    x2 = h.reshape(b * t, d)
    row = lambda a: a.reshape(1, -1).astype(F32)
    w_in_b = w_in.astype(BF16)

    mu = jnp.concatenate([mu_r, mu_k, mu_v, mu_w, mu_a, mu_g]).reshape(1, -1).astype(F32)
    q, kv, rw = _inproj(x2, row(norm1_gain), w_in_b, mu, t)
    rwkv = _rwkv(rw.reshape(b, t, -1), row(decay_bias), decay_up.astype(BF16),
                 row(aaa_bias), aaa_up.astype(BF16), gate_up.astype(BF16),
                 row(k_k), row(k_a), row(r_k), row(ln_x_gain), row(ln_x_bias))
    h2 = _attn_merge(x2, q.reshape(b, t, -1), kv.reshape(b, t, -1), rwkv.reshape(b * t, -1),
                     row(norm1_gain), row(q_norm_gain), row(k_norm_gain), attn_sinks.astype(F32),
                     w_in_b, w_branch_attn.astype(BF16), w_branch_rwkv.astype(BF16),
                     w_out.astype(BF16))
    out = _mlp(h2, row(norm2_gain), w_ff_in.astype(BF16), w_ff_out.astype(BF16))
    return out.reshape(b, t, d)


def kernel(x, norm1_gain, w_in, q_norm_gain, k_norm_gain, attn_sinks, mu_r, mu_k, mu_v, mu_w, mu_a, mu_g, decay_bias, decay_up, aaa_bias, aaa_up, gate_up, k_k, k_a, r_k, ln_x_gain, ln_x_bias, w_branch_attn, w_branch_rwkv, w_out, norm2_gain, w_ff_in, w_ff_out):
    h = x.astype(F32)
    params = (norm1_gain, w_in, q_norm_gain, k_norm_gain, attn_sinks, mu_r, mu_k, mu_v, mu_w,
              mu_a, mu_g, decay_bias, decay_up, aaa_bias, aaa_up, gate_up, k_k, k_a, r_k,
              ln_x_gain, ln_x_bias, w_branch_attn, w_branch_rwkv, w_out, norm2_gain,
              w_ff_in, w_ff_out)
    for l in range(norm1_gain.shape[0]):
        h = _layer(h, *(p[l] for p in params))
    return h.astype(x.dtype)
```

```python
import functools
import math

import jax
import jax.numpy as jnp
from jax import lax
from jax.experimental import pallas as pl
from jax.experimental.pallas import tpu as pltpu

F32 = jnp.float32
BF16 = jnp.bfloat16

HEAD_DIM = 64
ATTN_Q_HEADS = 8
ATTN_KV_HEADS = 2
ATTN_GROUP = ATTN_Q_HEADS // ATTN_KV_HEADS
WINDOW = 128
BLOCK = 128
RWKV_HEADS = 8
RWKV_HEAD_SIZE = 64
LORA_DECAY = 64
LORA_AAA = 64
LORA_GATE = 128
ATTN_Q_WIDTH = ATTN_Q_HEADS * HEAD_DIM
ATTN_KV_WIDTH = ATTN_KV_HEADS * HEAD_DIM
RWKV_WIDTH = RWKV_HEADS * RWKV_HEAD_SIZE
RWKV_IN_WIDTH = 3 * RWKV_WIDTH + LORA_DECAY + LORA_AAA + LORA_GATE
N_PROJ = ATTN_Q_WIDTH + 2 * ATTN_KV_WIDTH + RWKV_IN_WIDTH
RMS_EPS = 1e-6
GN_EPS = 64e-5
L2_EPS = 1e-12

CHUNK = 64
ROW_TILE = 512
ATTN_TILE = 1024
RWKV_TILE = 2048
RWKV_GROUP = 256
MLP_CHUNK = 1024
VMEM_LIMIT = 56 * 1024 * 1024

_NT = (((1,), (1,)), ((), ()))
_TN = (((0,), (0,)), ((), ()))


def _bdot(a, b):
    return jnp.dot(a.astype(BF16), b.astype(BF16), preferred_element_type=F32)


def _rms(x, gain):
    return x * lax.rsqrt(jnp.mean(x * x, axis=-1, keepdims=True) + RMS_EPS) * gain


def _sigmoid(x):
    return 0.5 * jnp.tanh(0.5 * x) + 0.5


def _inproj_kernel(x_ref, g_ref, w_ref, mu_ref, q_ref, kv_ref, rw_ref, last_ref, *, tiles_per_seq):
    u = _rms(x_ref[...], g_ref[...]).astype(BF16)
    n_attn = ATTN_Q_WIDTH + 2 * ATTN_KV_WIDTH
    p = jnp.dot(u, w_ref[:, :n_attn], preferred_element_type=F32)
    q_ref[...] = p[:, :ATTN_Q_WIDTH]
    kv_ref[...] = p[:, ATTN_Q_WIDTH:]
    rw = jnp.dot(u, w_ref[:, n_attn:], preferred_element_type=F32)
    tm = rw.shape[0]
    seq_start = pl.program_id(0) % tiles_per_seq == 0
    before = jnp.where(seq_start, 0.0, last_ref[...])
    first = lax.broadcasted_iota(jnp.int32, (tm, 1), 0) == 0
    prev = jnp.where(first, before, pltpu.roll(rw, 1, 0))
    last_ref[...] = rw[tm - 1:tm, :]
    rw_ref[...] = rw + (prev - rw) * mu_ref[...]


def _inproj(x2, gain, w, mu, seq_len):
    n, d = x2.shape
    wid = N_PROJ
    tm = min(2 * ROW_TILE, seq_len)

    def full(shape):
        return pl.BlockSpec(shape, lambda i: (0, 0), pipeline_mode=pl.Buffered(1))

    return pl.pallas_call(
        functools.partial(_inproj_kernel, tiles_per_seq=seq_len // tm),
        grid=(n // tm,),
        in_specs=[
            pl.BlockSpec((tm, d), lambda i: (i, 0)),
            full((1, d)), full((d, wid)), full((1, RWKV_IN_WIDTH)),
        ],
        out_specs=[
            pl.BlockSpec((tm, ATTN_Q_WIDTH), lambda i: (i, 0)),
            pl.BlockSpec((tm, 2 * ATTN_KV_WIDTH), lambda i: (i, 0)),
            pl.BlockSpec((tm, RWKV_IN_WIDTH), lambda i: (i, 0)),
        ],
        out_shape=[
            jax.ShapeDtypeStruct((n, ATTN_Q_WIDTH), F32),
            jax.ShapeDtypeStruct((n, 2 * ATTN_KV_WIDTH), F32),
            jax.ShapeDtypeStruct((n, RWKV_IN_WIDTH), F32),
        ],
        scratch_shapes=[pltpu.VMEM((1, RWKV_IN_WIDTH), F32)],
        compiler_params=pltpu.CompilerParams(
            dimension_semantics=("arbitrary",), vmem_limit_bytes=VMEM_LIMIT),
        name="inproj",
    )(x2, gain, w, mu)


def _head_sumsq(x, ones_bd):
    x2 = x * x
    hi = x2.astype(BF16)
    lo = (x2 - hi.astype(F32)).astype(BF16)
    return (jnp.dot(hi, ones_bd, preferred_element_type=F32)
            + jnp.dot(lo, ones_bd, preferred_element_type=F32))


def _attn_prep_kv(kv_all, q_gain, k_gain):
    hp = 2 * HEAD_DIM
    even = lax.broadcasted_iota(jnp.int32, (1, hp), 1) < HEAD_DIM
    ri = lax.broadcasted_iota(jnp.int32, (hp, hp), 0)
    ci = lax.broadcasted_iota(jnp.int32, (hp, hp), 1)
    ones_bd = ((ri < HEAD_DIM) == (ci < HEAD_DIM)).astype(BF16)
    k = kv_all[:, :hp]
    kn = k * lax.rsqrt(_head_sumsq(k, ones_bd) * (1.0 / HEAD_DIM) + RMS_EPS) * k_gain * q_gain
    kn_sw = pltpu.roll(kn, HEAD_DIM, 1)
    kdup = [jnp.where(even, kn, kn_sw).astype(BF16), jnp.where(even, kn_sw, kn).astype(BF16)]
    v_t = kv_all[:, hp:].T.astype(BF16)
    return kdup, v_t


def _attn_prep_q(q):
    hp = 2 * HEAD_DIM
    si = lax.broadcasted_iota(jnp.int32, (16, hp), 0)
    sj = lax.broadcasted_iota(jnp.int32, (16, hp), 1)
    sel = (((si == 0) & (sj < HEAD_DIM)) | ((si == 1) & (sj >= HEAD_DIM))).astype(BF16)
    q2 = q * q
    hi = q2.astype(BF16)
    lo = (q2 - hi.astype(F32)).astype(BF16)
    inv_rms = []
    for m in range(ATTN_Q_WIDTH // hp):
        cs = slice(m * hp, (m + 1) * hp)
        ss = (lax.dot_general(sel, hi[:, cs], _NT, preferred_element_type=F32)
              + lax.dot_general(sel, lo[:, cs], _NT, preferred_element_type=F32))
        inv_rms.append(lax.rsqrt(ss * (1.0 / HEAD_DIM) + RMS_EPS))
    return q.astype(BF16), inv_rms


def _attn_scores(j, kh, qb, kdup, inv_rms, no_prev):
    assert WINDOW == BLOCK
    hp = 2 * HEAD_DIM
    even = lax.broadcasted_iota(jnp.int32, (1, hp), 1) < HEAD_DIM
    zero_q = jnp.zeros((BLOCK, hp), BF16)
    parts, scale = [], []
    for g in range(ATTN_GROUP):
        h = kh * ATTN_GROUP + g
        col = qb[j * BLOCK:(j + 1) * BLOCK, (h // 2) * hp:(h // 2 + 1) * hp]
        parts.append(jnp.where(even, col, zero_q) if h % 2 == 0 else jnp.where(even, zero_q, col))
        scale.append(inv_rms[h // 2][h % 2:h % 2 + 1, j * BLOCK:(j + 1) * BLOCK])
    s_t = lax.dot_general(kdup[kh][j * BLOCK:(j + 2) * BLOCK], jnp.concatenate(parts, axis=0), _NT,
                          preferred_element_type=F32)
    key = lax.broadcasted_iota(jnp.int32, (BLOCK, ATTN_GROUP * BLOCK), 0)
    qry = lax.broadcasted_iota(jnp.int32, (BLOCK, ATTN_GROUP * BLOCK), 1) & (BLOCK - 1)
    use_prev = key > qry
    prev = s_t[:BLOCK]
    if no_prev is not None:
        prev = jnp.where(no_prev, -jnp.inf, prev)
    folded = jnp.where(use_prev, prev, s_t[BLOCK:]) * jnp.concatenate(scale, axis=1)
    return folded, use_prev


def _attn_softmax(kh, folded, use_prev, sink_ref):
    sink = jnp.concatenate(
        [jnp.full((1, BLOCK), sink_ref[kh * ATTN_GROUP + g], F32) for g in range(ATTN_GROUP)], axis=1)
    m = jnp.maximum(jnp.max(folded, axis=0, keepdims=True), sink)
    p = jnp.exp(folded - m)
    den = jnp.sum(p, axis=0, keepdims=True) + jnp.exp(sink - m)
    zero = jnp.zeros_like(p)
    p_t = jnp.concatenate([jnp.where(use_prev, p, zero), jnp.where(use_prev, zero, p)], axis=0)
    return p_t.astype(BF16), 1.0 / den


def _attn_values(j, kh, p_t, inv_den, v_t, o_ref):
    o_t = jnp.dot(v_t[kh * HEAD_DIM:(kh + 1) * HEAD_DIM, j * BLOCK:(j + 2) * BLOCK], p_t,
                  preferred_element_type=F32) * inv_den
    for g in range(ATTN_GROUP):
        h = kh * ATTN_GROUP + g
        o_ref[h * HEAD_DIM:(h + 1) * HEAD_DIM, j * BLOCK:(j + 1) * BLOCK] = (
            o_t[:, g * BLOCK:(g + 1) * BLOCK].astype(o_ref.dtype))


def _pair_blockdiag(x, even):
    zero = jnp.zeros_like(x)
    return jnp.concatenate([jnp.where(even, x, zero), jnp.where(even, zero, x)], axis=0)


def _pair_sum(x, even):
    s_e = jnp.sum(jnp.where(even, x, 0.0), axis=-1, keepdims=True)
    s_o = jnp.sum(jnp.where(even, 0.0, x), axis=-1, keepdims=True)
    return jnp.where(even, s_e, s_o)


def _rwkv_prep(xs, prm):
    c = CHUNK
    hp = 2 * RWKV_HEAD_SIZE
    w = RWKV_WIDTH
    rows = xs.shape[0]
    nc = rows // c
    cols = [slice(p * hp, (p + 1) * hp) for p in range(w // hp)]
    r = xs[:, 0:w]
    k = xs[:, w:2 * w]
    v = xs[:, 2 * w:3 * w]
    w_lora = xs[:, 3 * w:3 * w + LORA_DECAY]
    a_lora = xs[:, 3 * w + LORA_DECAY:3 * w + LORA_DECAY + LORA_AAA]
    g_lora = xs[:, 3 * w + LORA_DECAY + LORA_AAA:]

    d = prm["dbias"][...] + _bdot(jnp.tanh(w_lora), prm["dup"][...])
    half = 0.5 * math.exp(-0.5)
    lw = -half * jnp.tanh(0.5 * d) - half
    a_sig = _sigmoid(prm["abias"][...] + _bdot(a_lora, prm["aup"][...]))
    gate = _bdot(_sigmoid(g_lora), prm["gup"][...])
    yield

    kkr = k * prm["kk"][...]
    ka = prm["ka"][...]
    k2 = k * (a_sig * ka + (1.0 - ka))
    rk = r * k2 * prm["rk"][...]
    even_t = lax.broadcasted_iota(jnp.int32, (rows, hp), 1) < RWKV_HEAD_SIZE
    den = jnp.concatenate(
        [jnp.maximum(jnp.sqrt(_pair_sum(kkr[:, cs] * kkr[:, cs], even_t)), L2_EPS) for cs in cols],
        axis=1)
    bonus = jnp.concatenate([_pair_sum(rk[:, cs], even_t) for cs in cols], axis=1) * v
    kk = kkr / den
    a_s = -kk
    b_s = kk * a_sig
    yield

    ti = lax.broadcasted_iota(jnp.int32, (c, 3 * c), 0)
    si = lax.broadcasted_iota(jnp.int32, (c, 3 * c), 1) & (c - 1)
    lower3 = (ti >= si).astype(BF16)
    lw_hi = lw.astype(BF16)
    lw_r = lw - lw_hi.astype(F32)
    lw_mid = lw_r.astype(BF16)
    lw_lo = (lw_r - lw_mid.astype(F32)).astype(BF16)
    cl = jnp.concatenate(
        [jnp.dot(lower3, jnp.concatenate([z[ci * c:(ci + 1) * c] for z in (lw_hi, lw_mid, lw_lo)], axis=0),
                 preferred_element_type=F32) for ci in range(nc)], axis=0)
    yield

    e_c = jnp.exp(cl)
    e_neg = 1.0 / e_c
    g_rows = [e_c[ci * c + c - 1:ci * c + c, :] for ci in range(nc)]
    g_full = jnp.concatenate([jnp.broadcast_to(g, (c, w)) for g in g_rows], axis=0)
    at = a_s * jnp.exp(cl - lw)
    rt = r * e_c
    yield
    bt = b_s * e_neg
    kt = k2 * e_neg
    ops = dict(rt=rt, g_rows=g_rows, gate=gate, bonus=bonus,
               bh=(bt * g_full).astype(BF16), khat=(kt * g_full).astype(BF16),
               at_b=at.astype(BF16), rt_b=rt.astype(BF16), bt_b=bt.astype(BF16),
               kt_b=kt.astype(BF16), v_b=v.astype(BF16))
    return ops


def _rwkv_solve(ops, prm):
    c = CHUNK
    hs = RWKV_HEAD_SIZE
    assert c == hs
    hp = 2 * hs
    w = RWKV_WIDTH
    npair = w // hp
    nc = ops["rt"].shape[0] // c
    cols = [slice(p * hp, (p + 1) * hp) for p in range(npair)]
    even = lax.broadcasted_iota(jnp.int32, (c, hp), 1) < RWKV_HEAD_SIZE
    gi = lax.broadcasted_iota(jnp.int32, (2 * c, 2 * hp), 0)
    g_t = gi & (c - 1)
    g_s = lax.broadcasted_iota(jnp.int32, (2 * c, 2 * hp), 1) & (c - 1)
    g_mask = (g_t > g_s) | ((gi >= c) & (g_t == g_s))
    eye2 = (lax.broadcasted_iota(jnp.int32, (c, hp), 0)
            == (lax.broadcasted_iota(jnp.int32, (c, hp), 1) & (c - 1)))
    units = [(ci, p) for ci in range(nc) for p in range(npair)]

    def col(name, ci, p):
        return ops[name][ci * c:(ci + 1) * c, cols[p]]

    def bd(x):
        return _pair_blockdiag(x, even)

    def lanes(*xs):
        return jnp.concatenate(xs, axis=1)

    def rows(*xs):
        return jnp.concatenate(xs, axis=0)

    def mm(a, b):
        return jnp.dot(a, b, preferred_element_type=F32)

    gm = [jnp.where(g_mask,
                    lax.dot_general(rows(col("at_b", ci, p), col("rt_b", ci, p)),
                                    rows(bd(col("bt_b", ci, p)), bd(col("kt_b", ci, p))), _NT,
                                    preferred_element_type=F32), 0.0)
          for ci, p in units]
    yield
    a_pow = [g[:c, :hp].astype(BF16) for g in gm]
    t_inv = [jnp.where(eye2, 1.0, g[:c, :hp]) for g in gm]
    a_pow = [mm(a, bd(a)).astype(BF16) for a in a_pow]
    v_st = [col("v_b", ci, p) for ci, p in units]
    akv = [mm(g[:c, hp:].astype(BF16), bd(v)).astype(BF16) for g, v in zip(gm, v_st)]
    yield
    for _ in range(int(math.log2(c)) - 2):
        res = [mm(a, lanes(bd(t.astype(BF16)), bd(a))) for a, t in zip(a_pow, t_inv)]
        t_inv = [t + z[:, :hp] for t, z in zip(t_inv, res)]
        a_pow = [z[:, hp:].astype(BF16) for z in res]
        yield
    t_inv = [t + mm(a, bd(t.astype(BF16))) for a, t in zip(a_pow, t_inv)]
    yield
    wu = [mm(t.astype(BF16), lanes(bd(col("at_b", ci, p)), bd(x)))
          for (ci, p), t, x in zip(units, t_inv, akv)]
    w_b = [z[:, :hp].astype(BF16) for z in wu]
    u0_b = [z[:, hp:].astype(BF16) for z in wu]
    yield
    def head_t(x):
        xt = x.T
        return lanes(xt[:hs], xt[hs:]).astype(BF16)

    ab = [mm(rows(g[c:, :].astype(BF16), lanes(head_t(col("bh", ci, p)), head_t(col("khat", ci, p)))),
             rows(lanes(bd(wb), bd(ub)), lanes(jnp.zeros((hp, hp), BF16), bd(v))))
          for (ci, p), g, wb, ub, v in zip(units, gm, w_b, u0_b, v_st)]
    ab_y = [z[:c] for z in ab]
    ab_h = [z[c:] for z in ab]
    yield

    def tail(state, out_rows):
        for ci in range(nc):
            y_cols = []
            for p in range(npair):
                idx = ci * npair + p
                qmat = col("rt", ci, p) + ab_y[idx][:, :hp]
                g_diag = jnp.where(eye2, jnp.broadcast_to(ops["g_rows"][ci][:, cols[p]], (hs, hp)), 0.0)
                mmat = g_diag + ab_h[idx][:, :hp]
                ys = mm(rows(qmat, mmat).astype(BF16), bd(state[p].astype(BF16)))
                y = ys[:c] + ab_y[idx][:, hp:]
                state[p] = ys[c:] + ab_h[idx][:, hp:]
                mean = _pair_sum(y, even) * (1.0 / RWKV_HEAD_SIZE)
                yc = y - mean
                var = _pair_sum(yc * yc, even) * (1.0 / RWKV_HEAD_SIZE)
                y_cols.append(yc * lax.rsqrt(var + GN_EPS))
            yn = jnp.concatenate(y_cols, axis=1)
            rs = slice(ci * c, (ci + 1) * c)
            out_rows(ci, (yn * prm["lng"][...] + prm["lnb"][...] + ops["bonus"][rs]) * ops["gate"][rs])
            yield

    return tail


def _round_robin(gens):
    values = [None] * len(gens)
    live = list(range(len(gens)))
    while live:
        for i in list(live):
            try:
                next(gens[i])
            except StopIteration as stop:
                values[i] = stop.value
                live.remove(i)
    return values


def _rwkv_kernel(xs_ref, dbias_ref, dup_ref, abias_ref, aup_ref, gup_ref,
                 kk_ref, ka_ref, rk_ref, lng_ref, lnb_ref, o_ref, state_ref):
    hp = 2 * RWKV_HEAD_SIZE
    tb = xs_ref.shape[1]
    gr = min(RWKV_GROUP, tb)
    ngroup = tb // gr
    prm = dict(dbias=dbias_ref, dup=dup_ref, abias=abias_ref, aup=aup_ref, gup=gup_ref,
               kk=kk_ref, ka=ka_ref, rk=rk_ref, lng=lng_ref, lnb=lnb_ref)

    @pl.when(pl.program_id(1) == 0)
    def _():
        state_ref[...] = jnp.zeros_like(state_ref)

    def prep(g):
        return _rwkv_prep(xs_ref[0, g * gr:(g + 1) * gr, :], prm)

    state = [state_ref[:, p * hp:(p + 1) * hp] for p in range(RWKV_WIDTH // hp)]
    (ops,) = _round_robin([prep(0)])
    done = _round_robin([_rwkv_solve(ops, prm)] + ([prep(1)] if ngroup > 1 else []))
    tail, ops = done[0], (done[1] if ngroup > 1 else None)
    for g in range(ngroup):
        def out_rows(ci, value, g=g):
            r0 = g * gr + ci * CHUNK
            o_ref[0, r0:r0 + CHUNK, :] = value.astype(o_ref.dtype)

        gens = []
        if g + 1 < ngroup:
            gens.append(_rwkv_solve(ops, prm))
        if g + 2 < ngroup:
            gens.append(prep(g + 2))
        gens.append(tail(state, out_rows))
        done = _round_robin(gens)
        tail = done[0] if g + 1 < ngroup else None
        ops = done[1] if g + 2 < ngroup else None
    for p in range(RWKV_WIDTH // hp):
        state_ref[:, p * hp:(p + 1) * hp] = state[p]


def _rwkv(rw, decay_bias, decay_up, aaa_bias, aaa_up, gate_up, k_k, k_a, r_k, ln_g, ln_b):
    b, t, _ = rw.shape
    tb = min(RWKV_TILE, t)
    w = RWKV_WIDTH

    def full(shape):
        return pl.BlockSpec(shape, lambda bi, ti: (0, 0))

    return pl.pallas_call(
        _rwkv_kernel,
        grid=(b, t // tb),
        in_specs=[
            pl.BlockSpec((1, tb, RWKV_IN_WIDTH), lambda bi, ti: (bi, ti, 0)),
            full((1, w)), full((LORA_DECAY, w)),
            full((1, w)), full((LORA_AAA, w)),
            full((LORA_GATE, w)),
            full((1, w)), full((1, w)), full((1, w)), full((1, w)), full((1, w)),
        ],
        out_specs=pl.BlockSpec((1, tb, w), lambda bi, ti: (bi, ti, 0)),
        out_shape=jax.ShapeDtypeStruct((b, t, w), BF16),
        scratch_shapes=[pltpu.VMEM((RWKV_HEAD_SIZE, w), F32)],
        compiler_params=pltpu.CompilerParams(
            dimension_semantics=("arbitrary", "arbitrary"), vmem_limit_bytes=VMEM_LIMIT),
        name="rwkv",
    )(rw, decay_bias, decay_up, aaa_bias, aaa_up, gate_up, k_k, k_a, r_k, ln_g, ln_b)


def _attn_merge_kernel(x_ref, q_ref, kv_ref, kvp_ref, rwkv_ref, g_ref, qg_ref, kg_ref, sink_ref,
                       wg_ref, wba_ref, wbr_ref, wo_ref, h_ref, attn_ref):
    x = x_ref[...]
    d = x.shape[1]
    nblk = q_ref.shape[1] // BLOCK
    units = [(j, kh) for j in range(nblk) for kh in range(ATTN_KV_HEADS)]
    n_u = len(units)
    gw = 2 * BLOCK
    n_g = (wg_ref.shape[1] - N_PROJ) // gw
    gate_parts = []

    def gate_slice():
        c0 = N_PROJ + len(gate_parts) * gw
        gate_parts.append(jnp.dot(ub, wg_ref[:, c0:c0 + gw], preferred_element_type=F32))

    y_rwkv = jnp.dot(rwkv_ref[...], wbr_ref[...], preferred_element_type=F32)
    ub = _rms(x, g_ref[...]).astype(BF16)
    gate_slice()
    kv_all = jnp.concatenate([kvp_ref[0], kv_ref[0]], axis=0)
    kdup, v_t = _attn_prep_kv(kv_all, qg_ref[...], kg_ref[...])
    gate_slice()
    qb, inv_rms = _attn_prep_q(q_ref[0])
    first_tile = pl.program_id(1) == 0
    scores = [_attn_scores(j, kh, qb, kdup, inv_rms, first_tile if j == 0 else None)
              for j, kh in units]
    soft = []
    for idx, (_, kh) in enumerate(units):
        while len(gate_parts) < 2 + (idx + 1) * (n_g - 2) // n_u:
            gate_slice()
        soft.append(_attn_softmax(kh, *scores[idx], sink_ref))
    while len(gate_parts) < n_g:
        gate_slice()
    gates = jnp.concatenate(gate_parts, axis=1)
    for unit, sm in zip(units, soft):
        _attn_values(*unit, *sm, v_t, attn_ref)
    y_attn = lax.dot_general(attn_ref[...], wba_ref[...], _TN, preferred_element_type=F32)
    mixed = _sigmoid(gates[:, :d]) * y_attn + _sigmoid(gates[:, d:]) * y_rwkv
    h_ref[...] = x + _bdot(mixed, wo_ref[...])


def _attn_merge(x2, q, kv, rwkv, gain, q_gain, k_gain, sinks, w_in, w_ba, w_br, w_out):
    n, d = x2.shape
    b, t, _ = q.shape
    tq = min(ATTN_TILE, t)
    nblk = tq // BLOCK
    nt = t // tq
    q_gain = jnp.tile(q_gain * (HEAD_DIM ** -0.5), (1, ATTN_KV_HEADS))
    k_gain = jnp.tile(k_gain, (1, ATTN_KV_HEADS))

    def full(a):
        return pl.BlockSpec(a.shape, lambda bi, i: (0, 0), pipeline_mode=pl.Buffered(1))

    def rows(width):
        return pl.BlockSpec((tq, width), lambda bi, i: (bi * nt + i, 0))

    return pl.pallas_call(
        _attn_merge_kernel,
        grid=(b, nt),
        in_specs=[
            rows(d),
            pl.BlockSpec((1, tq, ATTN_Q_WIDTH), lambda bi, i: (bi, i, 0)),
            pl.BlockSpec((1, tq, 2 * ATTN_KV_WIDTH), lambda bi, i: (bi, i, 0)),
            pl.BlockSpec((1, BLOCK, 2 * ATTN_KV_WIDTH),
                         lambda bi, i: (bi, jnp.maximum(i * nblk - 1, 0), 0)),
            rows(rwkv.shape[1]),
            full(gain), full(q_gain), full(k_gain),
            pl.BlockSpec(memory_space=pltpu.SMEM),
            full(w_in), full(w_ba), full(w_br), full(w_out),
        ],
        out_specs=rows(d),
        out_shape=jax.ShapeDtypeStruct((n, d), F32),
        scratch_shapes=[pltpu.VMEM((ATTN_Q_WIDTH, tq), BF16)],
        compiler_params=pltpu.CompilerParams(
            dimension_semantics=("arbitrary", "arbitrary"), vmem_limit_bytes=VMEM_LIMIT),
        name="attn_merge",
    )(x2, q, kv, kv, rwkv, gain, q_gain, k_gain, sinks, w_in, w_ba, w_br, w_out)


def _mlp_kernel(h_ref, g_ref, w1_ref, w2_ref, o_ref):
    h = h_ref[...]
    u = _rms(h, g_ref[...]).astype(BF16)
    d_ff = w1_ref.shape[1]
    acc = h
    for c0 in range(0, d_ff, MLP_CHUNK):
        hidden = jnp.square(jnp.maximum(
            jnp.dot(u, w1_ref[:, c0:c0 + MLP_CHUNK], preferred_element_type=F32), 0.0))
        acc = acc + jnp.dot(hidden.astype(BF16), w2_ref[c0:c0 + MLP_CHUNK, :],
                            preferred_element_type=F32)
    o_ref[...] = acc


def _mlp(h2, gain, w1, w2):
    n, d = h2.shape
    tm = min(2 * ROW_TILE, n)

    def full(a):
        return pl.BlockSpec(a.shape, lambda i: (0, 0), pipeline_mode=pl.Buffered(1))

    return pl.pallas_call(
        _mlp_kernel,
        grid=(n // tm,),
        in_specs=[pl.BlockSpec((tm, d), lambda i: (i, 0)), full(gain), full(w1), full(w2)],
        out_specs=pl.BlockSpec((tm, d), lambda i: (i, 0)),
        out_shape=jax.ShapeDtypeStruct((n, d), F32),
        compiler_params=pltpu.CompilerParams(
            dimension_semantics=("arbitrary",), vmem_limit_bytes=VMEM_LIMIT),
        name="mlp",
    )(h2, gain, w1, w2)


def _layer(h, norm1_gain, w_in, q_norm_gain, k_norm_gain, attn_sinks,
           mu_r, mu_k, mu_v, mu_w, mu_a, mu_g, decay_bias, decay_up, aaa_bias, aaa_up,
           gate_up, k_k, k_a, r_k, ln_x_gain, ln_x_bias, w_branch_attn, w_branch_rwkv,
           w_out, norm2_gain, w_ff_in, w_ff_out):
    b, t, d = h.shape
    x2 = h.reshape(b * t, d)
    row = lambda a: a.reshape(1, -1).astype(F32)
    w_in_b = w_in.astype(BF16)

    mu = jnp.concatenate([mu_r, mu_k, mu_v, mu_w, mu_a, mu_g]).reshape(1, -1).astype(F32)
    q, kv, rw = _inproj(x2, row(norm1_gain), w_in_b, mu, t)
    rwkv = _rwkv(rw.reshape(b, t, -1), row(decay_bias), decay_up.astype(BF16),
                 row(aaa_bias), aaa_up.astype(BF16), gate_up.astype(BF16),
                 row(k_k), row(k_a), row(r_k), row(ln_x_gain), row(ln_x_bias))
    h2 = _attn_merge(x2, q.reshape(b, t, -1), kv.reshape(b, t, -1), rwkv.reshape(b * t, -1),
                     row(norm1_gain), row(q_norm_gain), row(k_norm_gain), attn_sinks.astype(F32),
                     w_in_b, w_branch_attn.astype(BF16), w_branch_rwkv.astype(BF16),
                     w_out.astype(BF16))
    out = _mlp(h2, row(norm2_gain), w_ff_in.astype(BF16), w_ff_out.astype(BF16))
    return out.reshape(b, t, d)


def kernel(x, norm1_gain, w_in, q_norm_gain, k_norm_gain, attn_sinks, mu_r, mu_k, mu_v, mu_w, mu_a, mu_g, decay_bias, decay_up, aaa_bias, aaa_up, gate_up, k_k, k_a, r_k, ln_x_gain, ln_x_bias, w_branch_attn, w_branch_rwkv, w_out, norm2_gain, w_ff_in, w_ff_out):
    h = x.astype(F32)
    params = (norm1_gain, w_in, q_norm_gain, k_norm_gain, attn_sinks, mu_r, mu_k, mu_v, mu_w,
              mu_a, mu_g, decay_bias, decay_up, aaa_bias, aaa_up, gate_up, k_k, k_a, r_k,
              ln_x_gain, ln_x_bias, w_branch_attn, w_branch_rwkv, w_out, norm2_gain,
              w_ff_in, w_ff_out)
    for l in range(norm1_gain.shape[0]):
        h = _layer(h, *(p[l] for p in params))
    return h.astype(x.dtype)
```

```python
import functools
import math

import jax
import jax.numpy as jnp
from jax import lax
from jax.experimental import pallas as pl
from jax.experimental.pallas import tpu as pltpu

F32 = jnp.float32
BF16 = jnp.bfloat16

HEAD_DIM = 64
ATTN_Q_HEADS = 8
ATTN_KV_HEADS = 2
ATTN_GROUP = ATTN_Q_HEADS // ATTN_KV_HEADS
WINDOW = 128
BLOCK = 128
RWKV_HEADS = 8
RWKV_HEAD_SIZE = 64
LORA_DECAY = 64
LORA_AAA = 64
LORA_GATE = 128
ATTN_Q_WIDTH = ATTN_Q_HEADS * HEAD_DIM
ATTN_KV_WIDTH = ATTN_KV_HEADS * HEAD_DIM
RWKV_WIDTH = RWKV_HEADS * RWKV_HEAD_SIZE
RWKV_IN_WIDTH = 3 * RWKV_WIDTH + LORA_DECAY + LORA_AAA + LORA_GATE
N_PROJ = ATTN_Q_WIDTH + 2 * ATTN_KV_WIDTH + RWKV_IN_WIDTH
RMS_EPS = 1e-6
GN_EPS = 64e-5
L2_EPS = 1e-12

CHUNK = 64
ROW_TILE = 512
ATTN_TILE = 1024
RWKV_TILE = 1024
RWKV_GROUP = 256
MLP_CHUNK = 1024
VMEM_LIMIT = 56 * 1024 * 1024

_NT = (((1,), (1,)), ((), ()))
_TN = (((0,), (0,)), ((), ()))


def _bdot(a, b):
    return jnp.dot(a.astype(BF16), b.astype(BF16), preferred_element_type=F32)


def _rms(x, gain):
    return x * lax.rsqrt(jnp.mean(x * x, axis=-1, keepdims=True) + RMS_EPS) * gain


def _sigmoid(x):
    return 0.5 * jnp.tanh(0.5 * x) + 0.5


def _inproj_kernel(x_ref, g_ref, w_ref, mu_ref, q_ref, kv_ref, rw_ref, last_ref, *, tiles_per_seq):
    u = _rms(x_ref[...], g_ref[...]).astype(BF16)
    n_attn = ATTN_Q_WIDTH + 2 * ATTN_KV_WIDTH
    p = jnp.dot(u, w_ref[:, :n_attn], preferred_element_type=F32)
    q_ref[...] = p[:, :ATTN_Q_WIDTH]
    kv_ref[...] = p[:, ATTN_Q_WIDTH:]
    rw = jnp.dot(u, w_ref[:, n_attn:], preferred_element_type=F32)
    tm = rw.shape[0]
    seq_start = pl.program_id(0) % tiles_per_seq == 0
    before = jnp.where(seq_start, 0.0, last_ref[...])
    first = lax.broadcasted_iota(jnp.int32, (tm, 1), 0) == 0
    prev = jnp.where(first, before, pltpu.roll(rw, 1, 0))
    last_ref[...] = rw[tm - 1:tm, :]
    rw_ref[...] = rw + (prev - rw) * mu_ref[...]


def _inproj(x2, gain, w, mu, seq_len):
    n, d = x2.shape
    wid = N_PROJ
    tm = min(2 * ROW_TILE, seq_len)

    def full(shape):
        return pl.BlockSpec(shape, lambda i: (0, 0), pipeline_mode=pl.Buffered(1))

    return pl.pallas_call(
        functools.partial(_inproj_kernel, tiles_per_seq=seq_len // tm),
        grid=(n // tm,),
        in_specs=[
            pl.BlockSpec((tm, d), lambda i: (i, 0)),
            full((1, d)), full((d, wid)), full((1, RWKV_IN_WIDTH)),
        ],
        out_specs=[
            pl.BlockSpec((tm, ATTN_Q_WIDTH), lambda i: (i, 0)),
            pl.BlockSpec((tm, 2 * ATTN_KV_WIDTH), lambda i: (i, 0)),
            pl.BlockSpec((tm, RWKV_IN_WIDTH), lambda i: (i, 0)),
        ],
        out_shape=[
            jax.ShapeDtypeStruct((n, ATTN_Q_WIDTH), F32),
            jax.ShapeDtypeStruct((n, 2 * ATTN_KV_WIDTH), F32),
            jax.ShapeDtypeStruct((n, RWKV_IN_WIDTH), F32),
        ],
        scratch_shapes=[pltpu.VMEM((1, RWKV_IN_WIDTH), F32)],
        compiler_params=pltpu.CompilerParams(
            dimension_semantics=("arbitrary",), vmem_limit_bytes=VMEM_LIMIT),
        name="inproj",
    )(x2, gain, w, mu)


def _head_sumsq(x, ones_bd):
    x2 = x * x
    hi = x2.astype(BF16)
    lo = (x2 - hi.astype(F32)).astype(BF16)
    return (jnp.dot(hi, ones_bd, preferred_element_type=F32)
            + jnp.dot(lo, ones_bd, preferred_element_type=F32))


def _attn_prep_kv(kv_all, q_gain, k_gain):
    hp = 2 * HEAD_DIM
    even = lax.broadcasted_iota(jnp.int32, (1, hp), 1) < HEAD_DIM
    ri = lax.broadcasted_iota(jnp.int32, (hp, hp), 0)
    ci = lax.broadcasted_iota(jnp.int32, (hp, hp), 1)
    ones_bd = ((ri < HEAD_DIM) == (ci < HEAD_DIM)).astype(BF16)
    k = kv_all[:, :hp]
    kn = k * lax.rsqrt(_head_sumsq(k, ones_bd) * (1.0 / HEAD_DIM) + RMS_EPS) * k_gain * q_gain
    kn_sw = pltpu.roll(kn, HEAD_DIM, 1)
    kdup = [jnp.where(even, kn, kn_sw).astype(BF16), jnp.where(even, kn_sw, kn).astype(BF16)]
    v_t = kv_all[:, hp:].T.astype(BF16)
    return kdup, v_t


def _attn_prep_q(q):
    hp = 2 * HEAD_DIM
    si = lax.broadcasted_iota(jnp.int32, (16, hp), 0)
    sj = lax.broadcasted_iota(jnp.int32, (16, hp), 1)
    sel = (((si == 0) & (sj < HEAD_DIM)) | ((si == 1) & (sj >= HEAD_DIM))).astype(BF16)
    q2 = q * q
    hi = q2.astype(BF16)
    lo = (q2 - hi.astype(F32)).astype(BF16)
    inv_rms = []
    for m in range(ATTN_Q_WIDTH // hp):
        cs = slice(m * hp, (m + 1) * hp)
        ss = (lax.dot_general(sel, hi[:, cs], _NT, preferred_element_type=F32)
              + lax.dot_general(sel, lo[:, cs], _NT, preferred_element_type=F32))
        inv_rms.append(lax.rsqrt(ss * (1.0 / HEAD_DIM) + RMS_EPS))
    return q.astype(BF16), inv_rms


def _attn_scores(j, kh, qb, kdup, inv_rms, no_prev):
    assert WINDOW == BLOCK
    hp = 2 * HEAD_DIM
    even = lax.broadcasted_iota(jnp.int32, (1, hp), 1) < HEAD_DIM
    zero_q = jnp.zeros((BLOCK, hp), BF16)
    parts, scale = [], []
    for g in range(ATTN_GROUP):
        h = kh * ATTN_GROUP + g
        col = qb[j * BLOCK:(j + 1) * BLOCK, (h // 2) * hp:(h // 2 + 1) * hp]
        parts.append(jnp.where(even, col, zero_q) if h % 2 == 0 else jnp.where(even, zero_q, col))
        scale.append(inv_rms[h // 2][h % 2:h % 2 + 1, j * BLOCK:(j + 1) * BLOCK])
    s_t = lax.dot_general(kdup[kh][j * BLOCK:(j + 2) * BLOCK], jnp.concatenate(parts, axis=0), _NT,
                          preferred_element_type=F32)
    key = lax.broadcasted_iota(jnp.int32, (BLOCK, ATTN_GROUP * BLOCK), 0)
    qry = lax.broadcasted_iota(jnp.int32, (BLOCK, ATTN_GROUP * BLOCK), 1) & (BLOCK - 1)
    use_prev = key > qry
    prev = s_t[:BLOCK]
    if no_prev is not None:
        prev = jnp.where(no_prev, -jnp.inf, prev)
    folded = jnp.where(use_prev, prev, s_t[BLOCK:]) * jnp.concatenate(scale, axis=1)
    return folded, use_prev


def _attn_softmax(kh, folded, use_prev, sink_ref):
    sink = jnp.concatenate(
        [jnp.full((1, BLOCK), sink_ref[kh * ATTN_GROUP + g], F32) for g in range(ATTN_GROUP)], axis=1)
    m = jnp.maximum(jnp.max(folded, axis=0, keepdims=True), sink)
    p = jnp.exp(folded - m)
    den = jnp.sum(p, axis=0, keepdims=True) + jnp.exp(sink - m)
    zero = jnp.zeros_like(p)
    p_t = jnp.concatenate([jnp.where(use_prev, p, zero), jnp.where(use_prev, zero, p)], axis=0)
    return p_t.astype(BF16), 1.0 / den


def _attn_values(j, kh, p_t, inv_den, v_t, o_ref):
    o_t = jnp.dot(v_t[kh * HEAD_DIM:(kh + 1) * HEAD_DIM, j * BLOCK:(j + 2) * BLOCK], p_t,
                  preferred_element_type=F32) * inv_den
    for g in range(ATTN_GROUP):
        h = kh * ATTN_GROUP + g
        o_ref[h * HEAD_DIM:(h + 1) * HEAD_DIM, j * BLOCK:(j + 1) * BLOCK] = (
            o_t[:, g * BLOCK:(g + 1) * BLOCK].astype(o_ref.dtype))


def _pair_blockdiag(x, even):
    zero = jnp.zeros_like(x)
    return jnp.concatenate([jnp.where(even, x, zero), jnp.where(even, zero, x)], axis=0)


def _pair_sum(x, even):
    s_e = jnp.sum(jnp.where(even, x, 0.0), axis=-1, keepdims=True)
    s_o = jnp.sum(jnp.where(even, 0.0, x), axis=-1, keepdims=True)
    return jnp.where(even, s_e, s_o)


def _rwkv_prep(xs, prm):
    c = CHUNK
    hp = 2 * RWKV_HEAD_SIZE
    w = RWKV_WIDTH
    rows = xs.shape[0]
    nc = rows // c
    cols = [slice(p * hp, (p + 1) * hp) for p in range(w // hp)]
    r = xs[:, 0:w]
    k = xs[:, w:2 * w]
    v = xs[:, 2 * w:3 * w]
    w_lora = xs[:, 3 * w:3 * w + LORA_DECAY]
    a_lora = xs[:, 3 * w + LORA_DECAY:3 * w + LORA_DECAY + LORA_AAA]
    g_lora = xs[:, 3 * w + LORA_DECAY + LORA_AAA:]

    d = prm["dbias"][...] + _bdot(jnp.tanh(w_lora), prm["dup"][...])
    half = 0.5 * math.exp(-0.5)
    lw = -half * jnp.tanh(0.5 * d) - half
    a_sig = _sigmoid(prm["abias"][...] + _bdot(a_lora, prm["aup"][...]))
    gate = _bdot(_sigmoid(g_lora), prm["gup"][...])
    yield

    kkr = k * prm["kk"][...]
    ka = prm["ka"][...]
    k2 = k * (a_sig * ka + (1.0 - ka))
    rk = r * k2 * prm["rk"][...]
    even_t = lax.broadcasted_iota(jnp.int32, (rows, hp), 1) < RWKV_HEAD_SIZE
    den = jnp.concatenate(
        [jnp.maximum(jnp.sqrt(_pair_sum(kkr[:, cs] * kkr[:, cs], even_t)), L2_EPS) for cs in cols],
        axis=1)
    bonus = jnp.concatenate([_pair_sum(rk[:, cs], even_t) for cs in cols], axis=1) * v
    kk = kkr / den
    a_s = -kk
    b_s = kk * a_sig
    yield

    ti = lax.broadcasted_iota(jnp.int32, (c, 3 * c), 0)
    si = lax.broadcasted_iota(jnp.int32, (c, 3 * c), 1) & (c - 1)
    lower3 = (ti >= si).astype(BF16)
    lw_hi = lw.astype(BF16)
    lw_r = lw - lw_hi.astype(F32)
    lw_mid = lw_r.astype(BF16)
    lw_lo = (lw_r - lw_mid.astype(F32)).astype(BF16)
    cl = jnp.concatenate(
        [jnp.dot(lower3, jnp.concatenate([z[ci * c:(ci + 1) * c] for z in (lw_hi, lw_mid, lw_lo)], axis=0),
                 preferred_element_type=F32) for ci in range(nc)], axis=0)
    yield

    e_c = jnp.exp(cl)
    e_neg = 1.0 / e_c
    g_rows = [e_c[ci * c + c - 1:ci * c + c, :] for ci in range(nc)]
    g_full = jnp.concatenate([jnp.broadcast_to(g, (c, w)) for g in g_rows], axis=0)
    at = a_s * jnp.exp(cl - lw)
    rt = r * e_c
    yield
    bt = b_s * e_neg
    kt = k2 * e_neg
    ops = dict(g_rows=g_rows, gate=gate, bonus=bonus,
               bh=(bt * g_full).astype(BF16), khat=(kt * g_full).astype(BF16),
               at_b=at.astype(BF16), rt_b=rt.astype(BF16), bt_b=bt.astype(BF16),
               kt_b=kt.astype(BF16), v_b=v.astype(BF16))
    return ops


def _rwkv_solve(ops, prm):
    c = CHUNK
    hs = RWKV_HEAD_SIZE
    assert c == hs
    hp = 2 * hs
    w = RWKV_WIDTH
    npair = w // hp
    nc = ops["rt_b"].shape[0] // c
    cols = [slice(p * hp, (p + 1) * hp) for p in range(npair)]
    even = lax.broadcasted_iota(jnp.int32, (c, hp), 1) < RWKV_HEAD_SIZE
    gi = lax.broadcasted_iota(jnp.int32, (2 * c, 2 * hp), 0)
    g_t = gi & (c - 1)
    g_s = lax.broadcasted_iota(jnp.int32, (2 * c, 2 * hp), 1) & (c - 1)
    g_mask = (g_t > g_s) | ((gi >= c) & (g_t == g_s))
    eye2 = (lax.broadcasted_iota(jnp.int32, (c, hp), 0)
            == (lax.broadcasted_iota(jnp.int32, (c, hp), 1) & (c - 1)))
    units = [(ci, p) for ci in range(nc) for p in range(npair)]

    def col(name, ci, p):
        return ops[name][ci * c:(ci + 1) * c, cols[p]]

    def bd(x):
        return _pair_blockdiag(x, even)

    def lanes(*xs):
        return jnp.concatenate(xs, axis=1)

    def rows(*xs):
        return jnp.concatenate(xs, axis=0)

    def mm(a, b):
        return jnp.dot(a, b, preferred_element_type=F32)

    gm = [jnp.where(g_mask,
                    lax.dot_general(rows(col("at_b", ci, p), col("rt_b", ci, p)),
                                    rows(bd(col("bt_b", ci, p)), bd(col("kt_b", ci, p))), _NT,
                                    preferred_element_type=F32), 0.0)
          for ci, p in units]
    yield
    a_pow = [g[:c, :hp].astype(BF16) for g in gm]
    t_inv = [jnp.where(eye2, 1.0, g[:c, :hp]) for g in gm]
    a_pow = [mm(a, bd(a)).astype(BF16) for a in a_pow]
    v_st = [col("v_b", ci, p) for ci, p in units]
    akv = [mm(g[:c, hp:].astype(BF16), bd(v)).astype(BF16) for g, v in zip(gm, v_st)]
    yield
    for _ in range(int(math.log2(c)) - 2):
        res = [mm(a, lanes(bd(t.astype(BF16)), bd(a))) for a, t in zip(a_pow, t_inv)]
        t_inv = [t + z[:, :hp] for t, z in zip(t_inv, res)]
        a_pow = [z[:, hp:].astype(BF16) for z in res]
        yield
    t_inv = [t + mm(a, bd(t.astype(BF16))) for a, t in zip(a_pow, t_inv)]
    yield
    wu = [mm(t.astype(BF16), lanes(bd(col("at_b", ci, p)), bd(x)))
          for (ci, p), t, x in zip(units, t_inv, akv)]
    w_b = [z[:, :hp].astype(BF16) for z in wu]
    u0_b = [z[:, hp:].astype(BF16) for z in wu]
    yield
    def head_t(x):
        xt = x.T
        return lanes(xt[:hs], xt[hs:]).astype(BF16)

    ab = [mm(rows(g[c:, :].astype(BF16), lanes(head_t(col("bh", ci, p)), head_t(col("khat", ci, p)))),
             rows(lanes(bd(wb), bd(ub)), lanes(jnp.zeros((hp, hp), BF16), bd(v))))
          for (ci, p), g, wb, ub, v in zip(units, gm, w_b, u0_b, v_st)]
    ab_y = [z[:c] for z in ab]
    ab_h = [z[c:] for z in ab]
    yield

    def tail(state, out_rows):
        for ci in range(nc):
            y_cols = []
            for p in range(npair):
                idx = ci * npair + p
                qmat = col("rt_b", ci, p).astype(F32) + ab_y[idx][:, :hp]
                g_diag = jnp.where(eye2, jnp.broadcast_to(ops["g_rows"][ci][:, cols[p]], (hs, hp)), 0.0)
                mmat = g_diag + ab_h[idx][:, :hp]
                ys = mm(rows(qmat, mmat).astype(BF16), bd(state[p].astype(BF16)))
                y = ys[:c] + ab_y[idx][:, hp:]
                state[p] = ys[c:] + ab_h[idx][:, hp:]
                mean = _pair_sum(y, even) * (1.0 / RWKV_HEAD_SIZE)
                yc = y - mean
                var = _pair_sum(yc * yc, even) * (1.0 / RWKV_HEAD_SIZE)
                y_cols.append(yc * lax.rsqrt(var + GN_EPS))
            yn = jnp.concatenate(y_cols, axis=1)
            rs = slice(ci * c, (ci + 1) * c)
            out_rows(ci, (yn * prm["lng"][...] + prm["lnb"][...] + ops["bonus"][rs]) * ops["gate"][rs])
            yield

    return tail


def _round_robin(gens):
    values = [None] * len(gens)
    live = list(range(len(gens)))
    while live:
        for i in list(live):
            try:
                next(gens[i])
            except StopIteration as stop:
                values[i] = stop.value
                live.remove(i)
    return values


def _rwkv_kernel(xs_ref, dbias_ref, dup_ref, abias_ref, aup_ref, gup_ref,
                 kk_ref, ka_ref, rk_ref, lng_ref, lnb_ref, o_ref, state_ref):
    hp = 2 * RWKV_HEAD_SIZE
    tb = xs_ref.shape[1]
    gr = min(RWKV_GROUP, tb)
    ngroup = tb // gr
    prm = dict(dbias=dbias_ref, dup=dup_ref, abias=abias_ref, aup=aup_ref, gup=gup_ref,
               kk=kk_ref, ka=ka_ref, rk=rk_ref, lng=lng_ref, lnb=lnb_ref)

    @pl.when(pl.program_id(1) == 0)
    def _():
        state_ref[...] = jnp.zeros_like(state_ref)

    def prep(g):
        return _rwkv_prep(xs_ref[0, g * gr:(g + 1) * gr, :], prm)

    state = [state_ref[:, p * hp:(p + 1) * hp] for p in range(RWKV_WIDTH // hp)]
    (ops,) = _round_robin([prep(0)])
    done = _round_robin([_rwkv_solve(ops, prm)] + ([prep(1)] if ngroup > 1 else []))
    tail, ops = done[0], (done[1] if ngroup > 1 else None)
    for g in range(ngroup):
        def out_rows(ci, value, g=g):
            r0 = g * gr + ci * CHUNK
            o_ref[0, r0:r0 + CHUNK, :] = value.astype(o_ref.dtype)

        gens = []
        if g + 1 < ngroup:
            gens.append(_rwkv_solve(ops, prm))
        if g + 2 < ngroup:
            gens.append(prep(g + 2))
        gens.append(tail(state, out_rows))
        done = _round_robin(gens)
        tail = done[0] if g + 1 < ngroup else None
        ops = done[1] if g + 2 < ngroup else None
    for p in range(RWKV_WIDTH // hp):
        state_ref[:, p * hp:(p + 1) * hp] = state[p]


def _rwkv(rw, decay_bias, decay_up, aaa_bias, aaa_up, gate_up, k_k, k_a, r_k, ln_g, ln_b):
    b, t, _ = rw.shape
    tb = min(RWKV_TILE, t)
    w = RWKV_WIDTH

    def full(shape):
        return pl.BlockSpec(shape, lambda bi, ti: (0, 0))

    return pl.pallas_call(
        _rwkv_kernel,
        grid=(b, t // tb),
        in_specs=[
            pl.BlockSpec((1, tb, RWKV_IN_WIDTH), lambda bi, ti: (bi, ti, 0)),
            full((1, w)), full((LORA_DECAY, w)),
            full((1, w)), full((LORA_AAA, w)),
            full((LORA_GATE, w)),
            full((1, w)), full((1, w)), full((1, w)), full((1, w)), full((1, w)),
        ],
        out_specs=pl.BlockSpec((1, tb, w), lambda bi, ti: (bi, ti, 0)),
        out_shape=jax.ShapeDtypeStruct((b, t, w), BF16),
        scratch_shapes=[pltpu.VMEM((RWKV_HEAD_SIZE, w), F32)],
        compiler_params=pltpu.CompilerParams(
            dimension_semantics=("arbitrary", "arbitrary"), vmem_limit_bytes=VMEM_LIMIT),
        name="rwkv",
    )(rw, decay_bias, decay_up, aaa_bias, aaa_up, gate_up, k_k, k_a, r_k, ln_g, ln_b)


def _attn_merge_kernel(x_ref, q_ref, kv_ref, kvp_ref, rwkv_ref, g_ref, qg_ref, kg_ref, sink_ref,
                       wg_ref, wba_ref, wbr_ref, wo_ref, h_ref, attn_ref):
    x = x_ref[...]
    d = x.shape[1]
    nblk = q_ref.shape[1] // BLOCK
    units = [(j, kh) for j in range(nblk) for kh in range(ATTN_KV_HEADS)]
    n_u = len(units)
    gw = 2 * BLOCK
    n_g = (wg_ref.shape[1] - N_PROJ) // gw
    gate_parts = []

    def gate_slice():
        c0 = N_PROJ + len(gate_parts) * gw
        gate_parts.append(jnp.dot(ub, wg_ref[:, c0:c0 + gw], preferred_element_type=F32))

    y_rwkv = jnp.dot(rwkv_ref[...], wbr_ref[...], preferred_element_type=F32)
    ub = _rms(x, g_ref[...]).astype(BF16)
    gate_slice()
    kv_all = jnp.concatenate([kvp_ref[0], kv_ref[0]], axis=0)
    kdup, v_t = _attn_prep_kv(kv_all, qg_ref[...], kg_ref[...])
    gate_slice()
    qb, inv_rms = _attn_prep_q(q_ref[0])
    first_tile = pl.program_id(1) == 0
    scores = [_attn_scores(j, kh, qb, kdup, inv_rms, first_tile if j == 0 else None)
              for j, kh in units]
    soft = []
    for idx, (_, kh) in enumerate(units):
        while len(gate_parts) < 2 + (idx + 1) * (n_g - 2) // n_u:
            gate_slice()
        soft.append(_attn_softmax(kh, *scores[idx], sink_ref))
    while len(gate_parts) < n_g:
        gate_slice()
    gates = jnp.concatenate(gate_parts, axis=1)
    for unit, sm in zip(units, soft):
        _attn_values(*unit, *sm, v_t, attn_ref)
    y_attn = lax.dot_general(attn_ref[...], wba_ref[...], _TN, preferred_element_type=F32)
    mixed = _sigmoid(gates[:, :d]) * y_attn + _sigmoid(gates[:, d:]) * y_rwkv
    h_ref[...] = x + _bdot(mixed, wo_ref[...])


def _attn_merge(x2, q, kv, rwkv, gain, q_gain, k_gain, sinks, w_in, w_ba, w_br, w_out):
    n, d = x2.shape
    b, t, _ = q.shape
    tq = min(ATTN_TILE, t)
    nblk = tq // BLOCK
    nt = t // tq
    q_gain = jnp.tile(q_gain * (HEAD_DIM ** -0.5), (1, ATTN_KV_HEADS))
    k_gain = jnp.tile(k_gain, (1, ATTN_KV_HEADS))

    def full(a):
        return pl.BlockSpec(a.shape, lambda bi, i: (0, 0), pipeline_mode=pl.Buffered(1))

    def rows(width):
        return pl.BlockSpec((tq, width), lambda bi, i: (bi * nt + i, 0))

    return pl.pallas_call(
        _attn_merge_kernel,
        grid=(b, nt),
        in_specs=[
            rows(d),
            pl.BlockSpec((1, tq, ATTN_Q_WIDTH), lambda bi, i: (bi, i, 0)),
            pl.BlockSpec((1, tq, 2 * ATTN_KV_WIDTH), lambda bi, i: (bi, i, 0)),
            pl.BlockSpec((1, BLOCK, 2 * ATTN_KV_WIDTH),
                         lambda bi, i: (bi, jnp.maximum(i * nblk - 1, 0), 0)),
            rows(rwkv.shape[1]),
            full(gain), full(q_gain), full(k_gain),
            pl.BlockSpec(memory_space=pltpu.SMEM),
            full(w_in), full(w_ba), full(w_br), full(w_out),
        ],
        out_specs=rows(d),
        out_shape=jax.ShapeDtypeStruct((n, d), F32),
        scratch_shapes=[pltpu.VMEM((ATTN_Q_WIDTH, tq), BF16)],
        compiler_params=pltpu.CompilerParams(
            dimension_semantics=("arbitrary", "arbitrary"), vmem_limit_bytes=VMEM_LIMIT),
        name="attn_merge",
    )(x2, q, kv, kv, rwkv, gain, q_gain, k_gain, sinks, w_in, w_ba, w_br, w_out)


def _mlp_kernel(h_ref, g_ref, w1_ref, w2_ref, o_ref):
    h = h_ref[...]
    u = _rms(h, g_ref[...]).astype(BF16)
    d_ff = w1_ref.shape[1]
    acc = h
    for c0 in range(0, d_ff, MLP_CHUNK):
        hidden = jnp.square(jnp.maximum(
            jnp.dot(u, w1_ref[:, c0:c0 + MLP_CHUNK], preferred_element_type=F32), 0.0))
        acc = acc + jnp.dot(hidden.astype(BF16), w2_ref[c0:c0 + MLP_CHUNK, :],
                            preferred_element_type=F32)
    o_ref[...] = acc


def _mlp(h2, gain, w1, w2):
    n, d = h2.shape
    tm = min(2 * ROW_TILE, n)

    def full(a):
        return pl.BlockSpec(a.shape, lambda i: (0, 0), pipeline_mode=pl.Buffered(1))

    return pl.pallas_call(
        _mlp_kernel,
        grid=(n // tm,),
        in_specs=[pl.BlockSpec((tm, d), lambda i: (i, 0)), full(gain), full(w1), full(w2)],
        out_specs=pl.BlockSpec((tm, d), lambda i: (i, 0)),
        out_shape=jax.ShapeDtypeStruct((n, d), F32),
        compiler_params=pltpu.CompilerParams(
            dimension_semantics=("arbitrary",), vmem_limit_bytes=VMEM_LIMIT),
        name="mlp",
    )(h2, gain, w1, w2)


def _layer(h, norm1_gain, w_in, q_norm_gain, k_norm_gain, attn_sinks,
           mu_r, mu_k, mu_v, mu_w, mu_a, mu_g, decay_bias, decay_up, aaa_bias, aaa_up,
           gate_up, k_k, k_a, r_k, ln_x_gain, ln_x_bias, w_branch_attn, w_branch_rwkv,
           w_out, norm2_gain, w_ff_in, w_ff_out):
    b, t, d = h.shape
    x2 = h.reshape(b * t, d)
    row = lambda a: a.reshape(1, -1).astype(F32)
    w_in_b = w_in.astype(BF16)

    mu = jnp.concatenate([mu_r, mu_k, mu_v, mu_w, mu_a, mu_g]).reshape(1, -1).astype(F32)
    q, kv, rw = _inproj(x2, row(norm1_gain), w_in_b, mu, t)
    rwkv = _rwkv(rw.reshape(b, t, -1), row(decay_bias), decay_up.astype(BF16),
                 row(aaa_bias), aaa_up.astype(BF16), gate_up.astype(BF16),
                 row(k_k), row(k_a), row(r_k), row(ln_x_gain), row(ln_x_bias))
    h2 = _attn_merge(x2, q.reshape(b, t, -1), kv.reshape(b, t, -1), rwkv.reshape(b * t, -1),
                     row(norm1_gain), row(q_norm_gain), row(k_norm_gain), attn_sinks.astype(F32),
                     w_in_b, w_branch_attn.astype(BF16), w_branch_rwkv.astype(BF16),
                     w_out.astype(BF16))
    out = _mlp(h2, row(norm2_gain), w_ff_in.astype(BF16), w_ff_out.astype(BF16))
    return out.reshape(b, t, d)


def kernel(x, norm1_gain, w_in, q_norm_gain, k_norm_gain, attn_sinks, mu_r, mu_k, mu_v, mu_w, mu_a, mu_g, decay_bias, decay_up, aaa_bias, aaa_up, gate_up, k_k, k_a, r_k, ln_x_gain, ln_x_bias, w_branch_attn, w_branch_rwkv, w_out, norm2_gain, w_ff_in, w_ff_out):
    h = x.astype(F32)
    params = (norm1_gain, w_in, q_norm_gain, k_norm_gain, attn_sinks, mu_r, mu_k, mu_v, mu_w,
              mu_a, mu_g, decay_bias, decay_up, aaa_bias, aaa_up, gate_up, k_k, k_a, r_k,
              ln_x_gain, ln_x_bias, w_branch_attn, w_branch_rwkv, w_out, norm2_gain,
              w_ff_in, w_ff_out)
    for l in range(norm1_gain.shape[0]):
        h = _layer(h, *(p[l] for p in params))
    return h.astype(x.dtype)
```
